```python
import math
import jax, jax.numpy as jnp
from jax import lax
import numpy as np

D_MODEL = 1024
BATCH = 8
SEQ = 4096
DEPTH = 4

EPS = 1e-6
HG_HEADS = 4
HG_DK = 128
HG_DV = 128
HG_WIDTH = HG_HEADS * HG_DV
HG_CHUNK = 64
MLA_HEADS = 8
MLA_NOPE = 64
MLA_ROPE = 32
MLA_V = 64
MLA_Q_LORA = 384
MLA_KV_LORA = 256
MLA_WIDTH = MLA_HEADS * MLA_V
ROPE_THETA = 10000.0
ATTN_BLOCK = 128
SSM_HEADS = 8
SSM_HEAD_DIM = 64
SSM_WIDTH = SSM_HEADS * SSM_HEAD_DIM
SSM_GROUPS = 2
SSM_HPG = SSM_HEADS // SSM_GROUPS
SSM_STATE = 64
SSM_CONV = 4
SSM_CHUNK = 128
SSM_CONV_DIM = SSM_WIDTH + 2 * SSM_GROUPS * SSM_STATE
D_FF = 4 * D_MODEL
N_BRANCH = 3
COL_SIZES = (HG_HEADS * HG_DK, HG_HEADS * HG_DK, HG_WIDTH, HG_WIDTH,
             MLA_Q_LORA, MLA_KV_LORA, MLA_ROPE,
             SSM_WIDTH, SSM_CONV_DIM, SSM_HEADS,
             N_BRANCH * D_MODEL)
IN_COLS = sum(COL_SIZES)

kernel_name = "hybrid_hgrn2_mla_mamba2_gated_block"


def rmsnorm(x, g):
    xf = x.astype(jnp.float32)
    y = xf * lax.rsqrt(jnp.mean(xf * xf, axis=-1, keepdims=True) + EPS)
    return (y * g.astype(jnp.float32)).astype(x.dtype)


def segsum(a):
    L = a.shape[-1]
    cs = jnp.cumsum(a, axis=-1)
    diff = cs[..., :, None] - cs[..., None, :]
    return jnp.where(jnp.tril(jnp.ones((L, L), bool)), diff, -jnp.inf)


def hgrn2_mixer(q, f, i, g, lb, norm_g):
    Bsz, T, _ = q.shape
    L = HG_CHUNK
    nC = T // L
    dt = q.dtype
    q = jax.nn.silu(q)
    ff = f.astype(jnp.float32)
    lb = lb.astype(jnp.float32)
    log_f = jnp.logaddexp(jnp.log(lb), jnp.log1p(-lb) + jax.nn.log_sigmoid(ff))
    k = (1.0 - lb) * jax.nn.sigmoid(-ff)

    def chunks(t, d):
        return t.reshape(Bsz, nC, L, HG_HEADS, d).transpose(1, 0, 3, 2, 4)

    qc = chunks(q, HG_DK)
    kc = chunks(k, HG_DK)
    vc = chunks(i, HG_DV)
    bc = jnp.cumsum(chunks(log_f, HG_DK), axis=3)
    causal = jnp.tril(jnp.ones((L, L), bool))[:, :, None]

    def step(S, inp):
        qk, kk, vk, bk = inp
        diff = bk[:, :, :, None, :] - bk[:, :, None, :, :]
        decay = jnp.exp(jnp.where(causal, diff, -jnp.inf))
        attn = jnp.einsum('bhik,bhijk,bhjk->bhij', qk, decay, kk)
        o = (jnp.einsum('bhij,bhjv->bhiv', attn, vk)
             + jnp.einsum('bhik,bhkv->bhiv', qk * jnp.exp(bk), S))
        blast = bk[:, :, -1:, :]
        S = (jnp.exp(blast[:, :, 0, :])[..., None] * S
             + jnp.einsum('bhjk,bhjv->bhkv', kk * jnp.exp(blast - bk), vk))
        return S, o.astype(jnp.float32)

    S0 = jnp.zeros((Bsz, HG_HEADS, HG_DK, HG_DV), jnp.float32)
    _, o = lax.scan(step, S0, (qc, kc, vc, bc))
    o = o.transpose(1, 0, 3, 2, 4).reshape(Bsz, T, HG_HEADS, HG_DV)
    o = rmsnorm(o, norm_g.reshape(HG_HEADS, HG_DV)).reshape(Bsz, T, HG_WIDTH)
    return (o * jax.nn.silu(g.astype(jnp.float32))).astype(dt)


def rope_cos_sin(pos):
    inv = ROPE_THETA ** (-jnp.arange(0, MLA_ROPE, 2, dtype=jnp.float32) / MLA_ROPE)
    ang = pos.astype(jnp.float32)[..., None] * inv
    return jnp.cos(ang), jnp.sin(ang)


def apply_rope(x, cos, sin):
    x1, x2 = jnp.split(x, 2, axis=-1)
    c = cos[:, :, None, :].astype(x.dtype)
    s = sin[:, :, None, :].astype(x.dtype)
    return jnp.concatenate([x1 * c - x2 * s, x2 * c + x1 * s], axis=-1)


def mla_mixer(cq, ckv, kr, positions, q_norm, kv_norm, w_uq, w_ukv):
    Bsz, T, _ = cq.shape
    cq = rmsnorm(cq, q_norm)
    ckv = rmsnorm(ckv, kv_norm)
    q = (cq @ w_uq).reshape(Bsz, T, MLA_HEADS, MLA_NOPE + MLA_ROPE)
    kv = (ckv @ w_ukv).reshape(Bsz, T, MLA_HEADS, MLA_NOPE + MLA_V)
    q_nope, q_rope = q[..., :MLA_NOPE], q[..., MLA_NOPE:]
    k_nope, v = kv[..., :MLA_NOPE], kv[..., MLA_NOPE:]
    cos, sin = rope_cos_sin(positions)
    q_rope = apply_rope(q_rope, cos, sin)
    k_rope = apply_rope(kr[:, :, None, :], cos, sin)[:, :, 0, :]
    scale = (MLA_NOPE + MLA_ROPE) ** -0.5
    nb = T // ATTN_BLOCK
    qn_b = q_nope.reshape(Bsz, nb, ATTN_BLOCK, MLA_HEADS, MLA_NOPE).transpose(1, 0, 2, 3, 4)
    qr_b = q_rope.reshape(Bsz, nb, ATTN_BLOCK, MLA_HEADS, MLA_ROPE).transpose(1, 0, 2, 3, 4)
    kpos = jnp.arange(T)

    def block(args):
        qn, qr, start = args
        s = (jnp.einsum('bqhd,bkhd->bhqk', qn, k_nope)
             + jnp.einsum('bqhr,bkr->bhqk', qr, k_rope)).astype(jnp.float32) * scale
        qpos = start + jnp.arange(ATTN_BLOCK)
        s = jnp.where(kpos[None, :] <= qpos[:, None], s, -jnp.inf)
        p = jax.nn.softmax(s, axis=-1)
        return jnp.einsum('bhqk,bkhd->bqhd', p.astype(v.dtype), v)

    o = lax.map(block, (qn_b, qr_b, jnp.arange(nb) * ATTN_BLOCK))
    return o.transpose(1, 0, 2, 3, 4).reshape(Bsz, T, MLA_WIDTH)


def mamba2_mixer(z, xbc, dt_raw, conv_w, conv_b, dt_bias, a_log, d_skip, norm_g):
    Bsz, T, _ = xbc.shape
    L = SSM_CHUNK
    c = T // L
    G, J, P, N = SSM_GROUPS, SSM_HPG, SSM_HEAD_DIM, SSM_STATE
    out_dtype = xbc.dtype
    xbc = lax.conv_general_dilated(xbc, conv_w[:, None, :], window_strides=(1,),
                                   padding=[(SSM_CONV - 1, 0)],
                                   dimension_numbers=('NWC', 'WIO', 'NWC'),
                                   feature_group_count=SSM_CONV_DIM)
    xbc = jax.nn.silu(xbc + conv_b)
    xs = xbc[..., :SSM_WIDTH].reshape(Bsz, T, G, J, P)
    Bm = xbc[..., SSM_WIDTH:SSM_WIDTH + G * N].reshape(Bsz, c, L, G, N)
    Cm = xbc[..., SSM_WIDTH + G * N:].reshape(Bsz, c, L, G, N)
    dt = jax.nn.softplus(dt_raw.astype(jnp.float32) + dt_bias.astype(jnp.float32))
    A = -jnp.exp(a_log.astype(jnp.float32))
    a = (dt * A).reshape(Bsz, c, L, G, J).transpose(0, 3, 4, 1, 2)
    X = (xs * dt.reshape(Bsz, T, G, J)[..., None]).reshape(Bsz, c, L, G, J, P)
    a_cum = jnp.cumsum(a, axis=-1)
    Lmat = jnp.exp(segsum(a))
    y_diag = jnp.einsum('bclgn,bcsgn,bgjcls,bcsgjp->bclgjp', Cm, Bm, Lmat, X)
    decay_states = jnp.exp(a_cum[..., -1:] - a_cum)
    states = jnp.einsum('bclgn,bgjcl,bclgjp->bcgjpn', Bm, decay_states, X)
    states = jnp.concatenate([jnp.zeros_like(states[:, :1]), states], axis=1)
    chunk_tot = jnp.pad(a_cum[..., -1], ((0, 0), (0, 0), (0, 0), (1, 0)))
    decay_chunk = jnp.exp(segsum(chunk_tot))
    states = jnp.einsum('bgjzc,bcgjpn->bzgjpn', decay_chunk, states)[:, :-1]
    y_off = jnp.einsum('bclgn,bcgjpn,bgjcl->bclgjp', Cm, states, jnp.exp(a_cum))
    y = (y_diag + y_off).reshape(Bsz, T, G, J, P) + xs * d_skip.reshape(G, J)[..., None]
    y = y.reshape(Bsz, T, G, J * P) * jax.nn.silu(z.astype(jnp.float32)).reshape(Bsz, T, G, J * P)
    y = rmsnorm(y, norm_g.reshape(G, J * P))
    return y.reshape(Bsz, T, SSM_WIDTH).astype(out_dtype)


def setup_inputs(seed: int = 0) -> dict:
    key = jax.random.key(seed)
    ks = jax.random.split(key, 26)

    def nrm(k, shape, scale):
        return jax.random.normal(k, shape, jnp.float32) * scale

    def gain(k, shape):
        return 1.0 + 0.02 * jax.random.normal(k, shape, jnp.float32)

    dt0 = jnp.exp(jax.random.uniform(ks[13], (DEPTH, SSM_HEADS), jnp.float32,
                                     minval=math.log(1e-3), maxval=math.log(1e-1)))
    return {
        "x": nrm(ks[0], (BATCH, SEQ, D_MODEL), 1.0),
        "positions": jnp.broadcast_to(jnp.arange(SEQ, dtype=jnp.int32), (BATCH, SEQ)),
        "mix_norm": gain(ks[1], (DEPTH, D_MODEL)),
        "w_in": nrm(ks[2], (DEPTH, D_MODEL, IN_COLS), D_MODEL ** -0.5),
        "hg_lb_logits": nrm(ks[3], (DEPTH, HG_HEADS * HG_DK), 0.5),
        "hg_norm": gain(ks[4], (DEPTH, HG_WIDTH)),
        "mla_q_norm": gain(ks[5], (DEPTH, MLA_Q_LORA)),
        "mla_kv_norm": gain(ks[6], (DEPTH, MLA_KV_LORA)),
        "mla_w_uq": nrm(ks[7], (DEPTH, MLA_Q_LORA, MLA_HEADS * (MLA_NOPE + MLA_ROPE)), MLA_Q_LORA ** -0.5),
        "mla_w_ukv": nrm(ks[8], (DEPTH, MLA_KV_LORA, MLA_HEADS * (MLA_NOPE + MLA_V)), MLA_KV_LORA ** -0.5),
        "ssm_conv_w": nrm(ks[9], (DEPTH, SSM_CONV, SSM_CONV_DIM), SSM_CONV ** -0.5),
        "ssm_conv_b": nrm(ks[10], (DEPTH, SSM_CONV_DIM), 0.02),
        "ssm_dt_bias": dt0 + jnp.log(-jnp.expm1(-dt0)),
        "ssm_a_log": jnp.log(jax.random.uniform(ks[11], (DEPTH, SSM_HEADS), jnp.float32, minval=1.0, maxval=16.0)),
        "ssm_d": 1.0 + 0.1 * jax.random.normal(ks[12], (DEPTH, SSM_HEADS), jnp.float32),
        "ssm_norm": gain(ks[14], (DEPTH, SSM_WIDTH)),
        "w_br_hg": nrm(ks[15], (DEPTH, HG_WIDTH, D_MODEL), HG_WIDTH ** -0.5),
        "w_br_mla": nrm(ks[16], (DEPTH, MLA_WIDTH, D_MODEL), MLA_WIDTH ** -0.5),
        "w_br_ssm": nrm(ks[17], (DEPTH, SSM_WIDTH, D_MODEL), SSM_WIDTH ** -0.5),
        "w_out": nrm(ks[18], (DEPTH, D_MODEL, D_MODEL), D_MODEL ** -0.5),
        "mlp_norm": gain(ks[19], (DEPTH, D_MODEL)),
        "w_up": nrm(ks[20], (DEPTH, D_MODEL, D_FF), D_MODEL ** -0.5),
        "w_down": nrm(ks[21], (DEPTH, D_FF, D_MODEL), 0.5 * D_FF ** -0.5),
        "final_norm": gain(ks[22], (D_MODEL,)),
    }


def reference(x, positions, mix_norm, w_in, hg_lb_logits, hg_norm, mla_q_norm, mla_kv_norm,
              mla_w_uq, mla_w_ukv, ssm_conv_w, ssm_conv_b, ssm_dt_bias, ssm_a_log, ssm_d, ssm_norm,
              w_br_hg, w_br_mla, w_br_ssm, w_out, mlp_norm, w_up, w_down, final_norm):
    Bsz, T, _ = x.shape
    split_pts = [int(s) for s in np.cumsum(COL_SIZES)[:-1]]
    lbs = jnp.cumsum(jax.nn.softmax(hg_lb_logits.astype(jnp.float32), axis=0), axis=0)
    lbs = lbs - lbs[0:1]
    for l in range(DEPTH):
        h = rmsnorm(x, mix_norm[l])
        proj = h @ w_in[l]
        (hq, hf, hi, hg, cq, ckv, kr, sz, sxbc, sdt, gate_raw) = jnp.split(proj, split_pts, axis=-1)
        y_a = hgrn2_mixer(hq, hf, hi, hg, lbs[l], hg_norm[l])
        y_b = mla_mixer(cq, ckv, kr, positions, mla_q_norm[l], mla_kv_norm[l], mla_w_uq[l], mla_w_ukv[l])
        y_c = mamba2_mixer(sz, sxbc, sdt, ssm_conv_w[l], ssm_conv_b[l], ssm_dt_bias[l], ssm_a_log[l],
                           ssm_d[l], ssm_norm[l])
        gates = jax.nn.sigmoid(gate_raw).reshape(Bsz, T, N_BRANCH, D_MODEL)
        merged = (gates[:, :, 0] * (y_a @ w_br_hg[l])
                  + gates[:, :, 1] * (y_b @ w_br_mla[l])
                  + gates[:, :, 2] * (y_c @ w_br_ssm[l]))
        x = x + (merged @ w_out[l]).astype(x.dtype)
        h = rmsnorm(x, mlp_norm[l])
        x = x + (jnp.square(jax.nn.relu(h @ w_up[l])) @ w_down[l]).astype(x.dtype)
    return rmsnorm(x, final_norm)
```

```python
import functools
import math

import numpy as np
import jax
import jax.numpy as jnp
from jax import lax
from jax.experimental import pallas as pl
from jax.experimental.pallas import tpu as pltpu

F32 = jnp.float32
BF16 = jnp.bfloat16

D_MODEL = 1024
EPS = 1e-6
HG_HEADS = 4
HG_DK = 128
HG_DV = 128
HG_WIDTH = HG_HEADS * HG_DV
MLA_HEADS = 8
MLA_NOPE = 64
MLA_ROPE = 32
MLA_V = 64
MLA_Q_LORA = 384
MLA_KV_LORA = 256
MLA_WIDTH = MLA_HEADS * MLA_V
ROPE_THETA = 10000.0
SSM_HEADS = 8
SSM_HEAD_DIM = 64
SSM_WIDTH = SSM_HEADS * SSM_HEAD_DIM
SSM_GROUPS = 2
SSM_STATE = 64
SSM_CONV = 4
SSM_CONV_DIM = SSM_WIDTH + 2 * SSM_GROUPS * SSM_STATE
D_FF = 4 * D_MODEL
N_BRANCH = 3

LANES = 128
HEAD_PAD = 128

OFF_GATE = 0
OFF_HQ, OFF_HF, OFF_HI, OFF_HG = 3072, 3584, 4096, 4608
OFF_SZ = 5120
OFF_SDT = 5632
OFF_CQ = 5760
OFF_SXBC = 6144
OFF_CKV = 6912
OFF_KR = 7168
PROJ_COLS = 7296

HG_CHUNK = 64
HG_LEVELS = 6
SSD_CHUNK = 128

VMEM_LIMIT = 48 * 1024 * 1024


def _cparams(sem):
    return pltpu.CompilerParams(dimension_semantics=sem, vmem_limit_bytes=VMEM_LIMIT)


def _sigmoid(x):
    return jax.nn.sigmoid(x)


def _split3(x):
    hi = x.astype(BF16)
    r1 = x - hi.astype(F32)
    mid = r1.astype(BF16)
    r2 = r1 - mid.astype(F32)
    return hi, mid, r2.astype(BF16)


def _dot(a, b):
    return jnp.dot(a, b, preferred_element_type=F32)


def _dot_nt(a, b):
    return lax.dot_general(a, b, (((1,), (1,)), ((), ())), preferred_element_type=F32)


def _sel_rows(sel01, x):
    hi, mid, lo = _split3(x)
    return _dot(sel01, hi) + _dot(sel01, mid) + _dot(sel01, lo)


def _sel_cols(x, sel01):
    hi, mid, lo = _split3(x)
    return _dot(hi, sel01) + _dot(mid, sel01) + _dot(lo, sel01)


def _rms(x, g):
    ms = jnp.mean(x * x, axis=-1, keepdims=True)
    return x * lax.rsqrt(ms + EPS) * g


def _inproj_body(x_ref, g_ref, w_ref, o_ref):
    h = _rms(x_ref[...], g_ref[...]).astype(BF16)
    o_ref[...] = _dot(h, w_ref[...])


def _inproj(x2, g, w, tm, tn):
    m, d = x2.shape
    n = w.shape[1]
    return pl.pallas_call(
        _inproj_body,
        grid=(n // tn, m // tm),
        in_specs=[pl.BlockSpec((tm, d), lambda j, i: (i, 0)),
                  pl.BlockSpec((1, d), lambda j, i: (0, 0)),
                  pl.BlockSpec((d, tn), lambda j, i: (0, j))],
        out_specs=pl.BlockSpec((tm, tn), lambda j, i: (i, j)),
        out_shape=jax.ShapeDtypeStruct((m, n), F32),
        compiler_params=_cparams(("arbitrary", "arbitrary")),
        name="inproj",
    )(x2, g, w)


def _hgrn_level_matrix():
    c = HG_CHUNK
    mat = np.zeros((8, c, c), np.float32)
    for i in range(c):
        mat[0, i, : i + 1] = 1.0
        mat[7, i, i + 1:] = 1.0
    for lvl in range(HG_LEVELS):
        m = 1 << lvl
        for r in range(c):
            start = (r // (2 * m)) * 2 * m
            mid = start + m
            if r >= mid:
                mat[lvl + 1, r, mid: r + 1] = 1.0
            else:
                mat[lvl + 1, r, r + 1: mid] = 1.0
    return mat.reshape(8 * c, c)


def _hgrn_body(q_ref, f_ref, i_ref, g_ref, lb_ref, ng_ref, m_ref, o_ref, s_ref, *, n_chunks):
    c = HG_CHUNK

    @pl.when(pl.program_id(1) == 0)
    def _():
        s_ref[...] = jnp.zeros_like(s_ref)

    lb = lb_ref[...]
    log_lb = jnp.log(lb)
    log1m_lb = jnp.log1p(-lb)
    one_m_lb = 1.0 - lb
    ng = ng_ref[...]
    sel = m_ref[...]

    row = lax.broadcasted_iota(jnp.int32, (c, c), 0)
    col = lax.broadcasted_iota(jnp.int32, (c, c), 1)
    diag = row == col
    masks = []
    for lvl in range(HG_LEVELS):
        same_block = (row >> (lvl + 1)) == (col >> (lvl + 1))
        masks.append(same_block & (((row >> lvl) & 1) == 1) & (((col >> lvl) & 1) == 0))
    rowv = lax.broadcasted_iota(jnp.int32, (c, HG_DK), 0)
    second_half = [((rowv >> lvl) & 1) == 1 for lvl in range(HG_LEVELS)]

    def chunk(ci, carry):
        r0 = pl.multiple_of(ci * c, c)
        rows = pl.ds(r0, c)
        hq = q_ref[rows, :]
        ff = f_ref[rows, :]
        hv = i_ref[rows, :]
        hg = g_ref[rows, :]
        q = hq * _sigmoid(hq)
        log_sig = jnp.minimum(ff, 0.0) - jnp.log1p(jnp.exp(-jnp.abs(ff)))
        b = log1m_lb + log_sig
        lf = jnp.maximum(log_lb, b) + jnp.log1p(jnp.exp(-jnp.abs(log_lb - b)))
        k = one_m_lb * _sigmoid(-ff)
        gate = hg * _sigmoid(hg)
        ex = _sel_rows(sel, lf)

        for h in range(HG_HEADS):
            sl = slice(h * HG_DK, (h + 1) * HG_DK)
            qh, kh = q[:, sl], k[:, sl]
            eh = ex[:, sl]
            vh = hv[:, sl]
            vb = vh.astype(BF16)
            qb = qh.astype(BF16)
            kb = kh.astype(BF16)
            att = jnp.where(diag, _dot_nt(qb, kb), 0.0)
            for lvl in range(HG_LEVELS):
                e_l = eh[(lvl + 1) * c:(lvl + 2) * c]
                xl = (jnp.where(second_half[lvl], qh, kh) * jnp.exp(e_l)).astype(BF16)
                att = att + jnp.where(masks[lvl], _dot_nt(xl, xl), 0.0)
            e0 = eh[0:c]
            q_in = (qh * jnp.exp(e0)).astype(BF16)
            k_out = (kh * jnp.exp(eh[7 * c:8 * c])).astype(BF16)
            st = s_ref[h]
            o = _dot(att.astype(BF16), vb) + _dot_nt(q_in, st.astype(BF16))
            decay = jnp.exp(e0[c - 1:c, :])
            s_ref[h] = st * decay + _dot(vh.T.astype(BF16), k_out)
            y = _rms(o, ng[:, sl]) * gate[:, sl]
            o_ref[rows, sl] = y.astype(o_ref.dtype)
        return carry

    lax.fori_loop(0, n_chunks, chunk, 0)


def _hgrn(proj, lb, ng, bsz, t, tq):
    m = proj.shape[0]
    nq = t // tq
    sel = jnp.asarray(_hgrn_level_matrix(), BF16)
    w = HG_WIDTH

    def col_spec(off):
        blk = off // w
        return pl.BlockSpec((tq, w), lambda b, i: (b * nq + i, blk))

    vec = pl.BlockSpec((1, w), lambda b, i: (0, 0))
    return pl.pallas_call(
        functools.partial(_hgrn_body, n_chunks=tq // HG_CHUNK),
        grid=(bsz, nq),
        in_specs=[col_spec(OFF_HQ), col_spec(OFF_HF), col_spec(OFF_HI), col_spec(OFF_HG), vec, vec,
                  pl.BlockSpec(sel.shape, lambda b, i: (0, 0))],
        out_specs=pl.BlockSpec((tq, w), lambda b, i: (b * nq + i, 0)),
        out_shape=jax.ShapeDtypeStruct((m, w), BF16),
        scratch_shapes=[pltpu.VMEM((HG_HEADS, HG_DV, HG_DK), F32)],
        compiler_params=_cparams(("arbitrary", "arbitrary")),
        name="hgrn2",
    )(proj, proj, proj, proj, lb, ng, sel)


def _mla_prep_body(cq_ref, ckv_ref, kr_ref, cos_ref, sin_ref, qn_ref, kvn_ref, wq_ref, wkv_ref, vone_ref,
                   q_out, k_out, v_out):
    scale = (MLA_NOPE + MLA_ROPE) ** -0.5
    cqn = _rms(cq_ref[...], qn_ref[...]).astype(BF16)
    ckvn = _rms(ckv_ref[...], kvn_ref[...]).astype(BF16)
    qq = _dot(cqn, wq_ref[...])
    kv = _dot(ckvn, wkv_ref[...])
    cos = cos_ref[...]
    sin = sin_ref[...]
    lane = lax.broadcasted_iota(jnp.int32, (1, HEAD_PAD), 1)
    cq_t = scale * (cos + (lane < MLA_NOPE).astype(F32))
    sq_t = scale * sin
    krb = kr_ref[...]
    k_rope = krb * cos + pltpu.roll(krb, 64, 1) * sin
    hw = MLA_HEADS * HEAD_PAD
    vone = vone_ref[...]
    for h in range(MLA_HEADS):
        sl = slice(h * HEAD_PAD, (h + 1) * HEAD_PAD)
        sl2 = slice(hw + h * HEAD_PAD, hw + (h + 1) * HEAD_PAD)
        q_out[:, sl] = (qq[:, sl] * cq_t + qq[:, sl2] * sq_t).astype(BF16)
        k_out[:, sl] = (kv[:, sl] + k_rope).astype(BF16)
        v_out[:, sl] = (kv[:, sl2] + vone[:, sl]).astype(BF16)


def _mla_prep(proj, cos_t, sin_t, qn, kvn, wq, wkv, vone, tm):
    m = proj.shape[0]
    hw = MLA_HEADS * HEAD_PAD
    full = lambda a: pl.BlockSpec(a.shape, lambda i: (0, 0))
    out = jax.ShapeDtypeStruct((m, hw), BF16)
    ospec = pl.BlockSpec((tm, hw), lambda i: (i, 0))
    return pl.pallas_call(
        _mla_prep_body,
        grid=(m // tm,),
        in_specs=[pl.BlockSpec((tm, MLA_Q_LORA), lambda i: (i, OFF_CQ // MLA_Q_LORA)),
                  pl.BlockSpec((tm, MLA_KV_LORA), lambda i: (i, OFF_CKV // MLA_KV_LORA)),
                  pl.BlockSpec((tm, LANES), lambda i: (i, OFF_KR // LANES)),
                  pl.BlockSpec((tm, LANES), lambda i: (i, 0)),
                  pl.BlockSpec((tm, LANES), lambda i: (i, 0)),
                  full(qn), full(kvn), full(wq), full(wkv), full(vone)],
        out_specs=[ospec, ospec, ospec],
        out_shape=[out, out, out],
        compiler_params=_cparams(("arbitrary",)),
        name="mla_prep",
    )(proj, proj, proj, cos_t, sin_t, qn, kvn, wq, wkv, vone)


def _attn_body(q_ref, k_ref, v_ref, o_ref, *, tq):
    i = pl.program_id(2)
    lane = lax.broadcasted_iota(jnp.int32, (1, HEAD_PAD), 1)
    qpos = lax.broadcasted_iota(jnp.int32, (tq, tq), 0)
    kpos = lax.broadcasted_iota(jnp.int32, (tq, tq), 1)
    causal = kpos <= qpos
    outs = []
    for hh in range(2):
        sl = slice(hh * HEAD_PAD, (hh + 1) * HEAD_PAD)
        q = q_ref[:, sl]

        def step(j, carry, masked):
            m_prev, acc = carry
            rows = pl.ds(pl.multiple_of(j * tq, tq), tq)
            s = _dot_nt(q, k_ref[rows, sl])
            if masked:
                s = jnp.where(causal, s, -jnp.inf)
            m_new = jnp.maximum(m_prev, jnp.max(s, axis=-1, keepdims=True))
            p = jnp.exp(s - m_new)
            alpha = jnp.exp(m_prev - m_new)
            acc = alpha * acc + _dot(p.astype(BF16), v_ref[rows, sl])
            return m_new, acc

        init = (jnp.full((tq, 1), -jnp.inf, F32), jnp.zeros((tq, HEAD_PAD), F32))
        carry = lax.fori_loop(0, i, functools.partial(step, masked=False), init)
        _, acc = step(i, carry, True)
        den_lane = MLA_V if hh == 0 else 0
        den = jnp.sum(jnp.where(lane == den_lane, acc, 0.0), axis=-1, keepdims=True)
        outs.append(acc / den)
    o_ref[...] = jnp.where(lane < MLA_V, outs[0], outs[1]).astype(o_ref.dtype)


def _attn(q, k, v, bsz, t, tq):
    m = q.shape[0]
    nq = t // tq
    pw = 2 * HEAD_PAD
    return pl.pallas_call(
        functools.partial(_attn_body, tq=tq),
        grid=(bsz, MLA_HEADS // 2, nq),
        in_specs=[pl.BlockSpec((tq, pw), lambda b, h, i: (b * nq + i, h)),
                  pl.BlockSpec((t, pw), lambda b, h, i: (b, h)),
                  pl.BlockSpec((t, pw), lambda b, h, i: (b, h))],
        out_specs=pl.BlockSpec((tq, 2 * MLA_V), lambda b, h, i: (b * nq + i, h)),
        out_shape=jax.ShapeDtypeStruct((m, MLA_WIDTH), BF16),
        compiler_params=_cparams(("arbitrary", "arbitrary", "arbitrary")),
        name="mla_attn",
    )(q, k, v)


def _softplus(x):
    return jnp.maximum(x, 0.0) + jnp.log1p(jnp.exp(-jnp.abs(x)))


def _ssd_body(z_ref, dt_ref, xbc_ref, cw_ref, cb_ref, dtb_ref, alog_ref, dsk_ref, ng_ref, tril_ref, e8_ref,
              o_ref, xp_ref, xc_ref, s_ref, *, tq):
    L = SSD_CHUNK
    gn = SSM_GROUPS * SSM_STATE
    half = SSM_WIDTH // SSM_GROUPS
    t_idx = pl.program_id(1)

    @pl.when(t_idx == 0)
    def _():
        s_ref[...] = jnp.zeros_like(s_ref)
        xp_ref[0:8, :] = jnp.zeros((8, SSM_CONV_DIM), F32)

    @pl.when(t_idx > 0)
    def _():
        xp_ref[0:8, :] = xp_ref[tq:tq + 8, :]

    xp_ref[8:8 + tq, :] = xbc_ref[...]
    cw = cw_ref[...]
    acc = cb_ref[...] + cw[3:4, :] * xp_ref[8:8 + tq, :]
    for w in range(SSM_CONV - 1):
        acc = acc + cw[w:w + 1, :] * xp_ref[5 + w:5 + w + tq, :]
    xc_ref[...] = acc * _sigmoid(acc)

    a_neg = -jnp.exp(alog_ref[...])
    dtb = dtb_ref[...]
    dsk = dsk_ref[...]
    ng = ng_ref[...]
    tril = tril_ref[...]
    e8 = e8_ref[...]
    ri = lax.broadcasted_iota(jnp.int32, (L, L), 0)
    ci = lax.broadcasted_iota(jnp.int32, (L, L), 1)
    tri = ci <= ri
    lane_gn = lax.broadcasted_iota(jnp.int32, (1, gn), 1)
    lane_w = lax.broadcasted_iota(jnp.int32, (1, SSM_WIDTH), 1)
    row_gn = lax.broadcasted_iota(jnp.int32, (gn, SSM_WIDTH), 0)
    col_w = lax.broadcasted_iota(jnp.int32, (gn, SSM_WIDTH), 1)
    blockdiag = (row_gn // SSM_STATE) == (col_w // half)
    head_mask = [(lane_w // SSM_HEAD_DIM) == h for h in range(SSM_HEADS)]

    def chunk(c, carry):
        rows = pl.ds(pl.multiple_of(c * L, L), L)
        xc = xc_ref[rows, :]
        xs = xc[:, :SSM_WIDTH]
        bm = xc[:, SSM_WIDTH:SSM_WIDTH + gn]
        cm = xc[:, SSM_WIDTH + gn:]
        dt = _softplus(dt_ref[rows, :] + dtb)
        a = dt * a_neg
        acum = _sel_rows(tril, a)
        dt_e = _sel_cols(dt, e8)
        ac_e = _sel_cols(acum, e8)
        x_dt = xs * dt_e
        last = ac_e[L - 1:L, :]
        x_dec = (x_dt * jnp.exp(last - ac_e)).astype(BF16)
        x_b = x_dt.astype(BF16)
        bm_t = bm.T.astype(BF16)
        cb0 = _dot(jnp.where(lane_gn < SSM_STATE, cm, 0.0).astype(BF16), bm_t)
        cb1 = _dot(jnp.where(lane_gn >= SSM_STATE, cm, 0.0).astype(BF16), bm_t)
        ac_t = acum.T
        ws, xblk = [], []
        for h in range(SSM_HEADS):
            seg = acum[:, h:h + 1] - ac_t[h:h + 1, :]
            lmat = jnp.exp(jnp.where(tri, seg, -jnp.inf))
            ws.append((lmat * (cb0 if h < SSM_HEADS // SSM_GROUPS else cb1)).astype(BF16))
            xblk.append(jnp.where(head_mask[h], x_b, jnp.zeros_like(x_b)))
        y = _dot(jnp.concatenate(ws, axis=1), jnp.concatenate(xblk, axis=0))
        st = s_ref[...]
        y = y + jnp.exp(ac_e) * _dot(cm.astype(BF16), st.astype(BF16))
        s_ref[...] = jnp.exp(last) * st + jnp.where(blockdiag, _dot(bm_t, x_dec), 0.0)
        y = y + xs * dsk
        z = z_ref[rows, :]
        y = y * (z * _sigmoid(z))
        y0 = _rms(y[:, :half], ng[:, :half])
        y1 = _rms(y[:, half:], ng[:, half:])
        o_ref[rows, :half] = y0.astype(o_ref.dtype)
        o_ref[rows, half:] = y1.astype(o_ref.dtype)
        return carry

    lax.fori_loop(0, tq // L, chunk, 0)


def _ssd(proj, cw, cb, dtb, alog, dsk, ng, bsz, t, tq):
    m = proj.shape[0]
    nq = t // tq
    L = SSD_CHUNK
    tril = jnp.asarray(np.tril(np.ones((L, L), np.float32)), BF16)
    e8 = np.zeros((LANES, SSM_WIDTH), np.float32)
    for h in range(SSM_HEADS):
        e8[h, h * SSM_HEAD_DIM:(h + 1) * SSM_HEAD_DIM] = 1.0
    e8 = jnp.asarray(e8, BF16)
    full = lambda a: pl.BlockSpec(a.shape, lambda b, i: (0, 0))
    return pl.pallas_call(
        functools.partial(_ssd_body, tq=tq),
        grid=(bsz, nq),
        in_specs=[pl.BlockSpec((tq, SSM_WIDTH), lambda b, i: (b * nq + i, OFF_SZ // SSM_WIDTH)),
                  pl.BlockSpec((tq, LANES), lambda b, i: (b * nq + i, OFF_SDT // LANES)),
                  pl.BlockSpec((tq, SSM_CONV_DIM), lambda b, i: (b * nq + i, OFF_SXBC // SSM_CONV_DIM)),
                  full(cw), full(cb), full(dtb), full(alog), full(dsk), full(ng), full(tril), full(e8)],
        out_specs=pl.BlockSpec((tq, SSM_WIDTH), lambda b, i: (b * nq + i, 0)),
        out_shape=jax.ShapeDtypeStruct((m, SSM_WIDTH), BF16),
        scratch_shapes=[pltpu.VMEM((tq + 8, SSM_CONV_DIM), F32),
                        pltpu.VMEM((tq, SSM_CONV_DIM), F32),
                        pltpu.VMEM((SSM_GROUPS * SSM_STATE, SSM_WIDTH), F32)],
        compiler_params=_cparams(("arbitrary", "arbitrary")),
        name="ssd",
    )(proj, proj, proj, cw, cb, dtb, alog, dsk, ng, tril, e8)


def _merge_body(ya_ref, yb_ref, yc_ref, gate_ref, x_ref, wa_ref, wb_ref, wc_ref, wo_ref, o_ref):
    d = D_MODEL
    merged = (_sigmoid(gate_ref[:, 0:d]) * _dot(ya_ref[...], wa_ref[...])
              + _sigmoid(gate_ref[:, d:2 * d]) * _dot(yb_ref[...], wb_ref[...])
              + _sigmoid(gate_ref[:, 2 * d:3 * d]) * _dot(yc_ref[...], wc_ref[...]))
    o_ref[...] = x_ref[...] + _dot(merged.astype(BF16), wo_ref[...])


def _merge(ya, yb, yc, proj, x2, wa, wb, wc, wo, tm):
    m, d = x2.shape
    full = lambda a: pl.BlockSpec(a.shape, lambda i: (0, 0))
    row = lambda w: pl.BlockSpec((tm, w), lambda i: (i, 0))
    return pl.pallas_call(
        _merge_body,
        grid=(m // tm,),
        in_specs=[row(HG_WIDTH), row(MLA_WIDTH), row(SSM_WIDTH), row(N_BRANCH * d), row(d),
                  full(wa), full(wb), full(wc), full(wo)],
        out_specs=row(d),
        out_shape=jax.ShapeDtypeStruct((m, d), F32),
        compiler_params=_cparams(("arbitrary",)),
        name="merge_out",
    )(ya, yb, yc, proj, x2, wa, wb, wc, wo)


def _mlp_body(x_ref, g_ref, wu_ref, wd_ref, o_ref):
    x = x_ref[...]
    h = _rms(x, g_ref[...]).astype(BF16)
    u = jnp.maximum(_dot(h, wu_ref[...]), 0.0)
    o_ref[...] = x + _dot((u * u).astype(BF16), wd_ref[...])


def _mlp(x2, g, wu, wd, tm):
    m, d = x2.shape
    const = lambda a: pl.BlockSpec(a.shape, lambda i: (0, 0), pipeline_mode=pl.Buffered(1))
    return pl.pallas_call(
        _mlp_body,
        grid=(m // tm,),
        in_specs=[pl.BlockSpec((tm, d), lambda i: (i, 0)), const(g), const(wu), const(wd)],
        out_specs=pl.BlockSpec((tm, d), lambda i: (i, 0)),
        out_shape=jax.ShapeDtypeStruct((m, d), F32),
        compiler_params=_cparams(("arbitrary",)),
        name="mlp",
    )(x2, g, wu, wd)


def _final_norm_body(x_ref, g_ref, o_ref):
    o_ref[...] = _rms(x_ref[...], g_ref[...])


def _final_norm(x2, g, tm):
    m, d = x2.shape
    return pl.pallas_call(
        _final_norm_body,
        grid=(m // tm,),
        in_specs=[pl.BlockSpec((tm, d), lambda i: (i, 0)), pl.BlockSpec((1, d), lambda i: (0, 0))],
        out_specs=pl.BlockSpec((tm, d), lambda i: (i, 0)),
        out_shape=jax.ShapeDtypeStruct((m, d), F32),
        compiler_params=_cparams(("arbitrary",)),
        name="final_norm",
    )(x2, g)


def _rot_cols(w):
    hr = MLA_ROPE // 2
    return jnp.concatenate([-w[..., hr:], w[..., :hr]], axis=-1)


def _pack_w_in(w_in):
    dep, d, _ = w_in.shape
    sizes = (512, 512, 512, 512, MLA_Q_LORA, MLA_KV_LORA, MLA_ROPE, SSM_WIDTH, SSM_CONV_DIM, SSM_HEADS,
             N_BRANCH * D_MODEL)
    pts = np.cumsum(sizes)[:-1].tolist()
    hq, hf, hi, hg, cq, ckv, kr, sz, sxbc, sdt, gate = jnp.split(w_in, pts, axis=-1)
    z = lambda n: jnp.zeros((dep, d, n), w_in.dtype)
    kr_blk = jnp.concatenate([_rot_cols(kr), z(32), kr, z(32)], axis=-1)
    packed = jnp.concatenate([gate, hq, hf, hi, hg, sz, sdt, z(LANES - SSM_HEADS), cq, sxbc, ckv, kr_blk],
                             axis=-1)
    assert packed.shape[-1] == PROJ_COLS
    return packed.astype(BF16)


def _pack_w_uq(w):
    dep, r, _ = w.shape
    wh = w.reshape(dep, r, MLA_HEADS, MLA_NOPE + MLA_ROPE)
    nope, rope = wh[..., :MLA_NOPE], wh[..., MLA_NOPE:]
    z = lambda n: jnp.zeros((dep, r, MLA_HEADS, n), w.dtype)
    a = jnp.concatenate([nope, rope, z(32)], axis=-1).reshape(dep, r, MLA_HEADS * HEAD_PAD)
    b = jnp.concatenate([z(64), _rot_cols(rope), z(32)], axis=-1).reshape(dep, r, MLA_HEADS * HEAD_PAD)
    return jnp.concatenate([a, b], axis=-1).astype(BF16)


def _pack_w_ukv(w):
    dep, r, _ = w.shape
    wh = w.reshape(dep, r, MLA_HEADS // 2, 2, MLA_NOPE + MLA_V)
    kn, v = wh[..., :MLA_NOPE], wh[..., MLA_NOPE:]
    z = jnp.zeros_like(kn)
    ka = jnp.concatenate([kn, z], axis=-1).reshape(dep, r, MLA_HEADS * HEAD_PAD)
    va = jnp.stack([jnp.concatenate([v[..., 0, :], z[..., 0, :]], axis=-1),
                    jnp.concatenate([z[..., 1, :], v[..., 1, :]], axis=-1)], axis=-2)
    va = va.reshape(dep, r, MLA_HEADS * HEAD_PAD)
    return jnp.concatenate([ka, va], axis=-1).astype(BF16)


def _v_ones():
    v = np.zeros((1, MLA_HEADS * HEAD_PAD), np.float32)
    for h in range(MLA_HEADS):
        v[0, h * HEAD_PAD + (MLA_V if h % 2 == 0 else 0)] = 1.0
    return jnp.asarray(v)


def _rope_tables(positions):
    inv = ROPE_THETA ** (-jnp.arange(0, MLA_ROPE, 2, dtype=F32) / MLA_ROPE)
    ang = positions.astype(F32).reshape(-1, 1) * inv
    z = lambda n: jnp.zeros((ang.shape[0], n), F32)
    cos, sin = jnp.cos(ang), jnp.sin(ang)
    return (jnp.concatenate([z(64), cos, cos, z(32)], axis=-1),
            jnp.concatenate([z(64), sin, sin, z(32)], axis=-1))


def _row_tile(n, want):
    t = min(n, want)
    assert n % t == 0, (n, t)
    return t


def kernel(x, positions, mix_norm, w_in, hg_lb_logits, hg_norm, mla_q_norm, mla_kv_norm, mla_w_uq, mla_w_ukv,
           ssm_conv_w, ssm_conv_b, ssm_dt_bias, ssm_a_log, ssm_d, ssm_norm, w_br_hg, w_br_mla, w_br_ssm,
           w_out, mlp_norm, w_up, w_down, final_norm):
    bsz, t, d = x.shape
    depth = w_in.shape[0]
    assert d == D_MODEL and t % SSD_CHUNK == 0
    m = bsz * t
    tm = _row_tile(m, 512)
    tq = _row_tile(t, 512)

    lbs = jnp.cumsum(jax.nn.softmax(hg_lb_logits.astype(F32), axis=0), axis=0)
    lbs = lbs - lbs[0:1]
    w_in_p = _pack_w_in(w_in)
    wq_p = _pack_w_uq(mla_w_uq)
    wkv_p = _pack_w_ukv(mla_w_ukv)
    vone = _v_ones()
    cos_t, sin_t = _rope_tables(positions)
    pad_h = lambda a: jnp.pad(a.astype(F32), ((0, 0), (0, LANES - SSM_HEADS)))
    dtb_p, alog_p = pad_h(ssm_dt_bias), pad_h(ssm_a_log)
    dsk_e = jnp.repeat(ssm_d.astype(F32), SSM_HEAD_DIM, axis=-1)
    bf = lambda a: a.astype(BF16)
    wa, wb, wc, wo, wu, wd = bf(w_br_hg), bf(w_br_mla), bf(w_br_ssm), bf(w_out), bf(w_up), bf(w_down)
    r1 = lambda a: a.reshape(1, -1).astype(F32)

    x2 = x.reshape(m, d)
    for l in range(depth):
        proj = _inproj(x2, r1(mix_norm[l]), w_in_p[l], tm, PROJ_COLS // 3)
        ya = _hgrn(proj, r1(lbs[l]), r1(hg_norm[l]), bsz, t, tq)
        q, k, v = _mla_prep(proj, cos_t, sin_t, r1(mla_q_norm[l]), r1(mla_kv_norm[l]), wq_p[l], wkv_p[l],
                            vone, tm)
        yb = _attn(q, k, v, bsz, t, tq)
        yc = _ssd(proj, ssm_conv_w[l].astype(F32), r1(ssm_conv_b[l]), r1(dtb_p[l]), r1(alog_p[l]),
                  r1(dsk_e[l]), r1(ssm_norm[l]), bsz, t, tq)
        x2 = _merge(ya, yb, yc, proj, x2, wa[l], wb[l], wc[l], wo[l], tm)
        x2 = _mlp(x2, r1(mlp_norm[l]), wu[l], wd[l], tm)
    return _final_norm(x2, r1(final_norm), tm).reshape(bsz, t, d)
```

```python
import functools
import math

import numpy as np
import jax
import jax.numpy as jnp
from jax import lax
from jax.experimental import pallas as pl
from jax.experimental.pallas import tpu as pltpu

F32 = jnp.float32
BF16 = jnp.bfloat16

D_MODEL = 1024
EPS = 1e-6
HG_HEADS = 4
HG_DK = 128
HG_DV = 128
HG_WIDTH = HG_HEADS * HG_DV
MLA_HEADS = 8
MLA_NOPE = 64
MLA_ROPE = 32
MLA_V = 64
MLA_Q_LORA = 384
MLA_KV_LORA = 256
MLA_WIDTH = MLA_HEADS * MLA_V
ROPE_THETA = 10000.0
SSM_HEADS = 8
SSM_HEAD_DIM = 64
SSM_WIDTH = SSM_HEADS * SSM_HEAD_DIM
SSM_GROUPS = 2
SSM_STATE = 64
SSM_CONV = 4
SSM_CONV_DIM = SSM_WIDTH + 2 * SSM_GROUPS * SSM_STATE
D_FF = 4 * D_MODEL
N_BRANCH = 3

LANES = 128
HEAD_PAD = 128

OFF_GATE = 0
OFF_HQ, OFF_HF, OFF_HI, OFF_HG = 3072, 3584, 4096, 4608
OFF_SZ = 5120
OFF_SDT = 5632
OFF_CQ = 5760
OFF_SXBC = 6144
OFF_CKV = 6912
OFF_KR = 7168
PROJ_COLS = 7296

HG_CHUNK = 64
HG_LEVELS = 6
SSD_CHUNK = 128

VMEM_LIMIT = 48 * 1024 * 1024


def _cparams(sem):
    return pltpu.CompilerParams(dimension_semantics=sem, vmem_limit_bytes=VMEM_LIMIT)


def _sigmoid(x):
    return jax.nn.sigmoid(x)


def _split3(x):
    hi = x.astype(BF16)
    r1 = x - hi.astype(F32)
    mid = r1.astype(BF16)
    r2 = r1 - mid.astype(F32)
    return hi, mid, r2.astype(BF16)


def _dot(a, b):
    return jnp.dot(a, b, preferred_element_type=F32)


def _dot_nt(a, b):
    return lax.dot_general(a, b, (((1,), (1,)), ((), ())), preferred_element_type=F32)


N_SPLIT = 3


def _sel_rows(sel3, x):
    return _dot(sel3, jnp.concatenate(_split3(x), axis=0))


def _sel_cols(x, sel3):
    return _dot(jnp.concatenate(_split3(x), axis=1), sel3)


def _rms(x, g):
    ms = jnp.mean(x * x, axis=-1, keepdims=True)
    return x * lax.rsqrt(ms + EPS) * g


def _inproj_body(x_ref, g_ref, w_ref, o_ref):
    h = _rms(x_ref[...], g_ref[...]).astype(BF16)
    o_ref[...] = _dot(h, w_ref[...]).astype(o_ref.dtype)


def _inproj(x2, g, w, tm, tn):
    m, d = x2.shape
    n = w.shape[1]
    return pl.pallas_call(
        _inproj_body,
        grid=(n // tn, m // tm),
        in_specs=[pl.BlockSpec((tm, d), lambda j, i: (i, 0)),
                  pl.BlockSpec((1, d), lambda j, i: (0, 0)),
                  pl.BlockSpec((d, tn), lambda j, i: (0, j))],
        out_specs=pl.BlockSpec((tm, tn), lambda j, i: (i, j)),
        out_shape=jax.ShapeDtypeStruct((m, n), BF16),
        compiler_params=_cparams(("arbitrary", "arbitrary")),
        name="inproj",
    )(x2, g, w)


def _hgrn_level_matrix():
    c = HG_CHUNK
    mat = np.zeros((8, c, c), np.float32)
    for i in range(c):
        mat[0, i, : i + 1] = 1.0
        mat[7, i, i + 1:] = 1.0
    for lvl in range(HG_LEVELS):
        m = 1 << lvl
        for r in range(c):
            start = (r // (2 * m)) * 2 * m
            mid = start + m
            if r >= mid:
                mat[lvl + 1, r, mid: r + 1] = 1.0
            else:
                mat[lvl + 1, r, r + 1: mid] = 1.0
    return mat.reshape(8 * c, c)


def _hgrn_body(q_ref, f_ref, i_ref, g_ref, lb_ref, ng_ref, m_ref, o_ref, s_ref, *, n_chunks):
    c = HG_CHUNK

    @pl.when(pl.program_id(1) == 0)
    def _():
        s_ref[...] = jnp.zeros_like(s_ref)

    lb = lb_ref[...]
    log_lb = jnp.log(lb)
    log1m_lb = jnp.log1p(-lb)
    one_m_lb = 1.0 - lb
    ng = ng_ref[...]
    sel = m_ref[...]

    row = lax.broadcasted_iota(jnp.int32, (c, c), 0)
    col = lax.broadcasted_iota(jnp.int32, (c, c), 1)
    diag = row == col
    masks = []
    for lvl in range(HG_LEVELS):
        same_block = (row >> (lvl + 1)) == (col >> (lvl + 1))
        masks.append(same_block & (((row >> lvl) & 1) == 1) & (((col >> lvl) & 1) == 0))
    rowv = lax.broadcasted_iota(jnp.int32, (c, HG_DK), 0)
    second_half = [((rowv >> lvl) & 1) == 1 for lvl in range(HG_LEVELS)]

    def chunk(ci, carry):
        r0 = pl.multiple_of(ci * c, c)
        rows = pl.ds(r0, c)
        hq = q_ref[rows, :].astype(F32)
        ff = f_ref[rows, :].astype(F32)
        hv = i_ref[rows, :].astype(F32)
        hg = g_ref[rows, :].astype(F32)
        q = hq * _sigmoid(hq)
        log_sig = jnp.minimum(ff, 0.0) - jnp.log1p(jnp.exp(-jnp.abs(ff)))
        b = log1m_lb + log_sig
        lf = jnp.maximum(log_lb, b) + jnp.log1p(jnp.exp(-jnp.abs(log_lb - b)))
        k = one_m_lb * _sigmoid(-ff)
        gate = hg * _sigmoid(hg)
        ex = _sel_rows(sel, lf)

        for h in range(HG_HEADS):
            sl = slice(h * HG_DK, (h + 1) * HG_DK)
            qh, kh = q[:, sl], k[:, sl]
            eh = ex[:, sl]
            vh = hv[:, sl]
            vb = vh.astype(BF16)
            qb = qh.astype(BF16)
            kb = kh.astype(BF16)
            att = jnp.where(diag, _dot_nt(qb, kb), 0.0)
            for lvl in range(HG_LEVELS):
                e_l = eh[(lvl + 1) * c:(lvl + 2) * c]
                xl = (jnp.where(second_half[lvl], qh, kh) * jnp.exp(e_l)).astype(BF16)
                att = att + jnp.where(masks[lvl], _dot_nt(xl, xl), 0.0)
            e0 = eh[0:c]
            q_in = (qh * jnp.exp(e0)).astype(BF16)
            k_out = (kh * jnp.exp(eh[7 * c:8 * c])).astype(BF16)
            st = s_ref[h]
            o = _dot(att.astype(BF16), vb) + _dot_nt(q_in, st.astype(BF16))
            decay = jnp.exp(e0[c - 1:c, :])
            s_ref[h] = st * decay + _dot(vh.T.astype(BF16), k_out)
            y = _rms(o, ng[:, sl]) * gate[:, sl]
            o_ref[rows, sl] = y.astype(o_ref.dtype)
        return carry

    lax.fori_loop(0, n_chunks, chunk, 0)


def _hgrn(proj, lb, ng, bsz, t, tq):
    m = proj.shape[0]
    nq = t // tq
    sel = jnp.asarray(np.tile(_hgrn_level_matrix(), (1, N_SPLIT)), BF16)
    w = HG_WIDTH

    def col_spec(off):
        blk = off // w
        return pl.BlockSpec((tq, w), lambda b, i: (b * nq + i, blk))

    vec = pl.BlockSpec((1, w), lambda b, i: (0, 0))
    return pl.pallas_call(
        functools.partial(_hgrn_body, n_chunks=tq // HG_CHUNK),
        grid=(bsz, nq),
        in_specs=[col_spec(OFF_HQ), col_spec(OFF_HF), col_spec(OFF_HI), col_spec(OFF_HG), vec, vec,
                  pl.BlockSpec(sel.shape, lambda b, i: (0, 0))],
        out_specs=pl.BlockSpec((tq, w), lambda b, i: (b * nq + i, 0)),
        out_shape=jax.ShapeDtypeStruct((m, w), BF16),
        scratch_shapes=[pltpu.VMEM((HG_HEADS, HG_DV, HG_DK), F32)],
        compiler_params=_cparams(("arbitrary", "arbitrary")),
        name="hgrn2",
    )(proj, proj, proj, proj, lb, ng, sel)


def _mla_prep_body(cq_ref, ckv_ref, kr_ref, cos_ref, sin_ref, qn_ref, kvn_ref, wq_ref, wkv_ref, vone_ref,
                   q_out, k_out, v_out):
    scale = (MLA_NOPE + MLA_ROPE) ** -0.5
    cqn = _rms(cq_ref[...].astype(F32), qn_ref[...]).astype(BF16)
    ckvn = _rms(ckv_ref[...].astype(F32), kvn_ref[...]).astype(BF16)
    qq = _dot(cqn, wq_ref[...])
    kv = _dot(ckvn, wkv_ref[...])
    cos = cos_ref[...]
    sin = sin_ref[...]
    lane = lax.broadcasted_iota(jnp.int32, (1, HEAD_PAD), 1)
    cq_t = scale * (cos + (lane < MLA_NOPE).astype(F32))
    sq_t = scale * sin
    krb = kr_ref[...].astype(F32)
    k_rope = krb * cos + pltpu.roll(krb, 64, 1) * sin
    hw = MLA_HEADS * HEAD_PAD
    vone = vone_ref[...]
    for h in range(MLA_HEADS):
        sl = slice(h * HEAD_PAD, (h + 1) * HEAD_PAD)
        sl2 = slice(hw + h * HEAD_PAD, hw + (h + 1) * HEAD_PAD)
        q_out[:, sl] = (qq[:, sl] * cq_t + qq[:, sl2] * sq_t).astype(BF16)
        k_out[:, sl] = (kv[:, sl] + k_rope).astype(BF16)
        v_out[:, sl] = (kv[:, sl2] + vone[:, sl]).astype(BF16)


def _mla_prep(proj, cos_t, sin_t, qn, kvn, wq, wkv, vone, tm):
    m = proj.shape[0]
    hw = MLA_HEADS * HEAD_PAD
    full = lambda a: pl.BlockSpec(a.shape, lambda i: (0, 0))
    out = jax.ShapeDtypeStruct((m, hw), BF16)
    ospec = pl.BlockSpec((tm, hw), lambda i: (i, 0))
    return pl.pallas_call(
        _mla_prep_body,
        grid=(m // tm,),
        in_specs=[pl.BlockSpec((tm, MLA_Q_LORA), lambda i: (i, OFF_CQ // MLA_Q_LORA)),
                  pl.BlockSpec((tm, MLA_KV_LORA), lambda i: (i, OFF_CKV // MLA_KV_LORA)),
                  pl.BlockSpec((tm, LANES), lambda i: (i, OFF_KR // LANES)),
                  pl.BlockSpec((tm, LANES), lambda i: (i, 0)),
                  pl.BlockSpec((tm, LANES), lambda i: (i, 0)),
                  full(qn), full(kvn), full(wq), full(wkv), full(vone)],
        out_specs=[ospec, ospec, ospec],
        out_shape=[out, out, out],
        compiler_params=_cparams(("arbitrary",)),
        name="mla_prep",
    )(proj, proj, proj, cos_t, sin_t, qn, kvn, wq, wkv, vone)


def _attn_body(q_ref, k_ref, v_ref, o_ref, *, tq):
    i = pl.program_id(2)
    lane = lax.broadcasted_iota(jnp.int32, (1, HEAD_PAD), 1)
    qpos = lax.broadcasted_iota(jnp.int32, (tq, tq), 0)
    kpos = lax.broadcasted_iota(jnp.int32, (tq, tq), 1)
    causal = kpos <= qpos
    slices = [slice(hh * HEAD_PAD, (hh + 1) * HEAD_PAD) for hh in range(2)]

    def step(j, carry, masked):
        rows = pl.ds(pl.multiple_of(j * tq, tq), tq)
        new = []
        for sl, (m_prev, acc) in zip(slices, carry):
            s = _dot_nt(q_ref[:, sl], k_ref[rows, sl])
            if masked:
                s = jnp.where(causal, s, -jnp.inf)
            m_new = jnp.maximum(m_prev, jnp.max(s, axis=-1, keepdims=True))
            p = jnp.exp(s - m_new)
            alpha = jnp.exp(m_prev - m_new)
            new.append((m_new, alpha * acc + _dot(p.astype(BF16), v_ref[rows, sl])))
        return tuple(new)

    init = tuple((jnp.full((tq, 1), -jnp.inf, F32), jnp.zeros((tq, HEAD_PAD), F32)) for _ in range(2))
    carry = lax.fori_loop(0, i, functools.partial(step, masked=False), init)
    (_, acc0), (_, acc1) = step(i, carry, True)
    den0 = jnp.sum(jnp.where(lane == MLA_V, acc0, 0.0), axis=-1, keepdims=True)
    den1 = jnp.sum(jnp.where(lane == 0, acc1, 0.0), axis=-1, keepdims=True)
    o_ref[...] = jnp.where(lane < MLA_V, acc0 / den0, acc1 / den1).astype(o_ref.dtype)


def _attn(q, k, v, bsz, t, tq):
    m = q.shape[0]
    nq = t // tq
    pw = 2 * HEAD_PAD
    return pl.pallas_call(
        functools.partial(_attn_body, tq=tq),
        grid=(bsz, MLA_HEADS // 2, nq),
        in_specs=[pl.BlockSpec((tq, pw), lambda b, h, i: (b * nq + i, h)),
                  pl.BlockSpec((t, pw), lambda b, h, i: (b, h)),
                  pl.BlockSpec((t, pw), lambda b, h, i: (b, h))],
        out_specs=pl.BlockSpec((tq, 2 * MLA_V), lambda b, h, i: (b * nq + i, h)),
        out_shape=jax.ShapeDtypeStruct((m, MLA_WIDTH), BF16),
        compiler_params=_cparams(("arbitrary", "arbitrary", "arbitrary")),
        name="mla_attn",
    )(q, k, v)


def _softplus(x):
    return jnp.maximum(x, 0.0) + jnp.log1p(jnp.exp(-jnp.abs(x)))


def _ssd_body(z_ref, dt_ref, xbc_ref, cw_ref, cb_ref, dtb_ref, alog_ref, dsk_ref, ng_ref, tril_ref, e8_ref,
              o_ref, xp_ref, xc_ref, s_ref, *, tq):
    L = SSD_CHUNK
    gn = SSM_GROUPS * SSM_STATE
    half = SSM_WIDTH // SSM_GROUPS
    t_idx = pl.program_id(1)

    @pl.when(t_idx == 0)
    def _():
        s_ref[...] = jnp.zeros_like(s_ref)
        xp_ref[0:8, :] = jnp.zeros((8, SSM_CONV_DIM), F32)

    @pl.when(t_idx > 0)
    def _():
        xp_ref[0:8, :] = xp_ref[tq:tq + 8, :]

    xp_ref[8:8 + tq, :] = xbc_ref[...].astype(F32)
    cw = cw_ref[...]
    acc = cb_ref[...] + cw[3:4, :] * xp_ref[8:8 + tq, :]
    for w in range(SSM_CONV - 1):
        acc = acc + cw[w:w + 1, :] * xp_ref[5 + w:5 + w + tq, :]
    xc_ref[...] = acc * _sigmoid(acc)

    a_neg = -jnp.exp(alog_ref[...])
    dtb = dtb_ref[...]
    dsk = dsk_ref[...]
    ng = ng_ref[...]
    tril = tril_ref[...]
    e8 = e8_ref[...]
    ri = lax.broadcasted_iota(jnp.int32, (L, L), 0)
    ci = lax.broadcasted_iota(jnp.int32, (L, L), 1)
    tri = ci <= ri
    lane_gn = lax.broadcasted_iota(jnp.int32, (1, gn), 1)
    lane_w = lax.broadcasted_iota(jnp.int32, (1, SSM_WIDTH), 1)
    row_gn = lax.broadcasted_iota(jnp.int32, (gn, SSM_WIDTH), 0)
    col_w = lax.broadcasted_iota(jnp.int32, (gn, SSM_WIDTH), 1)
    blockdiag = (row_gn // SSM_STATE) == (col_w // half)
    head_mask = [(lane_w // SSM_HEAD_DIM) == h for h in range(SSM_HEADS)]

    def chunk(c, carry):
        rows = pl.ds(pl.multiple_of(c * L, L), L)
        xc = xc_ref[rows, :]
        xs = xc[:, :SSM_WIDTH]
        bm = xc[:, SSM_WIDTH:SSM_WIDTH + gn]
        cm = xc[:, SSM_WIDTH + gn:]
        dt = _softplus(dt_ref[rows, :].astype(F32) + dtb)
        a = dt * a_neg
        acum = _sel_rows(tril, a)
        dt_e = _sel_cols(dt, e8)
        ac_e = _sel_cols(acum, e8)
        x_dt = xs * dt_e
        last = ac_e[L - 1:L, :]
        x_dec = (x_dt * jnp.exp(last - ac_e)).astype(BF16)
        x_b = x_dt.astype(BF16)
        bm_t = bm.T.astype(BF16)
        cb0 = _dot(jnp.where(lane_gn < SSM_STATE, cm, 0.0).astype(BF16), bm_t)
        cb1 = _dot(jnp.where(lane_gn >= SSM_STATE, cm, 0.0).astype(BF16), bm_t)
        ac_t = acum.T
        ws, xblk = [], []
        for h in range(SSM_HEADS):
            seg = acum[:, h:h + 1] - ac_t[h:h + 1, :]
            lmat = jnp.exp(jnp.where(tri, seg, -jnp.inf))
            ws.append((lmat * (cb0 if h < SSM_HEADS // SSM_GROUPS else cb1)).astype(BF16))
            xblk.append(jnp.where(head_mask[h], x_b, jnp.zeros_like(x_b)))
        y = _dot(jnp.concatenate(ws, axis=1), jnp.concatenate(xblk, axis=0))
        st = s_ref[...]
        y = y + jnp.exp(ac_e) * _dot(cm.astype(BF16), st.astype(BF16))
        s_ref[...] = jnp.exp(last) * st + jnp.where(blockdiag, _dot(bm_t, x_dec), 0.0)
        y = y + xs * dsk
        z = z_ref[rows, :].astype(F32)
        y = y * (z * _sigmoid(z))
        y0 = _rms(y[:, :half], ng[:, :half])
        y1 = _rms(y[:, half:], ng[:, half:])
        o_ref[rows, :half] = y0.astype(o_ref.dtype)
        o_ref[rows, half:] = y1.astype(o_ref.dtype)
        return carry

    lax.fori_loop(0, tq // L, chunk, 0)


def _ssd(proj, cw, cb, dtb, alog, dsk, ng, bsz, t, tq):
    m = proj.shape[0]
    nq = t // tq
    L = SSD_CHUNK
    tril = jnp.asarray(np.tile(np.tril(np.ones((L, L), np.float32)), (1, N_SPLIT)), BF16)
    e8 = np.zeros((LANES, SSM_WIDTH), np.float32)
    for h in range(SSM_HEADS):
        e8[h, h * SSM_HEAD_DIM:(h + 1) * SSM_HEAD_DIM] = 1.0
    e8 = jnp.asarray(np.tile(e8, (N_SPLIT, 1)), BF16)
    full = lambda a: pl.BlockSpec(a.shape, lambda b, i: (0, 0))
    return pl.pallas_call(
        functools.partial(_ssd_body, tq=tq),
        grid=(bsz, nq),
        in_specs=[pl.BlockSpec((tq, SSM_WIDTH), lambda b, i: (b * nq + i, OFF_SZ // SSM_WIDTH)),
                  pl.BlockSpec((tq, LANES), lambda b, i: (b * nq + i, OFF_SDT // LANES)),
                  pl.BlockSpec((tq, SSM_CONV_DIM), lambda b, i: (b * nq + i, OFF_SXBC // SSM_CONV_DIM)),
                  full(cw), full(cb), full(dtb), full(alog), full(dsk), full(ng), full(tril), full(e8)],
        out_specs=pl.BlockSpec((tq, SSM_WIDTH), lambda b, i: (b * nq + i, 0)),
        out_shape=jax.ShapeDtypeStruct((m, SSM_WIDTH), BF16),
        scratch_shapes=[pltpu.VMEM((tq + 8, SSM_CONV_DIM), F32),
                        pltpu.VMEM((tq, SSM_CONV_DIM), F32),
                        pltpu.VMEM((SSM_GROUPS * SSM_STATE, SSM_WIDTH), F32)],
        compiler_params=_cparams(("arbitrary", "arbitrary")),
        name="ssd",
    )(proj, proj, proj, cw, cb, dtb, alog, dsk, ng, tril, e8)


def _merge_body(ya_ref, yb_ref, yc_ref, gate_ref, x_ref, wa_ref, wb_ref, wc_ref, wo_ref, o_ref):
    d = D_MODEL
    gate = lambda b: _sigmoid(gate_ref[:, b * d:(b + 1) * d].astype(F32))
    merged = (gate(0) * _dot(ya_ref[...], wa_ref[...])
              + gate(1) * _dot(yb_ref[...], wb_ref[...])
              + gate(2) * _dot(yc_ref[...], wc_ref[...]))
    o_ref[...] = x_ref[...] + _dot(merged.astype(BF16), wo_ref[...])


def _merge(ya, yb, yc, proj, x2, wa, wb, wc, wo, tm):
    m, d = x2.shape
    full = lambda a: pl.BlockSpec(a.shape, lambda i: (0, 0))
    row = lambda w: pl.BlockSpec((tm, w), lambda i: (i, 0))
    return pl.pallas_call(
        _merge_body,
        grid=(m // tm,),
        in_specs=[row(HG_WIDTH), row(MLA_WIDTH), row(SSM_WIDTH), row(N_BRANCH * d), row(d),
                  full(wa), full(wb), full(wc), full(wo)],
        out_specs=row(d),
        out_shape=jax.ShapeDtypeStruct((m, d), F32),
        compiler_params=_cparams(("arbitrary",)),
        name="merge_out",
    )(ya, yb, yc, proj, x2, wa, wb, wc, wo)


def _mlp_body(x_ref, g_ref, wu_ref, wd_ref, fg_ref, o_ref, *, last_layer):
    x = x_ref[...]
    h = _rms(x, g_ref[...]).astype(BF16)
    u = jnp.maximum(_dot(h, wu_ref[...]), 0.0)
    y = x + _dot((u * u).astype(BF16), wd_ref[...])
    o_ref[...] = _rms(y, fg_ref[...]) if last_layer else y


def _mlp(x2, g, wu, wd, final_g, tm, last_layer):
    m, d = x2.shape
    const = lambda a: pl.BlockSpec(a.shape, lambda i: (0, 0), pipeline_mode=pl.Buffered(1))
    return pl.pallas_call(
        functools.partial(_mlp_body, last_layer=last_layer),
        grid=(m // tm,),
        in_specs=[pl.BlockSpec((tm, d), lambda i: (i, 0)), const(g), const(wu), const(wd), const(final_g)],
        out_specs=pl.BlockSpec((tm, d), lambda i: (i, 0)),
        out_shape=jax.ShapeDtypeStruct((m, d), F32),
        compiler_params=_cparams(("arbitrary",)),
        name="mlp",
    )(x2, g, wu, wd, final_g)


def _rot_cols(w):
    hr = MLA_ROPE // 2
    return jnp.concatenate([-w[..., hr:], w[..., :hr]], axis=-1)


def _pack_w_in(w_in):
    dep, d, _ = w_in.shape
    sizes = (512, 512, 512, 512, MLA_Q_LORA, MLA_KV_LORA, MLA_ROPE, SSM_WIDTH, SSM_CONV_DIM, SSM_HEADS,
             N_BRANCH * D_MODEL)
    pts = np.cumsum(sizes)[:-1].tolist()
    hq, hf, hi, hg, cq, ckv, kr, sz, sxbc, sdt, gate = jnp.split(w_in, pts, axis=-1)
    z = lambda n: jnp.zeros((dep, d, n), w_in.dtype)
    kr_blk = jnp.concatenate([_rot_cols(kr), z(32), kr, z(32)], axis=-1)
    packed = jnp.concatenate([gate, hq, hf, hi, hg, sz, sdt, z(LANES - SSM_HEADS), cq, sxbc, ckv, kr_blk],
                             axis=-1)
    assert packed.shape[-1] == PROJ_COLS
    return packed.astype(BF16)


def _pack_w_uq(w):
    dep, r, _ = w.shape
    wh = w.reshape(dep, r, MLA_HEADS, MLA_NOPE + MLA_ROPE)
    nope, rope = wh[..., :MLA_NOPE], wh[..., MLA_NOPE:]
    z = lambda n: jnp.zeros((dep, r, MLA_HEADS, n), w.dtype)
    a = jnp.concatenate([nope, rope, z(32)], axis=-1).reshape(dep, r, MLA_HEADS * HEAD_PAD)
    b = jnp.concatenate([z(64), _rot_cols(rope), z(32)], axis=-1).reshape(dep, r, MLA_HEADS * HEAD_PAD)
    return jnp.concatenate([a, b], axis=-1).astype(BF16)


def _pack_w_ukv(w):
    dep, r, _ = w.shape
    wh = w.reshape(dep, r, MLA_HEADS // 2, 2, MLA_NOPE + MLA_V)
    kn, v = wh[..., :MLA_NOPE], wh[..., MLA_NOPE:]
    z = jnp.zeros_like(kn)
    ka = jnp.concatenate([kn, z], axis=-1).reshape(dep, r, MLA_HEADS * HEAD_PAD)
    va = jnp.stack([jnp.concatenate([v[..., 0, :], z[..., 0, :]], axis=-1),
                    jnp.concatenate([z[..., 1, :], v[..., 1, :]], axis=-1)], axis=-2)
    va = va.reshape(dep, r, MLA_HEADS * HEAD_PAD)
    return jnp.concatenate([ka, va], axis=-1).astype(BF16)


def _v_ones():
    v = np.zeros((1, MLA_HEADS * HEAD_PAD), np.float32)
    for h in range(MLA_HEADS):
        v[0, h * HEAD_PAD + (MLA_V if h % 2 == 0 else 0)] = 1.0
    return jnp.asarray(v)


def _rope_tables(positions):
    inv = ROPE_THETA ** (-jnp.arange(0, MLA_ROPE, 2, dtype=F32) / MLA_ROPE)
    ang = positions.astype(F32).reshape(-1, 1) * inv
    z = lambda n: jnp.zeros((ang.shape[0], n), F32)
    cos, sin = jnp.cos(ang), jnp.sin(ang)
    return (jnp.concatenate([z(64), cos, cos, z(32)], axis=-1),
            jnp.concatenate([z(64), sin, sin, z(32)], axis=-1))


def _row_tile(n, want):
    t = min(n, want)
    assert n % t == 0, (n, t)
    return t


def kernel(x, positions, mix_norm, w_in, hg_lb_logits, hg_norm, mla_q_norm, mla_kv_norm, mla_w_uq, mla_w_ukv,
           ssm_conv_w, ssm_conv_b, ssm_dt_bias, ssm_a_log, ssm_d, ssm_norm, w_br_hg, w_br_mla, w_br_ssm,
           w_out, mlp_norm, w_up, w_down, final_norm):
    bsz, t, d = x.shape
    depth = w_in.shape[0]
    assert d == D_MODEL and t % SSD_CHUNK == 0
    m = bsz * t
    tm = _row_tile(m, 512)
    tq = _row_tile(t, 512)

    lbs = jnp.cumsum(jax.nn.softmax(hg_lb_logits.astype(F32), axis=0), axis=0)
    lbs = lbs - lbs[0:1]
    w_in_p = _pack_w_in(w_in)
    wq_p = _pack_w_uq(mla_w_uq)
    wkv_p = _pack_w_ukv(mla_w_ukv)
    vone = _v_ones()
    cos_t, sin_t = _rope_tables(positions)
    pad_h = lambda a: jnp.pad(a.astype(F32), ((0, 0), (0, LANES - SSM_HEADS)))
    dtb_p, alog_p = pad_h(ssm_dt_bias), pad_h(ssm_a_log)
    dsk_e = jnp.repeat(ssm_d.astype(F32), SSM_HEAD_DIM, axis=-1)
    bf = lambda a: a.astype(BF16)
    wa, wb, wc, wo, wu, wd = bf(w_br_hg), bf(w_br_mla), bf(w_br_ssm), bf(w_out), bf(w_up), bf(w_down)
    r1 = lambda a: a.reshape(1, -1).astype(F32)

    x2 = x.reshape(m, d)
    for l in range(depth):
        proj = _inproj(x2, r1(mix_norm[l]), w_in_p[l], tm, PROJ_COLS // 3)
        ya = _hgrn(proj, r1(lbs[l]), r1(hg_norm[l]), bsz, t, tq)
        q, k, v = _mla_prep(proj, cos_t, sin_t, r1(mla_q_norm[l]), r1(mla_kv_norm[l]), wq_p[l], wkv_p[l],
                            vone, tm)
        yb = _attn(q, k, v, bsz, t, tq)
        yc = _ssd(proj, ssm_conv_w[l].astype(F32), r1(ssm_conv_b[l]), r1(dtb_p[l]), r1(alog_p[l]),
                  r1(dsk_e[l]), r1(ssm_norm[l]), bsz, t, tq)
        x2 = _merge(ya, yb, yc, proj, x2, wa[l], wb[l], wc[l], wo[l], tm)
        x2 = _mlp(x2, r1(mlp_norm[l]), wu[l], wd[l], r1(final_norm), tm, last_layer=(l == depth - 1))
    return x2.reshape(bsz, t, d)
```

```python
import functools
import math

import numpy as np
import jax
import jax.numpy as jnp
from jax import lax
from jax.experimental import pallas as pl
from jax.experimental.pallas import tpu as pltpu

F32 = jnp.float32
BF16 = jnp.bfloat16

D_MODEL = 1024
EPS = 1e-6
HG_HEADS = 4
HG_DK = 128
HG_DV = 128
HG_WIDTH = HG_HEADS * HG_DV
MLA_HEADS = 8
MLA_NOPE = 64
MLA_ROPE = 32
MLA_V = 64
MLA_Q_LORA = 384
MLA_KV_LORA = 256
MLA_WIDTH = MLA_HEADS * MLA_V
ROPE_THETA = 10000.0
SSM_HEADS = 8
SSM_HEAD_DIM = 64
SSM_WIDTH = SSM_HEADS * SSM_HEAD_DIM
SSM_GROUPS = 2
SSM_STATE = 64
SSM_CONV = 4
SSM_CONV_DIM = SSM_WIDTH + 2 * SSM_GROUPS * SSM_STATE
D_FF = 4 * D_MODEL
N_BRANCH = 3

LANES = 128
HEAD_PAD = 128

OFF_GATE = 0
OFF_HQ, OFF_HF, OFF_HI, OFF_HG = 3072, 3584, 4096, 4608
OFF_SZ = 5120
OFF_MISC = 5632
DT_LANE = 32
OFF_CQ = 5760
OFF_SXBC = 6144
OFF_CKV = 6912
PROJ_COLS = 7168

HG_CHUNK = 64
HG_LEVELS = 6
SSD_CHUNK = 128
ATTN_HEADS = 4

VMEM_LIMIT = 48 * 1024 * 1024


def _cparams(sem, flags=None):
    return pltpu.CompilerParams(dimension_semantics=sem, vmem_limit_bytes=VMEM_LIMIT, flags=flags)


def _sigmoid(x):
    return jax.nn.sigmoid(x)


def _split3(x):
    hi = x.astype(BF16)
    r1 = x - hi.astype(F32)
    mid = r1.astype(BF16)
    r2 = r1 - mid.astype(F32)
    return hi, mid, r2.astype(BF16)


def _dot(a, b):
    return jnp.dot(a, b, preferred_element_type=F32)


def _dot_nt(a, b):
    return lax.dot_general(a, b, (((1,), (1,)), ((), ())), preferred_element_type=F32)


N_SPLIT = 3


def _sel_rows(sel3, x):
    return _dot(sel3, jnp.concatenate(_split3(x), axis=0))


def _sel_cols(x, sel3):
    return _dot(jnp.concatenate(_split3(x), axis=1), sel3)


def _rms(x, g):
    ms = jnp.mean(x * x, axis=-1, keepdims=True)
    return x * lax.rsqrt(ms + EPS) * g


def _inproj_body(x_ref, g_ref, w_ref, o_ref):
    h = _rms(x_ref[...], g_ref[...]).astype(BF16)
    o_ref[...] = _dot(h, w_ref[...]).astype(o_ref.dtype)


def _inproj(x2, g, w, tm, tn):
    m, d = x2.shape
    n = w.shape[1]
    return pl.pallas_call(
        _inproj_body,
        grid=(n // tn, m // tm),
        in_specs=[pl.BlockSpec((tm, d), lambda j, i: (i, 0)),
                  pl.BlockSpec((1, d), lambda j, i: (0, 0)),
                  pl.BlockSpec((d, tn), lambda j, i: (0, j))],
        out_specs=pl.BlockSpec((tm, tn), lambda j, i: (i, j)),
        out_shape=jax.ShapeDtypeStruct((m, n), BF16),
        compiler_params=_cparams(("arbitrary", "arbitrary")),
        name="inproj",
    )(x2, g, w)


def _hgrn_level_matrix():
    c = HG_CHUNK
    mat = np.zeros((8, c, c), np.float32)
    for i in range(c):
        mat[0, i, : i + 1] = 1.0
        mat[7, i, i + 1:] = 1.0
    for lvl in range(HG_LEVELS):
        m = 1 << lvl
        for r in range(c):
            start = (r // (2 * m)) * 2 * m
            mid = start + m
            if r >= mid:
                mat[lvl + 1, r, mid: r + 1] = 1.0
            else:
                mat[lvl + 1, r, r + 1: mid] = 1.0
    return mat.reshape(8 * c, c)


def _hgrn_body(q_ref, f_ref, i_ref, g_ref, lb_ref, ng_ref, m_ref, o_ref, s_ref, *, n_chunks):
    c = HG_CHUNK

    @pl.when(pl.program_id(1) == 0)
    def _():
        s_ref[...] = jnp.zeros_like(s_ref)

    lb = lb_ref[...]
    log_lb = jnp.log(lb)
    log1m_lb = jnp.log1p(-lb)
    one_m_lb = 1.0 - lb
    ng = ng_ref[...]
    sel = m_ref[...]

    row = lax.broadcasted_iota(jnp.int32, (c, c), 0)
    col = lax.broadcasted_iota(jnp.int32, (c, c), 1)
    diag = row == col
    masks = []
    for lvl in range(HG_LEVELS):
        same_block = (row >> (lvl + 1)) == (col >> (lvl + 1))
        masks.append(same_block & (((row >> lvl) & 1) == 1) & (((col >> lvl) & 1) == 0))
    rowv = lax.broadcasted_iota(jnp.int32, (c, HG_DK), 0)
    second_half = [((rowv >> lvl) & 1) == 1 for lvl in range(HG_LEVELS)]

    def chunk(ci, carry):
        r0 = pl.multiple_of(ci * c, c)
        rows = pl.ds(r0, c)
        hq = q_ref[rows, :].astype(F32)
        ff = f_ref[rows, :].astype(F32)
        hv = i_ref[rows, :].astype(F32)
        hg = g_ref[rows, :].astype(F32)
        q = hq * _sigmoid(hq)
        log_sig = jnp.minimum(ff, 0.0) - jnp.log1p(jnp.exp(-jnp.abs(ff)))
        b = log1m_lb + log_sig
        lf = jnp.maximum(log_lb, b) + jnp.log1p(jnp.exp(-jnp.abs(log_lb - b)))
        k = one_m_lb * _sigmoid(-ff)
        gate = hg * _sigmoid(hg)
        ex = _sel_rows(sel, lf)

        for h in range(HG_HEADS):
            sl = slice(h * HG_DK, (h + 1) * HG_DK)
            qh, kh = q[:, sl], k[:, sl]
            eh = ex[:, sl]
            vh = hv[:, sl]
            vb = vh.astype(BF16)
            qb = qh.astype(BF16)
            kb = kh.astype(BF16)
            att = jnp.where(diag, _dot_nt(qb, kb), 0.0)
            for lvl in range(HG_LEVELS):
                e_l = eh[(lvl + 1) * c:(lvl + 2) * c]
                xl = (jnp.where(second_half[lvl], qh, kh) * jnp.exp(e_l)).astype(BF16)
                att = att + jnp.where(masks[lvl], _dot_nt(xl, xl), 0.0)
            e0 = eh[0:c]
            q_in = (qh * jnp.exp(e0)).astype(BF16)
            k_out = (kh * jnp.exp(eh[7 * c:8 * c])).astype(BF16)
            st = s_ref[h]
            o = _dot(att.astype(BF16), vb) + _dot_nt(q_in, st.astype(BF16))
            decay = jnp.exp(e0[c - 1:c, :])
            s_ref[h] = st * decay + _dot(vh.T.astype(BF16), k_out)
            y = _rms(o, ng[:, sl]) * gate[:, sl]
            o_ref[rows, sl] = y.astype(o_ref.dtype)
        return carry

    lax.fori_loop(0, n_chunks, chunk, 0)


def _hgrn(proj, lb, ng, bsz, t, tq):
    m = proj.shape[0]
    nq = t // tq
    sel = jnp.asarray(np.tile(_hgrn_level_matrix(), (1, N_SPLIT)), BF16)
    w = HG_WIDTH

    def col_spec(off):
        blk = off // w
        return pl.BlockSpec((tq, w), lambda b, i: (b * nq + i, blk))

    vec = pl.BlockSpec((1, w), lambda b, i: (0, 0))
    return pl.pallas_call(
        functools.partial(_hgrn_body, n_chunks=tq // HG_CHUNK),
        grid=(bsz, nq),
        in_specs=[col_spec(OFF_HQ), col_spec(OFF_HF), col_spec(OFF_HI), col_spec(OFF_HG), vec, vec,
                  pl.BlockSpec(sel.shape, lambda b, i: (0, 0))],
        out_specs=pl.BlockSpec((tq, w), lambda b, i: (b * nq + i, 0)),
        out_shape=jax.ShapeDtypeStruct((m, w), BF16),
        scratch_shapes=[pltpu.VMEM((HG_HEADS, HG_DV, HG_DK), F32)],
        compiler_params=_cparams(("arbitrary", "arbitrary")),
        name="hgrn2",
    )(proj, proj, proj, proj, lb, ng, sel)


def _mla_prep_body(cq_ref, ckv_ref, kr_ref, cos_ref, sin_ref, qn_ref, kvn_ref, wq_ref, wkv_ref, vone_ref,
                   q_out, k_out, v_out):
    scale = (MLA_NOPE + MLA_ROPE) ** -0.5 * math.log2(math.e)
    cqn =_rms(cq_ref[...].astype(F32), qn_ref[...]).astype(BF16)
    ckvn = _rms(ckv_ref[...].astype(F32), kvn_ref[...]).astype(BF16)
    qq = _dot(cqn, wq_ref[...])
    kv = _dot(ckvn, wkv_ref[...])
    cos = cos_ref[...]
    sin = sin_ref[...]
    lane = lax.broadcasted_iota(jnp.int32, (1, HEAD_PAD), 1)
    cq_t = scale * (cos + (lane < MLA_NOPE).astype(F32))
    sq_t = scale * sin
    krb = kr_ref[...].astype(F32)
    k_rope = krb * cos + pltpu.roll(krb, 64, 1) * sin
    hw = MLA_HEADS * HEAD_PAD
    vone = vone_ref[...]
    for h in range(MLA_HEADS):
        sl = slice(h * HEAD_PAD, (h + 1) * HEAD_PAD)
        sl2 = slice(hw + h * HEAD_PAD, hw + (h + 1) * HEAD_PAD)
        q_out[:, sl] = (qq[:, sl] * cq_t + qq[:, sl2] * sq_t).astype(BF16)
        k_out[:, sl] = (kv[:, sl] + k_rope).astype(BF16)
        v_out[:, sl] = (kv[:, sl2] + vone[:, sl]).astype(BF16)


def _mla_prep(proj, cos_t, sin_t, qn, kvn, wq, wkv, vone, tm):
    m = proj.shape[0]
    hw = MLA_HEADS * HEAD_PAD
    full = lambda a: pl.BlockSpec(a.shape, lambda i: (0, 0))
    out = jax.ShapeDtypeStruct((m, hw), BF16)
    ospec = pl.BlockSpec((tm, hw), lambda i: (i, 0))
    return pl.pallas_call(
        _mla_prep_body,
        grid=(m // tm,),
        in_specs=[pl.BlockSpec((tm, MLA_Q_LORA), lambda i: (i, OFF_CQ // MLA_Q_LORA)),
                  pl.BlockSpec((tm, MLA_KV_LORA), lambda i: (i, OFF_CKV // MLA_KV_LORA)),
                  pl.BlockSpec((tm, LANES), lambda i: (i, OFF_MISC // LANES)),
                  pl.BlockSpec((tm, LANES), lambda i: (i, 0)),
                  pl.BlockSpec((tm, LANES), lambda i: (i, 0)),
                  full(qn), full(kvn), full(wq), full(wkv), full(vone)],
        out_specs=[ospec, ospec, ospec],
        out_shape=[out, out, out],
        compiler_params=_cparams(("arbitrary",)),
        name="mla_prep",
    )(proj, proj, proj, cos_t, sin_t, qn, kvn, wq, wkv, vone)


def _attn_body(q_ref, k_ref, v_ref, o_ref, *, tq):
    i = pl.program_id(2)
    lane = lax.broadcasted_iota(jnp.int32, (1, HEAD_PAD), 1)
    qpos = lax.broadcasted_iota(jnp.int32, (tq, tq), 0)
    kpos = lax.broadcasted_iota(jnp.int32, (tq, tq), 1)
    causal = kpos <= qpos
    heads = range(ATTN_HEADS)
    slices = [slice(h * HEAD_PAD, (h + 1) * HEAD_PAD) for h in heads]
    block = lambda j: pl.ds(pl.multiple_of(j * tq, tq), tq)

    def scores(h, j, masked, m_prev):
        s = _dot_nt(q_ref[:, slices[h]], k_ref[block(j), slices[h]])
        if masked:
            s = jnp.where(causal, s, -jnp.inf)
        m_new = jnp.maximum(m_prev, jnp.max(s, axis=-1, keepdims=True))
        return m_new, jnp.exp2(m_prev - m_new), jnp.exp2((s - m_new).astype(BF16))

    def step(j, state, masked):
        new = []
        for h in heads:
            m, acc = state[h]
            m, alpha, p = scores(h, j, masked, m)
            new.append((m, alpha * acc + _dot(p, v_ref[block(j), slices[h]])))
        return tuple(new)

    init = (jnp.full((tq, 1), -jnp.inf, F32), jnp.zeros((tq, HEAD_PAD), F32))
    state = lax.fori_loop(0, i, functools.partial(step, masked=False), (init,) * ATTN_HEADS)
    state = step(i, state, True)
    for pair in range(ATTN_HEADS // 2):
        acc0, acc1 = state[2 * pair][1], state[2 * pair + 1][1]
        den0 = jnp.sum(jnp.where(lane == MLA_V, acc0, 0.0), axis=-1, keepdims=True)
        den1 = jnp.sum(jnp.where(lane == 0, acc1, 0.0), axis=-1, keepdims=True)
        o_ref[:, pair * HEAD_PAD:(pair + 1) * HEAD_PAD] = jnp.where(
            lane < MLA_V, acc0 / den0, acc1 / den1).astype(o_ref.dtype)


def _attn(q, k, v, bsz, t, tq):
    m = q.shape[0]
    nq = t // tq
    pw = ATTN_HEADS * HEAD_PAD
    return pl.pallas_call(
        functools.partial(_attn_body, tq=tq),
        grid=(bsz, MLA_HEADS // ATTN_HEADS, nq),
        in_specs=[pl.BlockSpec((tq, pw), lambda b, h, i: (b * nq + i, h)),
                  pl.BlockSpec((t, pw), lambda b, h, i: (b, h)),
                  pl.BlockSpec((t, pw), lambda b, h, i: (b, h))],
        out_specs=pl.BlockSpec((tq, ATTN_HEADS * MLA_V), lambda b, h, i: (b * nq + i, h)),
        out_shape=jax.ShapeDtypeStruct((m, MLA_WIDTH), BF16),
        compiler_params=_cparams(("arbitrary", "arbitrary", "arbitrary")),
        name="mla_attn",
    )(q, k, v)


def _softplus(x):
    return jnp.maximum(x, 0.0) + jnp.log1p(jnp.exp(-jnp.abs(x)))


def _ssd_body(z_ref, dt_ref, xbc_ref, cw_ref, cb_ref, dtb_ref, alog_ref, dsk_ref, ng_ref, tril_ref, e8_ref,
              o_ref, xp_ref, xc_ref, s_ref, *, tq):
    L = SSD_CHUNK
    gn = SSM_GROUPS * SSM_STATE
    half = SSM_WIDTH // SSM_GROUPS
    t_idx = pl.program_id(1)

    @pl.when(t_idx == 0)
    def _():
        s_ref[...] = jnp.zeros_like(s_ref)
        xp_ref[0:8, :] = jnp.zeros((8, SSM_CONV_DIM), F32)

    @pl.when(t_idx > 0)
    def _():
        xp_ref[0:8, :] = xp_ref[tq:tq + 8, :]

    xp_ref[8:8 + tq, :] = xbc_ref[...].astype(F32)
    cw = cw_ref[...]
    acc = cb_ref[...] + cw[3:4, :] * xp_ref[8:8 + tq, :]
    for w in range(SSM_CONV - 1):
        acc = acc + cw[w:w + 1, :] * xp_ref[5 + w:5 + w + tq, :]
    xc_ref[...] = acc * _sigmoid(acc)

    a_neg = -jnp.exp(alog_ref[...])
    dtb = dtb_ref[...]
    dsk = dsk_ref[...]
    ng = ng_ref[...]
    tril = tril_ref[...]
    e8 = e8_ref[...]
    ri = lax.broadcasted_iota(jnp.int32, (L, L), 0)
    ci = lax.broadcasted_iota(jnp.int32, (L, L), 1)
    tri = ci <= ri
    lane_gn = lax.broadcasted_iota(jnp.int32, (1, gn), 1)
    lane_w = lax.broadcasted_iota(jnp.int32, (1, SSM_WIDTH), 1)
    row_gn = lax.broadcasted_iota(jnp.int32, (gn, SSM_WIDTH), 0)
    col_w = lax.broadcasted_iota(jnp.int32, (gn, SSM_WIDTH), 1)
    blockdiag = (row_gn // SSM_STATE) == (col_w // half)
    head_mask = [(lane_w // SSM_HEAD_DIM) == h for h in range(SSM_HEADS)]

    def chunk(c, carry):
        rows = pl.ds(pl.multiple_of(c * L, L), L)
        xc = xc_ref[rows, :]
        xs = xc[:, :SSM_WIDTH]
        bm = xc[:, SSM_WIDTH:SSM_WIDTH + gn]
        cm = xc[:, SSM_WIDTH + gn:]
        dt = _softplus(dt_ref[rows, :].astype(F32) + dtb)
        a = dt * a_neg
        acum = _sel_rows(tril, a)
        dt_e = _sel_cols(dt, e8)
        ac_e = _sel_cols(acum, e8)
        x_dt = xs * dt_e
        last = ac_e[L - 1:L, :]
        x_dec = (x_dt * jnp.exp(last - ac_e)).astype(BF16)
        x_b = x_dt.astype(BF16)
        bm_t = bm.T.astype(BF16)
        cb0 = _dot(jnp.where(lane_gn < SSM_STATE, cm, 0.0).astype(BF16), bm_t)
        cb1 = _dot(jnp.where(lane_gn >= SSM_STATE, cm, 0.0).astype(BF16), bm_t)
        ac_t = acum.T
        ws, xblk = [], []
        for h in range(SSM_HEADS):
            hl = DT_LANE + h
            seg = acum[:, hl:hl + 1] - ac_t[hl:hl + 1, :]
            lmat = jnp.exp(jnp.where(tri, seg, -jnp.inf))
            ws.append((lmat * (cb0 if h < SSM_HEADS // SSM_GROUPS else cb1)).astype(BF16))
            xblk.append(jnp.where(head_mask[h], x_b, jnp.zeros_like(x_b)))
        y = _dot(jnp.concatenate(ws, axis=1), jnp.concatenate(xblk, axis=0))
        st = s_ref[...]
        y = y + jnp.exp(ac_e) * _dot(cm.astype(BF16), st.astype(BF16))
        s_ref[...] = jnp.exp(last) * st + jnp.where(blockdiag, _dot(bm_t, x_dec), 0.0)
        y = y + xs * dsk
        z = z_ref[rows, :].astype(F32)
        y = y * (z * _sigmoid(z))
        y0 = _rms(y[:, :half], ng[:, :half])
        y1 = _rms(y[:, half:], ng[:, half:])
        o_ref[rows, :half] = y0.astype(o_ref.dtype)
        o_ref[rows, half:] = y1.astype(o_ref.dtype)
        return carry

    lax.fori_loop(0, tq // L, chunk, 0)


def _ssd(proj, cw, cb, dtb, alog, dsk, ng, bsz, t, tq):
    m = proj.shape[0]
    nq = t // tq
    L = SSD_CHUNK
    tril = jnp.asarray(np.tile(np.tril(np.ones((L, L), np.float32)), (1, N_SPLIT)), BF16)
    e8 = np.zeros((LANES, SSM_WIDTH), np.float32)
    for h in range(SSM_HEADS):
        e8[DT_LANE + h, h * SSM_HEAD_DIM:(h + 1) * SSM_HEAD_DIM] = 1.0
    e8 = jnp.asarray(np.tile(e8, (N_SPLIT, 1)), BF16)
    full = lambda a: pl.BlockSpec(a.shape, lambda b, i: (0, 0))
    return pl.pallas_call(
        functools.partial(_ssd_body, tq=tq),
        grid=(bsz, nq),
        in_specs=[pl.BlockSpec((tq, SSM_WIDTH), lambda b, i: (b * nq + i, OFF_SZ // SSM_WIDTH)),
                  pl.BlockSpec((tq, LANES), lambda b, i: (b * nq + i, OFF_MISC // LANES)),
                  pl.BlockSpec((tq, SSM_CONV_DIM), lambda b, i: (b * nq + i, OFF_SXBC // SSM_CONV_DIM)),
                  full(cw), full(cb), full(dtb), full(alog), full(dsk), full(ng), full(tril), full(e8)],
        out_specs=pl.BlockSpec((tq, SSM_WIDTH), lambda b, i: (b * nq + i, 0)),
        out_shape=jax.ShapeDtypeStruct((m, SSM_WIDTH), BF16),
        scratch_shapes=[pltpu.VMEM((tq + 8, SSM_CONV_DIM), F32),
                        pltpu.VMEM((tq, SSM_CONV_DIM), F32),
                        pltpu.VMEM((SSM_GROUPS * SSM_STATE, SSM_WIDTH), F32)],
        compiler_params=_cparams(("arbitrary", "arbitrary")),
        name="ssd",
    )(proj, proj, proj, cw, cb, dtb, alog, dsk, ng, tril, e8)


def _merge_body(ya_ref, yb_ref, yc_ref, gate_ref, x_ref, wa_ref, wb_ref, wc_ref, wo_ref, o_ref):
    d = D_MODEL
    gate = lambda b: _sigmoid(gate_ref[:, b * d:(b + 1) * d].astype(F32))
    merged = (gate(0) * _dot(ya_ref[...], wa_ref[...])
              + gate(1) * _dot(yb_ref[...], wb_ref[...])
              + gate(2) * _dot(yc_ref[...], wc_ref[...]))
    o_ref[...] = x_ref[...] + _dot(merged.astype(BF16), wo_ref[...])


def _merge(ya, yb, yc, proj, x2, wa, wb, wc, wo, tm):
    m, d = x2.shape
    full = lambda a: pl.BlockSpec(a.shape, lambda i: (0, 0))
    row = lambda w: pl.BlockSpec((tm, w), lambda i: (i, 0))
    return pl.pallas_call(
        _merge_body,
        grid=(m // tm,),
        in_specs=[row(HG_WIDTH), row(MLA_WIDTH), row(SSM_WIDTH), row(N_BRANCH * d), row(d),
                  full(wa), full(wb), full(wc), full(wo)],
        out_specs=row(d),
        out_shape=jax.ShapeDtypeStruct((m, d), F32),
        compiler_params=_cparams(("arbitrary",)),
        name="merge_out",
    )(ya, yb, yc, proj, x2, wa, wb, wc, wo)


def _mlp_body(x_ref, g_ref, wu_ref, wd_ref, fg_ref, o_ref, *, last_layer):
    x = x_ref[...]
    h = _rms(x, g_ref[...]).astype(BF16)
    u = jnp.maximum(_dot(h, wu_ref[...]), 0.0)
    y = x + _dot((u * u).astype(BF16), wd_ref[...])
    o_ref[...] = _rms(y, fg_ref[...]) if last_layer else y


def _mlp(x2, g, wu, wd, final_g, tm, last_layer):
    m, d = x2.shape
    const = lambda a: pl.BlockSpec(a.shape, lambda i: (0, 0), pipeline_mode=pl.Buffered(1))
    return pl.pallas_call(
        functools.partial(_mlp_body, last_layer=last_layer),
        grid=(m // tm,),
        in_specs=[pl.BlockSpec((tm, d), lambda i: (i, 0)), const(g), const(wu), const(wd), const(final_g)],
        out_specs=pl.BlockSpec((tm, d), lambda i: (i, 0)),
        out_shape=jax.ShapeDtypeStruct((m, d), F32),
        compiler_params=_cparams(("arbitrary",)),
        name="mlp",
    )(x2, g, wu, wd, final_g)


def _rot_cols(w):
    hr = MLA_ROPE // 2
    return jnp.concatenate([-w[..., hr:], w[..., :hr]], axis=-1)


def _pack_w_in(w_in):
    dep, d, _ = w_in.shape
    sizes = (512, 512, 512, 512, MLA_Q_LORA, MLA_KV_LORA, MLA_ROPE, SSM_WIDTH, SSM_CONV_DIM, SSM_HEADS,
             N_BRANCH * D_MODEL)
    pts = np.cumsum(sizes)[:-1].tolist()
    hq, hf, hi, hg, cq, ckv, kr, sz, sxbc, sdt, gate = jnp.split(w_in, pts, axis=-1)
    z = lambda n: jnp.zeros((dep, d, n), w_in.dtype)
    misc = jnp.concatenate([_rot_cols(kr), sdt, z(32 - SSM_HEADS), kr, z(32)], axis=-1)
    packed = jnp.concatenate([gate, hq, hf, hi, hg, sz, misc, cq, sxbc, ckv], axis=-1)
    assert packed.shape[-1] == PROJ_COLS
    return packed.astype(BF16)


def _pack_w_uq(w):
    dep, r, _ = w.shape
    wh = w.reshape(dep, r, MLA_HEADS, MLA_NOPE + MLA_ROPE)
    nope, rope = wh[..., :MLA_NOPE], wh[..., MLA_NOPE:]
    z = lambda n: jnp.zeros((dep, r, MLA_HEADS, n), w.dtype)
    a = jnp.concatenate([nope, rope, z(32)], axis=-1).reshape(dep, r, MLA_HEADS * HEAD_PAD)
    b = jnp.concatenate([z(64), _rot_cols(rope), z(32)], axis=-1).reshape(dep, r, MLA_HEADS * HEAD_PAD)
    return jnp.concatenate([a, b], axis=-1).astype(BF16)


def _pack_w_ukv(w):
    dep, r, _ = w.shape
    wh = w.reshape(dep, r, MLA_HEADS // 2, 2, MLA_NOPE + MLA_V)
    kn, v = wh[..., :MLA_NOPE], wh[..., MLA_NOPE:]
    z = jnp.zeros_like(kn)
    ka = jnp.concatenate([kn, z], axis=-1).reshape(dep, r, MLA_HEADS * HEAD_PAD)
    va = jnp.stack([jnp.concatenate([v[..., 0, :], z[..., 0, :]], axis=-1),
                    jnp.concatenate([z[..., 1, :], v[..., 1, :]], axis=-1)], axis=-2)
    va = va.reshape(dep, r, MLA_HEADS * HEAD_PAD)
    return jnp.concatenate([ka, va], axis=-1).astype(BF16)


def _v_ones():
    v = np.zeros((1, MLA_HEADS * HEAD_PAD), np.float32)
    for h in range(MLA_HEADS):
        v[0, h * HEAD_PAD + (MLA_V if h % 2 == 0 else 0)] = 1.0
    return jnp.asarray(v)


def _rope_tables(positions):
    inv = ROPE_THETA ** (-jnp.arange(0, MLA_ROPE, 2, dtype=F32) / MLA_ROPE)
    ang = positions.astype(F32).reshape(-1, 1) * inv
    z = lambda n: jnp.zeros((ang.shape[0], n), F32)
    cos, sin = jnp.cos(ang), jnp.sin(ang)
    return (jnp.concatenate([z(64), cos, cos, z(32)], axis=-1),
            jnp.concatenate([z(64), sin, sin, z(32)], axis=-1))


def _row_tile(n, want):
    t = min(n, want)
    assert n % t == 0, (n, t)
    return t


def kernel(x, positions, mix_norm, w_in, hg_lb_logits, hg_norm, mla_q_norm, mla_kv_norm, mla_w_uq, mla_w_ukv,
           ssm_conv_w, ssm_conv_b, ssm_dt_bias, ssm_a_log, ssm_d, ssm_norm, w_br_hg, w_br_mla, w_br_ssm,
           w_out, mlp_norm, w_up, w_down, final_norm):
    bsz, t, d = x.shape
    depth = w_in.shape[0]
    assert d == D_MODEL and t % SSD_CHUNK == 0
    m = bsz * t
    tm = _row_tile(m, 512)
    tm_in = _row_tile(m, 1024)
    tq = _row_tile(t, 512)

    lbs = jnp.cumsum(jax.nn.softmax(hg_lb_logits.astype(F32), axis=0), axis=0)
    lbs = lbs - lbs[0:1]
    w_in_p = _pack_w_in(w_in)
    wq_p = _pack_w_uq(mla_w_uq)
    wkv_p = _pack_w_ukv(mla_w_ukv)
    vone = _v_ones()
    cos_t, sin_t = _rope_tables(positions)
    pad_h = lambda a: jnp.pad(a.astype(F32), ((0, 0), (DT_LANE, LANES - DT_LANE - SSM_HEADS)))
    dtb_p, alog_p = pad_h(ssm_dt_bias), pad_h(ssm_a_log)
    dsk_e = jnp.repeat(ssm_d.astype(F32), SSM_HEAD_DIM, axis=-1)
    bf = lambda a: a.astype(BF16)
    wa, wb, wc, wo, wu, wd = bf(w_br_hg), bf(w_br_mla), bf(w_br_ssm), bf(w_out), bf(w_up), bf(w_down)
    r1 = lambda a: a.reshape(1, -1).astype(F32)

    x2 = x.reshape(m, d)
    for l in range(depth):
        proj = _inproj(x2, r1(mix_norm[l]), w_in_p[l], tm_in, PROJ_COLS // 2)
        ya = _hgrn(proj, r1(lbs[l]), r1(hg_norm[l]), bsz, t, tq)
        q, k, v = _mla_prep(proj, cos_t, sin_t, r1(mla_q_norm[l]), r1(mla_kv_norm[l]), wq_p[l], wkv_p[l],
                            vone, tm)
        yb = _attn(q, k, v, bsz, t, tq)
        yc = _ssd(proj, ssm_conv_w[l].astype(F32), r1(ssm_conv_b[l]), r1(dtb_p[l]), r1(alog_p[l]),
                  r1(dsk_e[l]), r1(ssm_norm[l]), bsz, t, tq)
        x2 = _merge(ya, yb, yc, proj, x2, wa[l], wb[l], wc[l], wo[l], tm)
        x2 = _mlp(x2, r1(mlp_norm[l]), wu[l], wd[l], r1(final_norm), tm, last_layer=(l == depth - 1))
    return x2.reshape(bsz, t, d)
```

```python
import functools
import math

import numpy as np
import jax
import jax.numpy as jnp
from jax import lax
from jax.experimental import pallas as pl
from jax.experimental.pallas import tpu as pltpu

F32 = jnp.float32
BF16 = jnp.bfloat16

D_MODEL = 1024
EPS = 1e-6
HG_HEADS = 4
HG_DK = 128
HG_DV = 128
HG_WIDTH = HG_HEADS * HG_DV
MLA_HEADS = 8
MLA_NOPE = 64
MLA_ROPE = 32
MLA_V = 64
MLA_Q_LORA = 384
MLA_KV_LORA = 256
MLA_WIDTH = MLA_HEADS * MLA_V
ROPE_THETA = 10000.0
SSM_HEADS = 8
SSM_HEAD_DIM = 64
SSM_WIDTH = SSM_HEADS * SSM_HEAD_DIM
SSM_GROUPS = 2
SSM_STATE = 64
SSM_CONV = 4
SSM_CONV_DIM = SSM_WIDTH + 2 * SSM_GROUPS * SSM_STATE
D_FF = 4 * D_MODEL
N_BRANCH = 3

LANES = 128
HEAD_PAD = 128

OFF_GATE = 0
OFF_HQ, OFF_HF, OFF_HI, OFF_HG = 3072, 3584, 4096, 4608
OFF_SZ = 5120
OFF_MISC = 5632
DT_LANE = 32
OFF_CQ = 5760
OFF_SXBC = 6144
OFF_CKV = 6912
PROJ_COLS = 7168

HG_CHUNK = 64
HG_LEVELS = 6
SSD_CHUNK = 128
ATTN_HEADS = 4

VMEM_LIMIT = 48 * 1024 * 1024


def _cparams(sem, flags=None):
    return pltpu.CompilerParams(dimension_semantics=sem, vmem_limit_bytes=VMEM_LIMIT, flags=flags)


def _sigmoid(x):
    return jax.nn.sigmoid(x)


def _split3(x):
    hi = x.astype(BF16)
    r1 = x - hi.astype(F32)
    mid = r1.astype(BF16)
    r2 = r1 - mid.astype(F32)
    return hi, mid, r2.astype(BF16)


def _dot(a, b):
    return jnp.dot(a, b, preferred_element_type=F32)


def _dot_nt(a, b):
    return lax.dot_general(a, b, (((1,), (1,)), ((), ())), preferred_element_type=F32)


N_SPLIT = 3


def _sel_rows(sel3, x):
    return _dot(sel3, jnp.concatenate(_split3(x), axis=0))


def _sel_cols(x, sel3):
    return _dot(jnp.concatenate(_split3(x), axis=1), sel3)


def _rms(x, g):
    ms = jnp.mean(x * x, axis=-1, keepdims=True)
    return x * lax.rsqrt(ms + EPS) * g


def _inproj_body(x_ref, g_ref, w_ref, o_ref):
    h = _rms(x_ref[...], g_ref[...]).astype(BF16)
    o_ref[...] = _dot(h, w_ref[...]).astype(o_ref.dtype)


def _inproj(x2, g, w, tm, tn):
    m, d = x2.shape
    n = w.shape[1]
    return pl.pallas_call(
        _inproj_body,
        grid=(n // tn, m // tm),
        in_specs=[pl.BlockSpec((tm, d), lambda j, i: (i, 0)),
                  pl.BlockSpec((1, d), lambda j, i: (0, 0)),
                  pl.BlockSpec((d, tn), lambda j, i: (0, j))],
        out_specs=pl.BlockSpec((tm, tn), lambda j, i: (i, j)),
        out_shape=jax.ShapeDtypeStruct((m, n), BF16),
        compiler_params=_cparams(("arbitrary", "arbitrary")),
        name="inproj",
    )(x2, g, w)


def _hgrn_level_matrix():
    c = HG_CHUNK
    mat = np.zeros((8, c, c), np.float32)
    for i in range(c):
        mat[0, i, : i + 1] = 1.0
        mat[7, i, i + 1:] = 1.0
    for lvl in range(HG_LEVELS):
        m = 1 << lvl
        for r in range(c):
            start = (r // (2 * m)) * 2 * m
            mid = start + m
            if r >= mid:
                mat[lvl + 1, r, mid: r + 1] = 1.0
            else:
                mat[lvl + 1, r, r + 1: mid] = 1.0
    return mat.reshape(8 * c, c)


def _hgrn_pair_masks():
    n = 2 * HG_CHUNK
    row, col = np.meshgrid(np.arange(n), np.arange(n), indexing="ij")
    out = [row == col]
    for lvl in range(HG_LEVELS):
        same_block = (row >> (lvl + 1)) == (col >> (lvl + 1))
        out.append(same_block & (((row >> lvl) & 1) == 1) & (((col >> lvl) & 1) == 0))
    return np.stack(out).astype(np.float32)


def _hgrn_body(q_ref, f_ref, i_ref, g_ref, lb_ref, ng_ref, m_ref, mask_ref, o_ref, s_ref, *, n_chunks):
    c = HG_CHUNK

    @pl.when(pl.program_id(1) == 0)
    def _():
        s_ref[...] = jnp.zeros_like(s_ref)

    lb = lb_ref[...]
    log_lb = jnp.log(lb)
    log1m_lb = jnp.log1p(-lb)
    one_m_lb = 1.0 - lb
    ng = ng_ref[...]
    sel = m_ref[...]

    rowv = lax.broadcasted_iota(jnp.int32, (2 * c, HG_DK), 0)
    second_half = [((rowv >> lvl) & 1) == 1 for lvl in range(HG_LEVELS)]

    def chunk(ci, carry):
        r0 = pl.multiple_of(ci * c, c)
        rows = pl.ds(r0, c)
        hq = q_ref[rows, :].astype(F32)
        ff = f_ref[rows, :].astype(F32)
        hv = i_ref[rows, :].astype(F32)
        hg = g_ref[rows, :].astype(F32)
        q = hq * _sigmoid(hq)
        e_f = jnp.exp(-jnp.abs(ff))
        one_p = 1.0 + e_f
        log_sig = jnp.minimum(ff, 0.0) - jnp.log(one_p)
        b = log1m_lb + log_sig
        lf = jnp.maximum(log_lb, b) + jnp.log(1.0 + jnp.exp(-jnp.abs(log_lb - b)))
        k = one_m_lb * jnp.where(ff >= 0.0, e_f, 1.0) / one_p
        gate = hg * _sigmoid(hg)
        ex = _sel_rows(sel, lf)

        for pair in range(HG_HEADS // 2):
            sls = [slice(h * HG_DK, (h + 1) * HG_DK) for h in (2 * pair, 2 * pair + 1)]
            stack = lambda a, lo=0, hi=c: jnp.concatenate([a[lo:hi, sl] for sl in sls], axis=0)
            level = lambda n: stack(ex, n * c, (n + 1) * c)
            qp, kp, vp = stack(q), stack(k), stack(hv)
            att = mask_ref[0] * _dot_nt(qp.astype(BF16), kp.astype(BF16))
            for lvl in range(HG_LEVELS):
                xl = (jnp.where(second_half[lvl], qp, kp) * jnp.exp(level(lvl + 1))).astype(BF16)
                att = att + mask_ref[lvl + 1] * _dot_nt(xl, xl)
            o_intra = _dot(att.astype(BF16), vp.astype(BF16))
            e0 = level(0)
            q_in = (qp * jnp.exp(e0)).astype(BF16)
            k_out = (kp * jnp.exp(level(7))).astype(BF16)
            for idx, sl in enumerate(sls):
                h = 2 * pair + idx
                r = slice(idx * c, (idx + 1) * c)
                st = s_ref[h]
                o = o_intra[r] + _dot_nt(q_in[r], st.astype(BF16))
                decay = jnp.exp(e0[(idx + 1) * c - 1:(idx + 1) * c, :])
                s_ref[h] = st * decay + _dot(vp[r].T.astype(BF16), k_out[r])
                y = _rms(o, ng[:, sl]) * gate[:, sl]
                o_ref[rows, sl] = y.astype(o_ref.dtype)
        return carry

    lax.fori_loop(0, n_chunks, chunk, 0, unroll=4)


def _hgrn(proj, lb, ng, bsz, t, tq):
    m = proj.shape[0]
    nq = t // tq
    sel = jnp.asarray(np.tile(_hgrn_level_matrix(), (1, N_SPLIT)), BF16)
    masks = jnp.asarray(_hgrn_pair_masks())
    w = HG_WIDTH

    def col_spec(off):
        blk = off // w
        return pl.BlockSpec((tq, w), lambda b, i: (b * nq + i, blk))

    vec = pl.BlockSpec((1, w), lambda b, i: (0, 0))
    return pl.pallas_call(
        functools.partial(_hgrn_body, n_chunks=tq // HG_CHUNK),
        grid=(bsz, nq),
        in_specs=[col_spec(OFF_HQ), col_spec(OFF_HF), col_spec(OFF_HI), col_spec(OFF_HG), vec, vec,
                  pl.BlockSpec(sel.shape, lambda b, i: (0, 0)),
                  pl.BlockSpec(masks.shape, lambda b, i: (0, 0, 0))],
        out_specs=pl.BlockSpec((tq, w), lambda b, i: (b * nq + i, 0)),
        out_shape=jax.ShapeDtypeStruct((m, w), BF16),
        scratch_shapes=[pltpu.VMEM((HG_HEADS, HG_DV, HG_DK), F32)],
        compiler_params=_cparams(("arbitrary", "arbitrary")),
        name="hgrn2",
    )(proj, proj, proj, proj, lb, ng, sel, masks)


def _mla_prep_body(cq_ref, ckv_ref, kr_ref, cos_ref, sin_ref, qn_ref, kvn_ref, wq_ref, wkv_ref, vone_ref,
                   q_out, k_out, v_out):
    scale = (MLA_NOPE + MLA_ROPE) ** -0.5 * math.log2(math.e)
    cqn = _rms(cq_ref[...].astype(F32), qn_ref[...]).astype(BF16)
    ckvn = _rms(ckv_ref[...].astype(F32), kvn_ref[...]).astype(BF16)
    qq = _dot(cqn, wq_ref[...])
    kv = _dot(ckvn, wkv_ref[...])
    cos = cos_ref[...]
    sin = sin_ref[...]
    lane = lax.broadcasted_iota(jnp.int32, (1, HEAD_PAD), 1)
    cq_t = scale * (cos + (lane < MLA_NOPE).astype(F32))
    sq_t = scale * sin
    krb = kr_ref[...].astype(F32)
    k_rope = krb * cos + pltpu.roll(krb, 64, 1) * sin
    hw = MLA_HEADS * HEAD_PAD
    vone = vone_ref[...]
    for h in range(MLA_HEADS):
        sl = slice(h * HEAD_PAD, (h + 1) * HEAD_PAD)
        sl2 = slice(hw + h * HEAD_PAD, hw + (h + 1) * HEAD_PAD)
        q_out[:, sl] = (qq[:, sl] * cq_t + qq[:, sl2] * sq_t).astype(BF16)
        k_out[:, sl] = (kv[:, sl] + k_rope).astype(BF16)
        v_out[:, sl] = (kv[:, sl2] + vone[:, sl]).astype(BF16)


def _mla_prep(proj, cos_t, sin_t, qn, kvn, wq, wkv, vone, tm):
    m = proj.shape[0]
    hw = MLA_HEADS * HEAD_PAD
    full = lambda a: pl.BlockSpec(a.shape, lambda i: (0, 0))
    out = jax.ShapeDtypeStruct((m, hw), BF16)
    ospec = pl.BlockSpec((tm, hw), lambda i: (i, 0))
    return pl.pallas_call(
        _mla_prep_body,
        grid=(m // tm,),
        in_specs=[pl.BlockSpec((tm, MLA_Q_LORA), lambda i: (i, OFF_CQ // MLA_Q_LORA)),
                  pl.BlockSpec((tm, MLA_KV_LORA), lambda i: (i, OFF_CKV // MLA_KV_LORA)),
                  pl.BlockSpec((tm, LANES), lambda i: (i, OFF_MISC // LANES)),
                  pl.BlockSpec((tm, LANES), lambda i: (i, 0)),
                  pl.BlockSpec((tm, LANES), lambda i: (i, 0)),
                  full(qn), full(kvn), full(wq), full(wkv), full(vone)],
        out_specs=[ospec, ospec, ospec],
        out_shape=[out, out, out],
        compiler_params=_cparams(("arbitrary",)),
        name="mla_prep",
    )(proj, proj, proj, cos_t, sin_t, qn, kvn, wq, wkv, vone)


def _attn_body(q_ref, k_ref, v_ref, o_ref, *, tq):
    i = pl.program_id(2)
    lane = lax.broadcasted_iota(jnp.int32, (1, HEAD_PAD), 1)
    qpos = lax.broadcasted_iota(jnp.int32, (tq, tq), 0)
    kpos = lax.broadcasted_iota(jnp.int32, (tq, tq), 1)
    causal = kpos <= qpos
    heads = range(ATTN_HEADS)
    slices = [slice(h * HEAD_PAD, (h + 1) * HEAD_PAD) for h in heads]
    block = lambda j: pl.ds(pl.multiple_of(j * tq, tq), tq)

    def scores(h, j, masked, m_prev):
        s = _dot_nt(q_ref[:, slices[h]], k_ref[block(j), slices[h]])
        if masked:
            s = jnp.where(causal, s, -jnp.inf)
        m_new = jnp.maximum(m_prev, jnp.max(s, axis=-1, keepdims=True))
        return m_new, jnp.exp2(m_prev - m_new), jnp.exp2((s - m_new).astype(BF16))

    def step(j, state, masked):
        new = []
        for h in heads:
            m, acc = state[h]
            m, alpha, p = scores(h, j, masked, m)
            new.append((m, alpha * acc + _dot(p, v_ref[block(j), slices[h]])))
        return tuple(new)

    init = (jnp.full((tq, 1), -jnp.inf, F32), jnp.zeros((tq, HEAD_PAD), F32))
    state = lax.fori_loop(0, i, functools.partial(step, masked=False), (init,) * ATTN_HEADS)
    state = step(i, state, True)
    for pair in range(ATTN_HEADS // 2):
        acc0, acc1 = state[2 * pair][1], state[2 * pair + 1][1]
        den0 = jnp.sum(jnp.where(lane == MLA_V, acc0, 0.0), axis=-1, keepdims=True)
        den1 = jnp.sum(jnp.where(lane == 0, acc1, 0.0), axis=-1, keepdims=True)
        o_ref[:, pair * HEAD_PAD:(pair + 1) * HEAD_PAD] = jnp.where(
            lane < MLA_V, acc0 / den0, acc1 / den1).astype(o_ref.dtype)


def _attn(q, k, v, bsz, t, tq):
    m = q.shape[0]
    nq = t // tq
    pw = ATTN_HEADS * HEAD_PAD
    return pl.pallas_call(
        functools.partial(_attn_body, tq=tq),
        grid=(bsz, MLA_HEADS // ATTN_HEADS, nq),
        in_specs=[pl.BlockSpec((tq, pw), lambda b, h, i: (b * nq + i, h)),
                  pl.BlockSpec((t, pw), lambda b, h, i: (b, h)),
                  pl.BlockSpec((t, pw), lambda b, h, i: (b, h))],
        out_specs=pl.BlockSpec((tq, ATTN_HEADS * MLA_V), lambda b, h, i: (b * nq + i, h)),
        out_shape=jax.ShapeDtypeStruct((m, MLA_WIDTH), BF16),
        compiler_params=_cparams(("arbitrary", "arbitrary", "arbitrary")),
        name="mla_attn",
    )(q, k, v)


def _softplus(x):
    return jnp.maximum(x, 0.0) + jnp.log1p(jnp.exp(-jnp.abs(x)))


def _ssd_body(z_ref, dt_ref, xbc_ref, cw_ref, cb_ref, dtb_ref, alog_ref, dsk_ref, ng_ref, tril_ref, e8_ref,
              o_ref, xp_ref, xc_ref, s_ref, *, tq):
    L = SSD_CHUNK
    gn = SSM_GROUPS * SSM_STATE
    half = SSM_WIDTH // SSM_GROUPS
    t_idx = pl.program_id(1)

    @pl.when(t_idx == 0)
    def _():
        s_ref[...] = jnp.zeros_like(s_ref)
        xp_ref[0:8, :] = jnp.zeros((8, SSM_CONV_DIM), F32)

    @pl.when(t_idx > 0)
    def _():
        xp_ref[0:8, :] = xp_ref[tq:tq + 8, :]

    xp_ref[8:8 + tq, :] = xbc_ref[...].astype(F32)
    cw = cw_ref[...]
    acc = cb_ref[...] + cw[3:4, :] * xp_ref[8:8 + tq, :]
    for w in range(SSM_CONV - 1):
        acc = acc + cw[w:w + 1, :] * xp_ref[5 + w:5 + w + tq, :]
    xc_ref[...] = acc * _sigmoid(acc)

    a_neg = -jnp.exp(alog_ref[...])
    dtb = dtb_ref[...]
    dsk = dsk_ref[...]
    ng = ng_ref[...]
    tril = tril_ref[...]
    e8 = e8_ref[...]
    ri = lax.broadcasted_iota(jnp.int32, (L, L), 0)
    ci = lax.broadcasted_iota(jnp.int32, (L, L), 1)
    tri = ci <= ri
    lane_gn = lax.broadcasted_iota(jnp.int32, (1, gn), 1)
    lane_w = lax.broadcasted_iota(jnp.int32, (1, SSM_WIDTH), 1)
    row_gn = lax.broadcasted_iota(jnp.int32, (gn, SSM_WIDTH), 0)
    col_w = lax.broadcasted_iota(jnp.int32, (gn, SSM_WIDTH), 1)
    blockdiag = (row_gn // SSM_STATE) == (col_w // half)
    head_mask = [(lane_w // SSM_HEAD_DIM) == h for h in range(SSM_HEADS)]

    def chunk(c, carry):
        rows = pl.ds(pl.multiple_of(c * L, L), L)
        xc = xc_ref[rows, :]
        xs = xc[:, :SSM_WIDTH]
        bm = xc[:, SSM_WIDTH:SSM_WIDTH + gn]
        cm = xc[:, SSM_WIDTH + gn:]
        dt = _softplus(dt_ref[rows, :].astype(F32) + dtb)
        a = dt * a_neg
        acum = _sel_rows(tril, a)
        dt_e = _sel_cols(dt, e8)
        ac_e = _sel_cols(acum, e8)
        x_dt = xs * dt_e
        last = ac_e[L - 1:L, :]
        x_dec = (x_dt * jnp.exp(last - ac_e)).astype(BF16)
        x_b = x_dt.astype(BF16)
        bm_t = bm.T.astype(BF16)
        cb0 = _dot(jnp.where(lane_gn < SSM_STATE, cm, 0.0).astype(BF16), bm_t)
        cb1 = _dot(jnp.where(lane_gn >= SSM_STATE, cm, 0.0).astype(BF16), bm_t)
        ac_t = acum.T
        ws, xblk = [], []
        for h in range(SSM_HEADS):
            hl = DT_LANE + h
            seg = acum[:, hl:hl + 1] - ac_t[hl:hl + 1, :]
            lmat = jnp.exp(jnp.where(tri, seg, -jnp.inf))
            ws.append((lmat * (cb0 if h < SSM_HEADS // SSM_GROUPS else cb1)).astype(BF16))
            xblk.append(jnp.where(head_mask[h], x_b, jnp.zeros_like(x_b)))
        y = _dot(jnp.concatenate(ws, axis=1), jnp.concatenate(xblk, axis=0))
        st = s_ref[...]
        y = y + jnp.exp(ac_e) * _dot(cm.astype(BF16), st.astype(BF16))
        s_ref[...] = jnp.exp(last) * st + jnp.where(blockdiag, _dot(bm_t, x_dec), 0.0)
        y = y + xs * dsk
        z = z_ref[rows, :].astype(F32)
        y = y * (z * _sigmoid(z))
        y0 = _rms(y[:, :half], ng[:, :half])
        y1 = _rms(y[:, half:], ng[:, half:])
        o_ref[rows, :half] = y0.astype(o_ref.dtype)
        o_ref[rows, half:] = y1.astype(o_ref.dtype)
        return carry

    lax.fori_loop(0, tq // L, chunk, 0, unroll=2)


def _ssd(proj, cw, cb, dtb, alog, dsk, ng, bsz, t, tq):
    m = proj.shape[0]
    nq = t // tq
    L = SSD_CHUNK
    tril = jnp.asarray(np.tile(np.tril(np.ones((L, L), np.float32)), (1, N_SPLIT)), BF16)
    e8 = np.zeros((LANES, SSM_WIDTH), np.float32)
    for h in range(SSM_HEADS):
        e8[DT_LANE + h, h * SSM_HEAD_DIM:(h + 1) * SSM_HEAD_DIM] = 1.0
    e8 = jnp.asarray(np.tile(e8, (N_SPLIT, 1)), BF16)
    full = lambda a: pl.BlockSpec(a.shape, lambda b, i: (0, 0))
    return pl.pallas_call(
        functools.partial(_ssd_body, tq=tq),
        grid=(bsz, nq),
        in_specs=[pl.BlockSpec((tq, SSM_WIDTH), lambda b, i: (b * nq + i, OFF_SZ // SSM_WIDTH)),
                  pl.BlockSpec((tq, LANES), lambda b, i: (b * nq + i, OFF_MISC // LANES)),
                  pl.BlockSpec((tq, SSM_CONV_DIM), lambda b, i: (b * nq + i, OFF_SXBC // SSM_CONV_DIM)),
                  full(cw), full(cb), full(dtb), full(alog), full(dsk), full(ng), full(tril), full(e8)],
        out_specs=pl.BlockSpec((tq, SSM_WIDTH), lambda b, i: (b * nq + i, 0)),
        out_shape=jax.ShapeDtypeStruct((m, SSM_WIDTH), BF16),
        scratch_shapes=[pltpu.VMEM((tq + 8, SSM_CONV_DIM), F32),
                        pltpu.VMEM((tq, SSM_CONV_DIM), F32),
                        pltpu.VMEM((SSM_GROUPS * SSM_STATE, SSM_WIDTH), F32)],
        compiler_params=_cparams(("arbitrary", "arbitrary")),
        name="ssd",
    )(proj, proj, proj, cw, cb, dtb, alog, dsk, ng, tril, e8)


def _merge_body(ya_ref, yb_ref, yc_ref, gate_ref, x_ref, wa_ref, wb_ref, wc_ref, wo_ref, o_ref):
    d = D_MODEL
    gate = lambda b: _sigmoid(gate_ref[:, b * d:(b + 1) * d].astype(F32))
    merged = (gate(0) * _dot(ya_ref[...], wa_ref[...])
              + gate(1) * _dot(yb_ref[...], wb_ref[...])
              + gate(2) * _dot(yc_ref[...], wc_ref[...]))
    o_ref[...] = x_ref[...] + _dot(merged.astype(BF16), wo_ref[...])


def _merge(ya, yb, yc, proj, x2, wa, wb, wc, wo, tm):
    m, d = x2.shape
    full = lambda a: pl.BlockSpec(a.shape, lambda i: (0, 0))
    row = lambda w: pl.BlockSpec((tm, w), lambda i: (i, 0))
    return pl.pallas_call(
        _merge_body,
        grid=(m // tm,),
        in_specs=[row(HG_WIDTH), row(MLA_WIDTH), row(SSM_WIDTH), row(N_BRANCH * d), row(d),
                  full(wa), full(wb), full(wc), full(wo)],
        out_specs=row(d),
        out_shape=jax.ShapeDtypeStruct((m, d), F32),
        compiler_params=_cparams(("arbitrary",)),
        name="merge_out",
    )(ya, yb, yc, proj, x2, wa, wb, wc, wo)


def _mlp_body(x_ref, g_ref, wu_ref, wd_ref, fg_ref, o_ref, *, last_layer):
    x = x_ref[...]
    h = _rms(x, g_ref[...]).astype(BF16)
    u = jnp.maximum(_dot(h, wu_ref[...]), 0.0)
    y = x + _dot((u * u).astype(BF16), wd_ref[...])
    o_ref[...] = _rms(y, fg_ref[...]) if last_layer else y


def _mlp(x2, g, wu, wd, final_g, tm, last_layer):
    m, d = x2.shape
    const = lambda a: pl.BlockSpec(a.shape, lambda i: (0, 0), pipeline_mode=pl.Buffered(1))
    return pl.pallas_call(
        functools.partial(_mlp_body, last_layer=last_layer),
        grid=(m // tm,),
        in_specs=[pl.BlockSpec((tm, d), lambda i: (i, 0)), const(g), const(wu), const(wd), const(final_g)],
        out_specs=pl.BlockSpec((tm, d), lambda i: (i, 0)),
        out_shape=jax.ShapeDtypeStruct((m, d), F32),
        compiler_params=_cparams(("arbitrary",)),
        name="mlp",
    )(x2, g, wu, wd, final_g)


def _rot_cols(w):
    hr = MLA_ROPE // 2
    return jnp.concatenate([-w[..., hr:], w[..., :hr]], axis=-1)


def _pack_w_in(w_in):
    dep, d, _ = w_in.shape
    sizes = (512, 512, 512, 512, MLA_Q_LORA, MLA_KV_LORA, MLA_ROPE, SSM_WIDTH, SSM_CONV_DIM, SSM_HEADS,
             N_BRANCH * D_MODEL)
    pts = np.cumsum(sizes)[:-1].tolist()
    hq, hf, hi, hg, cq, ckv, kr, sz, sxbc, sdt, gate = jnp.split(w_in, pts, axis=-1)
    z = lambda n: jnp.zeros((dep, d, n), w_in.dtype)
    misc = jnp.concatenate([_rot_cols(kr), sdt, z(32 - SSM_HEADS), kr, z(32)], axis=-1)
    packed = jnp.concatenate([gate, hq, hf, hi, hg, sz, misc, cq, sxbc, ckv], axis=-1)
    assert packed.shape[-1] == PROJ_COLS
    return packed.astype(BF16)


def _pack_w_uq(w):
    dep, r, _ = w.shape
    wh = w.reshape(dep, r, MLA_HEADS, MLA_NOPE + MLA_ROPE)
    nope, rope = wh[..., :MLA_NOPE], wh[..., MLA_NOPE:]
    z = lambda n: jnp.zeros((dep, r, MLA_HEADS, n), w.dtype)
    a = jnp.concatenate([nope, rope, z(32)], axis=-1).reshape(dep, r, MLA_HEADS * HEAD_PAD)
    b = jnp.concatenate([z(64), _rot_cols(rope), z(32)], axis=-1).reshape(dep, r, MLA_HEADS * HEAD_PAD)
    return jnp.concatenate([a, b], axis=-1).astype(BF16)


def _pack_w_ukv(w):
    dep, r, _ = w.shape
    wh = w.reshape(dep, r, MLA_HEADS // 2, 2, MLA_NOPE + MLA_V)
    kn, v = wh[..., :MLA_NOPE], wh[..., MLA_NOPE:]
    z = jnp.zeros_like(kn)
    ka = jnp.concatenate([kn, z], axis=-1).reshape(dep, r, MLA_HEADS * HEAD_PAD)
    va = jnp.stack([jnp.concatenate([v[..., 0, :], z[..., 0, :]], axis=-1),
                    jnp.concatenate([z[..., 1, :], v[..., 1, :]], axis=-1)], axis=-2)
    va = va.reshape(dep, r, MLA_HEADS * HEAD_PAD)
    return jnp.concatenate([ka, va], axis=-1).astype(BF16)


def _v_ones():
    v = np.zeros((1, MLA_HEADS * HEAD_PAD), np.float32)
    for h in range(MLA_HEADS):
        v[0, h * HEAD_PAD + (MLA_V if h % 2 == 0 else 0)] = 1.0
    return jnp.asarray(v)


def _rope_tables(positions):
    inv = ROPE_THETA ** (-jnp.arange(0, MLA_ROPE, 2, dtype=F32) / MLA_ROPE)
    ang = positions.astype(F32).reshape(-1, 1) * inv
    z = lambda n: jnp.zeros((ang.shape[0], n), F32)
    cos, sin = jnp.cos(ang), jnp.sin(ang)
    return (jnp.concatenate([z(64), cos, cos, z(32)], axis=-1),
            jnp.concatenate([z(64), sin, sin, z(32)], axis=-1))


def _row_tile(n, want):
    t = min(n, want)
    assert n % t == 0, (n, t)
    return t


def kernel(x, positions, mix_norm, w_in, hg_lb_logits, hg_norm, mla_q_norm, mla_kv_norm, mla_w_uq, mla_w_ukv,
           ssm_conv_w, ssm_conv_b, ssm_dt_bias, ssm_a_log, ssm_d, ssm_norm, w_br_hg, w_br_mla, w_br_ssm,
           w_out, mlp_norm, w_up, w_down, final_norm):
    bsz, t, d = x.shape
    depth = w_in.shape[0]
    assert d == D_MODEL and t % SSD_CHUNK == 0
    m = bsz * t
    tm = _row_tile(m, 512)
    tm_in = _row_tile(m, 1024)
    tq = _row_tile(t, 512)

    lbs = jnp.cumsum(jax.nn.softmax(hg_lb_logits.astype(F32), axis=0), axis=0)
    lbs = lbs - lbs[0:1]
    w_in_p = _pack_w_in(w_in)
    wq_p = _pack_w_uq(mla_w_uq)
    wkv_p = _pack_w_ukv(mla_w_ukv)
    vone = _v_ones()
    cos_t, sin_t = _rope_tables(positions)
    pad_h = lambda a: jnp.pad(a.astype(F32), ((0, 0), (DT_LANE, LANES - DT_LANE - SSM_HEADS)))
    dtb_p, alog_p = pad_h(ssm_dt_bias), pad_h(ssm_a_log)
    dsk_e = jnp.repeat(ssm_d.astype(F32), SSM_HEAD_DIM, axis=-1)
    bf = lambda a: a.astype(BF16)
    wa, wb, wc, wo, wu, wd = bf(w_br_hg), bf(w_br_mla), bf(w_br_ssm), bf(w_out), bf(w_up), bf(w_down)
    r1 = lambda a: a.reshape(1, -1).astype(F32)

    x2 = x.reshape(m, d)
    for l in range(depth):
        proj = _inproj(x2, r1(mix_norm[l]), w_in_p[l], tm_in, PROJ_COLS // 2)
        ya = _hgrn(proj, r1(lbs[l]), r1(hg_norm[l]), bsz, t, tq)
        q, k, v = _mla_prep(proj, cos_t, sin_t, r1(mla_q_norm[l]), r1(mla_kv_norm[l]), wq_p[l], wkv_p[l],
                            vone, tm)
        yb = _attn(q, k, v, bsz, t, tq)
        yc = _ssd(proj, ssm_conv_w[l].astype(F32), r1(ssm_conv_b[l]), r1(dtb_p[l]), r1(alog_p[l]),
                  r1(dsk_e[l]), r1(ssm_norm[l]), bsz, t, tq)
        x2 = _merge(ya, yb, yc, proj, x2, wa[l], wb[l], wc[l], wo[l], tm)
        x2 = _mlp(x2, r1(mlp_norm[l]), wu[l], wd[l], r1(final_norm), tm, last_layer=(l == depth - 1))
    return x2.reshape(bsz, t, d)
```

```python
import functools
import math

import numpy as np
import jax
import jax.numpy as jnp
from jax import lax
from jax.experimental import pallas as pl
from jax.experimental.pallas import tpu as pltpu

F32 = jnp.float32
BF16 = jnp.bfloat16

D_MODEL = 1024
EPS = 1e-6
HG_HEADS = 4
HG_DK = 128
HG_DV = 128
HG_WIDTH = HG_HEADS * HG_DV
MLA_HEADS = 8
MLA_NOPE = 64
MLA_ROPE = 32
MLA_V = 64
MLA_Q_LORA = 384
MLA_KV_LORA = 256
MLA_WIDTH = MLA_HEADS * MLA_V
ROPE_THETA = 10000.0
SSM_HEADS = 8
SSM_HEAD_DIM = 64
SSM_WIDTH = SSM_HEADS * SSM_HEAD_DIM
SSM_GROUPS = 2
SSM_STATE = 64
SSM_CONV = 4
SSM_CONV_DIM = SSM_WIDTH + 2 * SSM_GROUPS * SSM_STATE
D_FF = 4 * D_MODEL
N_BRANCH = 3

LANES = 128
HEAD_PAD = 128

OFF_GATE = 0
OFF_HQ, OFF_HF, OFF_HI, OFF_HG = 3072, 3584, 4096, 4608
OFF_SZ = 5120
OFF_MISC = 5632
DT_LANE = 32
OFF_CQ = 5760
OFF_SXBC = 6144
OFF_CKV = 6912
PROJ_COLS = 7168

HG_CHUNK = 64
HG_LEVELS = 6
SSD_CHUNK = 128
ATTN_HEADS = 8

VMEM_LIMIT = 48 * 1024 * 1024


def _cparams(sem, flags=None):
    return pltpu.CompilerParams(dimension_semantics=sem, vmem_limit_bytes=VMEM_LIMIT, flags=flags)


def _sigmoid(x):
    return jax.nn.sigmoid(x)


def _split3(x):
    hi = x.astype(BF16)
    r1 = x - hi.astype(F32)
    mid = r1.astype(BF16)
    r2 = r1 - mid.astype(F32)
    return hi, mid, r2.astype(BF16)


def _dot(a, b):
    return jnp.dot(a, b, preferred_element_type=F32)


def _dot_nt(a, b):
    return lax.dot_general(a, b, (((1,), (1,)), ((), ())), preferred_element_type=F32)


N_SPLIT = 3


def _sel_rows(sel3, x):
    return _dot(sel3, jnp.concatenate(_split3(x), axis=0))


def _sel_cols(x, sel3):
    return _dot(jnp.concatenate(_split3(x), axis=1), sel3)


def _rms(x, g):
    ms = jnp.mean(x * x, axis=-1, keepdims=True)
    return x * lax.rsqrt(ms + EPS) * g


def _inproj_body(x_ref, g_ref, w_ref, o_ref):
    h = _rms(x_ref[...], g_ref[...]).astype(BF16)
    o_ref[...] = _dot(h, w_ref[...]).astype(o_ref.dtype)


def _inproj(x2, g, w, tm, tn):
    m, d = x2.shape
    n = w.shape[1]
    return pl.pallas_call(
        _inproj_body,
        grid=(n // tn, m // tm),
        in_specs=[pl.BlockSpec((tm, d), lambda j, i: (i, 0)),
                  pl.BlockSpec((1, d), lambda j, i: (0, 0)),
                  pl.BlockSpec((d, tn), lambda j, i: (0, j))],
        out_specs=pl.BlockSpec((tm, tn), lambda j, i: (i, j)),
        out_shape=jax.ShapeDtypeStruct((m, n), BF16),
        compiler_params=_cparams(("arbitrary", "arbitrary")),
        name="inproj",
    )(x2, g, w)


def _hgrn_level_matrix():
    c = HG_CHUNK
    mat = np.zeros((8, c, c), np.float32)
    for i in range(c):
        mat[0, i, : i + 1] = 1.0
        mat[7, i, i + 1:] = 1.0
    for lvl in range(HG_LEVELS):
        m = 1 << lvl
        for r in range(c):
            start = (r // (2 * m)) * 2 * m
            mid = start + m
            if r >= mid:
                mat[lvl + 1, r, mid: r + 1] = 1.0
            else:
                mat[lvl + 1, r, r + 1: mid] = 1.0
    return mat.reshape(8 * c, c)


def _hgrn_pair_masks():
    n = 2 * HG_CHUNK
    row, col = np.meshgrid(np.arange(n), np.arange(n), indexing="ij")
    out = [row == col]
    for lvl in range(HG_LEVELS):
        same_block = (row >> (lvl + 1)) == (col >> (lvl + 1))
        out.append(same_block & (((row >> lvl) & 1) == 1) & (((col >> lvl) & 1) == 0))
    return np.stack(out).astype(np.float32)


def _hgrn_body(q_ref, f_ref, i_ref, g_ref, lb_ref, ng_ref, m_ref, mask_ref, o_ref, s_ref, *, n_chunks):
    c = HG_CHUNK

    @pl.when(pl.program_id(1) == 0)
    def _():
        s_ref[...] = jnp.zeros_like(s_ref)

    lb = lb_ref[...]
    log_lb = jnp.log(lb)
    log1m_lb = jnp.log1p(-lb)
    one_m_lb = 1.0 - lb
    ng = ng_ref[...]
    sel = m_ref[...]

    rowv = lax.broadcasted_iota(jnp.int32, (2 * c, HG_DK), 0)
    second_half = [((rowv >> lvl) & 1) == 1 for lvl in range(HG_LEVELS)]

    def chunk(ci, carry):
        r0 = pl.multiple_of(ci * c, c)
        rows = pl.ds(r0, c)
        hq = q_ref[rows, :].astype(F32)
        ff = f_ref[rows, :].astype(F32)
        hv = i_ref[rows, :].astype(F32)
        hg = g_ref[rows, :].astype(F32)
        q = hq * _sigmoid(hq)
        e_f = jnp.exp(-jnp.abs(ff))
        one_p = 1.0 + e_f
        log_sig = jnp.minimum(ff, 0.0) - jnp.log(one_p)
        b = log1m_lb + log_sig
        lf = jnp.maximum(log_lb, b) + jnp.log(1.0 + jnp.exp(-jnp.abs(log_lb - b)))
        k = one_m_lb * jnp.where(ff >= 0.0, e_f, 1.0) / one_p
        gate = hg * _sigmoid(hg)
        ex = _sel_rows(sel, lf)

        for pair in range(HG_HEADS // 2):
            sls = [slice(h * HG_DK, (h + 1) * HG_DK) for h in (2 * pair, 2 * pair + 1)]
            stack = lambda a, lo=0, hi=c: jnp.concatenate([a[lo:hi, sl] for sl in sls], axis=0)
            level = lambda n: stack(ex, n * c, (n + 1) * c)
            qp, kp, vp = stack(q), stack(k), stack(hv)
            att = mask_ref[0] * _dot_nt(qp.astype(BF16), kp.astype(BF16))
            for lvl in range(HG_LEVELS):
                xl = (jnp.where(second_half[lvl], qp, kp) * jnp.exp(level(lvl + 1))).astype(BF16)
                att = att + mask_ref[lvl + 1] * _dot_nt(xl, xl)
            o_intra = _dot(att.astype(BF16), vp.astype(BF16))
            e0 = level(0)
            q_in = (qp * jnp.exp(e0)).astype(BF16)
            k_out = (kp * jnp.exp(level(7))).astype(BF16)
            for idx, sl in enumerate(sls):
                h = 2 * pair + idx
                r = slice(idx * c, (idx + 1) * c)
                st = s_ref[h]
                o = o_intra[r] + _dot_nt(q_in[r], st.astype(BF16))
                decay = jnp.exp(e0[(idx + 1) * c - 1:(idx + 1) * c, :])
                s_ref[h] = st * decay + _dot(vp[r].T.astype(BF16), k_out[r])
                y = _rms(o, ng[:, sl]) * gate[:, sl]
                o_ref[rows, sl] = y.astype(o_ref.dtype)
        return carry

    lax.fori_loop(0, n_chunks, chunk, 0, unroll=8)


def _hgrn(proj, lb, ng, bsz, t, tq):
    m = proj.shape[0]
    nq = t // tq
    sel = jnp.asarray(np.tile(_hgrn_level_matrix(), (1, N_SPLIT)), BF16)
    masks = jnp.asarray(_hgrn_pair_masks())
    w = HG_WIDTH

    def col_spec(off):
        blk = off // w
        return pl.BlockSpec((tq, w), lambda b, i: (b * nq + i, blk))

    vec = pl.BlockSpec((1, w), lambda b, i: (0, 0))
    return pl.pallas_call(
        functools.partial(_hgrn_body, n_chunks=tq // HG_CHUNK),
        grid=(bsz, nq),
        in_specs=[col_spec(OFF_HQ), col_spec(OFF_HF), col_spec(OFF_HI), col_spec(OFF_HG), vec, vec,
                  pl.BlockSpec(sel.shape, lambda b, i: (0, 0)),
                  pl.BlockSpec(masks.shape, lambda b, i: (0, 0, 0))],
        out_specs=pl.BlockSpec((tq, w), lambda b, i: (b * nq + i, 0)),
        out_shape=jax.ShapeDtypeStruct((m, w), BF16),
        scratch_shapes=[pltpu.VMEM((HG_HEADS, HG_DV, HG_DK), F32)],
        compiler_params=_cparams(("arbitrary", "arbitrary")),
        name="hgrn2",
    )(proj, proj, proj, proj, lb, ng, sel, masks)


def _mla_prep_body(cq_ref, ckv_ref, kr_ref, cos_ref, sin_ref, qn_ref, kvn_ref, wq_ref, wkv_ref, vone_ref,
                   q_out, k_out, v_out):
    scale = (MLA_NOPE + MLA_ROPE) ** -0.5 * math.log2(math.e)
    cqn = _rms(cq_ref[...].astype(F32), qn_ref[...]).astype(BF16)
    ckvn = _rms(ckv_ref[...].astype(F32), kvn_ref[...]).astype(BF16)
    qq = _dot(cqn, wq_ref[...])
    kv = _dot(ckvn, wkv_ref[...])
    cos = cos_ref[...]
    sin = sin_ref[...]
    lane = lax.broadcasted_iota(jnp.int32, (1, HEAD_PAD), 1)
    cq_t = scale * (cos + (lane < MLA_NOPE).astype(F32))
    sq_t = scale * sin
    krb = kr_ref[...].astype(F32)
    k_rope = krb * cos + pltpu.roll(krb, 64, 1) * sin
    hw = MLA_HEADS * HEAD_PAD
    vone = vone_ref[...]
    for h in range(MLA_HEADS):
        sl = slice(h * HEAD_PAD, (h + 1) * HEAD_PAD)
        sl2 = slice(hw + h * HEAD_PAD, hw + (h + 1) * HEAD_PAD)
        q_out[:, sl] = (qq[:, sl] * cq_t + qq[:, sl2] * sq_t).astype(BF16)
        k_out[:, sl] = (kv[:, sl] + k_rope).astype(BF16)
        v_out[:, sl] = (kv[:, sl2] + vone[:, sl]).astype(BF16)


def _mla_prep(proj, cos_t, sin_t, qn, kvn, wq, wkv, vone, tm):
    m = proj.shape[0]
    hw = MLA_HEADS * HEAD_PAD
    full = lambda a: pl.BlockSpec(a.shape, lambda i: (0, 0))
    out = jax.ShapeDtypeStruct((m, hw), BF16)
    ospec = pl.BlockSpec((tm, hw), lambda i: (i, 0))
    return pl.pallas_call(
        _mla_prep_body,
        grid=(m // tm,),
        in_specs=[pl.BlockSpec((tm, MLA_Q_LORA), lambda i: (i, OFF_CQ // MLA_Q_LORA)),
                  pl.BlockSpec((tm, MLA_KV_LORA), lambda i: (i, OFF_CKV // MLA_KV_LORA)),
                  pl.BlockSpec((tm, LANES), lambda i: (i, OFF_MISC // LANES)),
                  pl.BlockSpec((tm, LANES), lambda i: (i, 0)),
                  pl.BlockSpec((tm, LANES), lambda i: (i, 0)),
                  full(qn), full(kvn), full(wq), full(wkv), full(vone)],
        out_specs=[ospec, ospec, ospec],
        out_shape=[out, out, out],
        compiler_params=_cparams(("arbitrary",)),
        name="mla_prep",
    )(proj, proj, proj, cos_t, sin_t, qn, kvn, wq, wkv, vone)


def _attn_body(q_ref, k_ref, v_ref, o_ref, *, tq):
    i = pl.program_id(2)
    lane = lax.broadcasted_iota(jnp.int32, (1, HEAD_PAD), 1)
    qpos = lax.broadcasted_iota(jnp.int32, (tq, tq), 0)
    kpos = lax.broadcasted_iota(jnp.int32, (tq, tq), 1)
    causal = kpos <= qpos
    heads = range(ATTN_HEADS)
    slices = [slice(h * HEAD_PAD, (h + 1) * HEAD_PAD) for h in heads]
    block = lambda j: pl.ds(pl.multiple_of(j * tq, tq), tq)

    def scores(h, j, masked, m_prev):
        s = _dot_nt(q_ref[:, slices[h]], k_ref[block(j), slices[h]])
        if masked:
            s = jnp.where(causal, s, -jnp.inf)
        m_new = jnp.maximum(m_prev, jnp.max(s, axis=-1, keepdims=True))
        return m_new, jnp.exp2(m_prev - m_new), jnp.exp2((s - m_new).astype(BF16))

    def step(j, state, masked):
        new = []
        for h in heads:
            m, acc = state[h]
            m, alpha, p = scores(h, j, masked, m)
            new.append((m, alpha * acc + _dot(p, v_ref[block(j), slices[h]])))
        return tuple(new)

    init = (jnp.full((tq, 1), -jnp.inf, F32), jnp.zeros((tq, HEAD_PAD), F32))
    state = lax.fori_loop(0, i, functools.partial(step, masked=False), (init,) * ATTN_HEADS)
    state = step(i, state, True)
    for pair in range(ATTN_HEADS // 2):
        acc0, acc1 = state[2 * pair][1], state[2 * pair + 1][1]
        den0 = jnp.sum(jnp.where(lane == MLA_V, acc0, 0.0), axis=-1, keepdims=True)
        den1 = jnp.sum(jnp.where(lane == 0, acc1, 0.0), axis=-1, keepdims=True)
        o_ref[:, pair * HEAD_PAD:(pair + 1) * HEAD_PAD] = jnp.where(
            lane < MLA_V, acc0 / den0, acc1 / den1).astype(o_ref.dtype)


def _attn(q, k, v, bsz, t, tq):
    m = q.shape[0]
    nq = t // tq
    pw = ATTN_HEADS * HEAD_PAD
    return pl.pallas_call(
        functools.partial(_attn_body, tq=tq),
        grid=(bsz, MLA_HEADS // ATTN_HEADS, nq),
        in_specs=[pl.BlockSpec((tq, pw), lambda b, h, i: (b * nq + i, h)),
                  pl.BlockSpec((t, pw), lambda b, h, i: (b, h)),
                  pl.BlockSpec((t, pw), lambda b, h, i: (b, h))],
        out_specs=pl.BlockSpec((tq, ATTN_HEADS * MLA_V), lambda b, h, i: (b * nq + i, h)),
        out_shape=jax.ShapeDtypeStruct((m, MLA_WIDTH), BF16),
        compiler_params=_cparams(("arbitrary", "arbitrary", "arbitrary")),
        name="mla_attn",
    )(q, k, v)


def _softplus(x):
    return jnp.maximum(x, 0.0) + jnp.log1p(jnp.exp(-jnp.abs(x)))


def _ssd_body(z_ref, dt_ref, xbc_ref, cw_ref, cb_ref, dtb_ref, alog_ref, dsk_ref, ng_ref, tril_ref, e8_ref,
              o_ref, xp_ref, xc_ref, s_ref, *, tq):
    L = SSD_CHUNK
    gn = SSM_GROUPS * SSM_STATE
    half = SSM_WIDTH // SSM_GROUPS
    t_idx = pl.program_id(1)

    @pl.when(t_idx == 0)
    def _():
        s_ref[...] = jnp.zeros_like(s_ref)
        xp_ref[0:8, :] = jnp.zeros((8, SSM_CONV_DIM), F32)

    @pl.when(t_idx > 0)
    def _():
        xp_ref[0:8, :] = xp_ref[tq:tq + 8, :]

    xp_ref[8:8 + tq, :] = xbc_ref[...].astype(F32)
    cw = cw_ref[...]
    acc = cb_ref[...] + cw[3:4, :] * xp_ref[8:8 + tq, :]
    for w in range(SSM_CONV - 1):
        acc = acc + cw[w:w + 1, :] * xp_ref[5 + w:5 + w + tq, :]
    xc_ref[...] = acc * _sigmoid(acc)

    a_neg = -jnp.exp(alog_ref[...])
    dtb = dtb_ref[...]
    dsk = dsk_ref[...]
    ng = ng_ref[...]
    tril = tril_ref[...]
    e8 = e8_ref[...]
    ri = lax.broadcasted_iota(jnp.int32, (L, L), 0)
    ci = lax.broadcasted_iota(jnp.int32, (L, L), 1)
    tri = ci <= ri
    lane_gn = lax.broadcasted_iota(jnp.int32, (1, gn), 1)
    lane_w = lax.broadcasted_iota(jnp.int32, (1, SSM_WIDTH), 1)
    row_gn = lax.broadcasted_iota(jnp.int32, (gn, SSM_WIDTH), 0)
    col_w = lax.broadcasted_iota(jnp.int32, (gn, SSM_WIDTH), 1)
    blockdiag = (row_gn // SSM_STATE) == (col_w // half)
    head_mask = [(lane_w // SSM_HEAD_DIM) == h for h in range(SSM_HEADS)]

    def chunk(c, carry):
        rows = pl.ds(pl.multiple_of(c * L, L), L)
        xc = xc_ref[rows, :]
        xs = xc[:, :SSM_WIDTH]
        bm = xc[:, SSM_WIDTH:SSM_WIDTH + gn]
        cm = xc[:, SSM_WIDTH + gn:]
        dt = _softplus(dt_ref[rows, :].astype(F32) + dtb)
        a = dt * a_neg
        acum = _sel_rows(tril, a)
        dt_e = _sel_cols(dt, e8)
        ac_e = _sel_cols(acum, e8)
        x_dt = xs * dt_e
        last = ac_e[L - 1:L, :]
        x_dec = (x_dt * jnp.exp(last - ac_e)).astype(BF16)
        x_b = x_dt.astype(BF16)
        bm_t = bm.T.astype(BF16)
        cb0 = _dot(jnp.where(lane_gn < SSM_STATE, cm, 0.0).astype(BF16), bm_t)
        cb1 = _dot(jnp.where(lane_gn >= SSM_STATE, cm, 0.0).astype(BF16), bm_t)
        ac_t = acum.T
        ws, xblk = [], []
        for h in range(SSM_HEADS):
            hl = DT_LANE + h
            seg = acum[:, hl:hl + 1] - ac_t[hl:hl + 1, :]
            lmat = jnp.exp(jnp.where(tri, seg, -jnp.inf))
            ws.append((lmat * (cb0 if h < SSM_HEADS // SSM_GROUPS else cb1)).astype(BF16))
            xblk.append(jnp.where(head_mask[h], x_b, jnp.zeros_like(x_b)))
        y = _dot(jnp.concatenate(ws, axis=1), jnp.concatenate(xblk, axis=0))
        st = s_ref[...]
        y = y + jnp.exp(ac_e) * _dot(cm.astype(BF16), st.astype(BF16))
        s_ref[...] = jnp.exp(last) * st + jnp.where(blockdiag, _dot(bm_t, x_dec), 0.0)
        y = y + xs * dsk
        z = z_ref[rows, :].astype(F32)
        y = y * (z * _sigmoid(z))
        y0 = _rms(y[:, :half], ng[:, :half])
        y1 = _rms(y[:, half:], ng[:, half:])
        o_ref[rows, :half] = y0.astype(o_ref.dtype)
        o_ref[rows, half:] = y1.astype(o_ref.dtype)
        return carry

    lax.fori_loop(0, tq // L, chunk, 0, unroll=4)


def _ssd(proj, cw, cb, dtb, alog, dsk, ng, bsz, t, tq):
    m = proj.shape[0]
    nq = t // tq
    L = SSD_CHUNK
    tril = jnp.asarray(np.tile(np.tril(np.ones((L, L), np.float32)), (1, N_SPLIT)), BF16)
    e8 = np.zeros((LANES, SSM_WIDTH), np.float32)
    for h in range(SSM_HEADS):
        e8[DT_LANE + h, h * SSM_HEAD_DIM:(h + 1) * SSM_HEAD_DIM] = 1.0
    e8 = jnp.asarray(np.tile(e8, (N_SPLIT, 1)), BF16)
    full = lambda a: pl.BlockSpec(a.shape, lambda b, i: (0, 0))
    return pl.pallas_call(
        functools.partial(_ssd_body, tq=tq),
        grid=(bsz, nq),
        in_specs=[pl.BlockSpec((tq, SSM_WIDTH), lambda b, i: (b * nq + i, OFF_SZ // SSM_WIDTH)),
                  pl.BlockSpec((tq, LANES), lambda b, i: (b * nq + i, OFF_MISC // LANES)),
                  pl.BlockSpec((tq, SSM_CONV_DIM), lambda b, i: (b * nq + i, OFF_SXBC // SSM_CONV_DIM)),
                  full(cw), full(cb), full(dtb), full(alog), full(dsk), full(ng), full(tril), full(e8)],
        out_specs=pl.BlockSpec((tq, SSM_WIDTH), lambda b, i: (b * nq + i, 0)),
        out_shape=jax.ShapeDtypeStruct((m, SSM_WIDTH), BF16),
        scratch_shapes=[pltpu.VMEM((tq + 8, SSM_CONV_DIM), F32),
                        pltpu.VMEM((tq, SSM_CONV_DIM), F32),
                        pltpu.VMEM((SSM_GROUPS * SSM_STATE, SSM_WIDTH), F32)],
        compiler_params=_cparams(("arbitrary", "arbitrary")),
        name="ssd",
    )(proj, proj, proj, cw, cb, dtb, alog, dsk, ng, tril, e8)


def _merge_body(ya_ref, yb_ref, yc_ref, gate_ref, x_ref, wa_ref, wb_ref, wc_ref, wo_ref, o_ref):
    d = D_MODEL
    gate = lambda b: _sigmoid(gate_ref[:, b * d:(b + 1) * d].astype(F32))
    merged = (gate(0) * _dot(ya_ref[...], wa_ref[...])
              + gate(1) * _dot(yb_ref[...], wb_ref[...])
              + gate(2) * _dot(yc_ref[...], wc_ref[...]))
    o_ref[...] = x_ref[...] + _dot(merged.astype(BF16), wo_ref[...])


def _merge(ya, yb, yc, proj, x2, wa, wb, wc, wo, tm):
    m, d = x2.shape
    full = lambda a: pl.BlockSpec(a.shape, lambda i: (0, 0))
    row = lambda w: pl.BlockSpec((tm, w), lambda i: (i, 0))
    return pl.pallas_call(
        _merge_body,
        grid=(m // tm,),
        in_specs=[row(HG_WIDTH), row(MLA_WIDTH), row(SSM_WIDTH), row(N_BRANCH * d), row(d),
                  full(wa), full(wb), full(wc), full(wo)],
        out_specs=row(d),
        out_shape=jax.ShapeDtypeStruct((m, d), F32),
        compiler_params=_cparams(("arbitrary",)),
        name="merge_out",
    )(ya, yb, yc, proj, x2, wa, wb, wc, wo)


def _mlp_body(x_ref, g_ref, wu_ref, wd_ref, fg_ref, o_ref, *, last_layer):
    x = x_ref[...]
    h = _rms(x, g_ref[...]).astype(BF16)
    u = jnp.maximum(_dot(h, wu_ref[...]), 0.0)
    y = x + _dot((u * u).astype(BF16), wd_ref[...])
    o_ref[...] = _rms(y, fg_ref[...]) if last_layer else y


def _mlp(x2, g, wu, wd, final_g, tm, last_layer):
    m, d = x2.shape
    const = lambda a: pl.BlockSpec(a.shape, lambda i: (0, 0), pipeline_mode=pl.Buffered(1))
    return pl.pallas_call(
        functools.partial(_mlp_body, last_layer=last_layer),
        grid=(m // tm,),
        in_specs=[pl.BlockSpec((tm, d), lambda i: (i, 0)), const(g), const(wu), const(wd), const(final_g)],
        out_specs=pl.BlockSpec((tm, d), lambda i: (i, 0)),
        out_shape=jax.ShapeDtypeStruct((m, d), F32),
        compiler_params=_cparams(("arbitrary",)),
        name="mlp",
    )(x2, g, wu, wd, final_g)


def _rot_cols(w):
    hr = MLA_ROPE // 2
    return jnp.concatenate([-w[..., hr:], w[..., :hr]], axis=-1)


def _pack_w_in(w_in):
    dep, d, _ = w_in.shape
    sizes = (512, 512, 512, 512, MLA_Q_LORA, MLA_KV_LORA, MLA_ROPE, SSM_WIDTH, SSM_CONV_DIM, SSM_HEADS,
             N_BRANCH * D_MODEL)
    pts = np.cumsum(sizes)[:-1].tolist()
    hq, hf, hi, hg, cq, ckv, kr, sz, sxbc, sdt, gate = jnp.split(w_in.astype(BF16), pts, axis=-1)
    z = lambda n: jnp.zeros((dep, d, n), BF16)
    misc = jnp.concatenate([_rot_cols(kr), sdt, z(32 - SSM_HEADS), kr, z(32)], axis=-1)
    packed = jnp.concatenate([gate, hq, hf, hi, hg, sz, misc, cq, sxbc, ckv], axis=-1)
    assert packed.shape[-1] == PROJ_COLS
    return packed.astype(BF16)


def _pack_w_uq(w):
    dep, r, _ = w.shape
    wh = w.reshape(dep, r, MLA_HEADS, MLA_NOPE + MLA_ROPE)
    nope, rope = wh[..., :MLA_NOPE], wh[..., MLA_NOPE:]
    z = lambda n: jnp.zeros((dep, r, MLA_HEADS, n), w.dtype)
    a = jnp.concatenate([nope, rope, z(32)], axis=-1).reshape(dep, r, MLA_HEADS * HEAD_PAD)
    b = jnp.concatenate([z(64), _rot_cols(rope), z(32)], axis=-1).reshape(dep, r, MLA_HEADS * HEAD_PAD)
    return jnp.concatenate([a, b], axis=-1).astype(BF16)


def _pack_w_ukv(w):
    dep, r, _ = w.shape
    wh = w.reshape(dep, r, MLA_HEADS // 2, 2, MLA_NOPE + MLA_V)
    kn, v = wh[..., :MLA_NOPE], wh[..., MLA_NOPE:]
    z = jnp.zeros_like(kn)
    ka = jnp.concatenate([kn, z], axis=-1).reshape(dep, r, MLA_HEADS * HEAD_PAD)
    va = jnp.stack([jnp.concatenate([v[..., 0, :], z[..., 0, :]], axis=-1),
                    jnp.concatenate([z[..., 1, :], v[..., 1, :]], axis=-1)], axis=-2)
    va = va.reshape(dep, r, MLA_HEADS * HEAD_PAD)
    return jnp.concatenate([ka, va], axis=-1).astype(BF16)


def _v_ones():
    v = np.zeros((1, MLA_HEADS * HEAD_PAD), np.float32)
    for h in range(MLA_HEADS):
        v[0, h * HEAD_PAD + (MLA_V if h % 2 == 0 else 0)] = 1.0
    return jnp.asarray(v)


def _rope_tables(positions):
    inv = ROPE_THETA ** (-jnp.arange(0, MLA_ROPE, 2, dtype=F32) / MLA_ROPE)
    ang = positions.astype(F32).reshape(-1, 1) * inv
    z = lambda n: jnp.zeros((ang.shape[0], n), F32)
    cos, sin = jnp.cos(ang), jnp.sin(ang)
    return (jnp.concatenate([z(64), cos, cos, z(32)], axis=-1),
            jnp.concatenate([z(64), sin, sin, z(32)], axis=-1))


def _row_tile(n, want):
    t = min(n, want)
    assert n % t == 0, (n, t)
    return t


def kernel(x, positions, mix_norm, w_in, hg_lb_logits, hg_norm, mla_q_norm, mla_kv_norm, mla_w_uq, mla_w_ukv,
           ssm_conv_w, ssm_conv_b, ssm_dt_bias, ssm_a_log, ssm_d, ssm_norm, w_br_hg, w_br_mla, w_br_ssm,
           w_out, mlp_norm, w_up, w_down, final_norm):
    bsz, t, d = x.shape
    depth = w_in.shape[0]
    assert d == D_MODEL and t % SSD_CHUNK == 0
    m = bsz * t
    tm = _row_tile(m, 512)
    tm_in = _row_tile(m, 1024)
    tq = _row_tile(t, 512)

    lbs = jnp.cumsum(jax.nn.softmax(hg_lb_logits.astype(F32), axis=0), axis=0)
    lbs = lbs - lbs[0:1]
    w_in_p = _pack_w_in(w_in)
    wq_p = _pack_w_uq(mla_w_uq)
    wkv_p = _pack_w_ukv(mla_w_ukv)
    vone = _v_ones()
    cos_t, sin_t = _rope_tables(positions)
    pad_h = lambda a: jnp.pad(a.astype(F32), ((0, 0), (DT_LANE, LANES - DT_LANE - SSM_HEADS)))
    dtb_p, alog_p = pad_h(ssm_dt_bias), pad_h(ssm_a_log)
    dsk_e = jnp.repeat(ssm_d.astype(F32), SSM_HEAD_DIM, axis=-1)
    bf = lambda a: a.astype(BF16)
    wa, wb, wc, wo, wu, wd = bf(w_br_hg), bf(w_br_mla), bf(w_br_ssm), bf(w_out), bf(w_up), bf(w_down)
    r1 = lambda a: a.reshape(1, -1).astype(F32)

    x2 = x.reshape(m, d)
    for l in range(depth):
        proj = _inproj(x2, r1(mix_norm[l]), w_in_p[l], tm_in, PROJ_COLS // 2)
        ya = _hgrn(proj, r1(lbs[l]), r1(hg_norm[l]), bsz, t, tq)
        q, k, v = _mla_prep(proj, cos_t, sin_t, r1(mla_q_norm[l]), r1(mla_kv_norm[l]), wq_p[l], wkv_p[l],
                            vone, tm)
        yb = _attn(q, k, v, bsz, t, tq)
        yc = _ssd(proj, ssm_conv_w[l].astype(F32), r1(ssm_conv_b[l]), r1(dtb_p[l]), r1(alog_p[l]),
                  r1(dsk_e[l]), r1(ssm_norm[l]), bsz, t, tq)
        x2 = _merge(ya, yb, yc, proj, x2, wa[l], wb[l], wc[l], wo[l], tm)
        x2 = _mlp(x2, r1(mlp_norm[l]), wu[l], wd[l], r1(final_norm), tm, last_layer=(l == depth - 1))
    return x2.reshape(bsz, t, d)
```

```python
import functools
import math

import numpy as np
import jax
import jax.numpy as jnp
from jax import lax
from jax.experimental import pallas as pl
from jax.experimental.pallas import tpu as pltpu

F32 = jnp.float32
BF16 = jnp.bfloat16

D_MODEL = 1024
EPS = 1e-6
HG_HEADS = 4
HG_DK = 128
HG_DV = 128
HG_WIDTH = HG_HEADS * HG_DV
MLA_HEADS = 8
MLA_NOPE = 64
MLA_ROPE = 32
MLA_V = 64
MLA_Q_LORA = 384
MLA_KV_LORA = 256
MLA_WIDTH = MLA_HEADS * MLA_V
ROPE_THETA = 10000.0
SSM_HEADS = 8
SSM_HEAD_DIM = 64
SSM_WIDTH = SSM_HEADS * SSM_HEAD_DIM
SSM_GROUPS = 2
SSM_STATE = 64
SSM_CONV = 4
SSM_CONV_DIM = SSM_WIDTH + 2 * SSM_GROUPS * SSM_STATE
D_FF = 4 * D_MODEL
N_BRANCH = 3

LANES = 128
HEAD_PAD = 128

OFF_HQ, OFF_HF, OFF_HI, OFF_HG = 0, 512, 1024, 1536
OFF_SZ = 2048
OFF_MISC = 2560
DT_LANE = 32
OFF_CQ = 2688
OFF_SXBC = 3072
OFF_CKV = 3840
PROJ_COLS = 4096
MIXER_IN_COLS = 4008

HG_CHUNK = 64
HG_LEVELS = 6
SSD_CHUNK = 128
ATTN_HEADS = 8

VMEM_LIMIT = 48 * 1024 * 1024


def _cparams(sem, flags=None):
    return pltpu.CompilerParams(dimension_semantics=sem, vmem_limit_bytes=VMEM_LIMIT, flags=flags)


def _sigmoid(x):
    return jax.nn.sigmoid(x)


def _split3(x):
    hi = x.astype(BF16)
    r1 = x - hi.astype(F32)
    mid = r1.astype(BF16)
    r2 = r1 - mid.astype(F32)
    return hi, mid, r2.astype(BF16)


def _dot(a, b):
    return jnp.dot(a, b, preferred_element_type=F32)


def _dot_nt(a, b):
    return lax.dot_general(a, b, (((1,), (1,)), ((), ())), preferred_element_type=F32)


N_SPLIT = 3


def _sel_rows(sel3, x):
    return _dot(sel3, jnp.concatenate(_split3(x), axis=0))


def _sel_cols(x, sel3):
    return _dot(jnp.concatenate(_split3(x), axis=1), sel3)


def _rms(x, g):
    ms = jnp.mean(x * x, axis=-1, keepdims=True)
    return x * lax.rsqrt(ms + EPS) * g


def _inproj_body(x_ref, g_ref, w_ref, o_ref):
    h = _rms(x_ref[...], g_ref[...]).astype(BF16)
    o_ref[...] = _dot(h, w_ref[...]).astype(o_ref.dtype)


def _inproj(x2, g, w, tm, tn):
    m, d = x2.shape
    n = w.shape[1]
    return pl.pallas_call(
        _inproj_body,
        grid=(n // tn, m // tm),
        in_specs=[pl.BlockSpec((tm, d), lambda j, i: (i, 0)),
                  pl.BlockSpec((1, d), lambda j, i: (0, 0)),
                  pl.BlockSpec((d, tn), lambda j, i: (0, j))],
        out_specs=pl.BlockSpec((tm, tn), lambda j, i: (i, j)),
        out_shape=jax.ShapeDtypeStruct((m, n), BF16),
        compiler_params=_cparams(("arbitrary", "arbitrary")),
        name="inproj",
    )(x2, g, w)


def _hgrn_level_matrix():
    c = HG_CHUNK
    mat = np.zeros((8, c, c), np.float32)
    for i in range(c):
        mat[0, i, : i + 1] = 1.0
        mat[7, i, i + 1:] = 1.0
    for lvl in range(HG_LEVELS):
        m = 1 << lvl
        for r in range(c):
            start = (r // (2 * m)) * 2 * m
            mid = start + m
            if r >= mid:
                mat[lvl + 1, r, mid: r + 1] = 1.0
            else:
                mat[lvl + 1, r, r + 1: mid] = 1.0
    return mat.reshape(8 * c, c)


def _hgrn_pair_masks():
    n = 2 * HG_CHUNK
    row, col = np.meshgrid(np.arange(n), np.arange(n), indexing="ij")
    out = [row == col]
    for lvl in range(HG_LEVELS):
        same_block = (row >> (lvl + 1)) == (col >> (lvl + 1))
        out.append(same_block & (((row >> lvl) & 1) == 1) & (((col >> lvl) & 1) == 0))
    return np.stack(out).astype(np.float32)


def _hgrn_body(q_ref, f_ref, i_ref, g_ref, lb_ref, ng_ref, m_ref, mask_ref, o_ref, s_ref, *, n_chunks):
    c = HG_CHUNK

    @pl.when(pl.program_id(1) == 0)
    def _():
        s_ref[...] = jnp.zeros_like(s_ref)

    lb = lb_ref[...]
    log_lb = jnp.log(lb)
    log1m_lb = jnp.log1p(-lb)
    one_m_lb = 1.0 - lb
    ng = ng_ref[...]
    sel = m_ref[...]

    rowv = lax.broadcasted_iota(jnp.int32, (2 * c, HG_DK), 0)
    second_half = [((rowv >> lvl) & 1) == 1 for lvl in range(HG_LEVELS)]

    def chunk(ci, carry):
        r0 = pl.multiple_of(ci * c, c)
        rows = pl.ds(r0, c)
        hq = q_ref[rows, :].astype(F32)
        ff = f_ref[rows, :].astype(F32)
        hv = i_ref[rows, :].astype(F32)
        hg = g_ref[rows, :].astype(F32)
        q = hq * _sigmoid(hq)
        e_f = jnp.exp(-jnp.abs(ff))
        one_p = 1.0 + e_f
        log_sig = jnp.minimum(ff, 0.0) - jnp.log(one_p)
        b = log1m_lb + log_sig
        lf = jnp.maximum(log_lb, b) + jnp.log(1.0 + jnp.exp(-jnp.abs(log_lb - b)))
        k = one_m_lb * jnp.where(ff >= 0.0, e_f, 1.0) / one_p
        gate = hg * _sigmoid(hg)
        ex = _sel_rows(sel, lf)

        for pair in range(HG_HEADS // 2):
            sls = [slice(h * HG_DK, (h + 1) * HG_DK) for h in (2 * pair, 2 * pair + 1)]
            stack = lambda a, lo=0, hi=c: jnp.concatenate([a[lo:hi, sl] for sl in sls], axis=0)
            level = lambda n: stack(ex, n * c, (n + 1) * c)
            qp, kp, vp = stack(q).astype(BF16), stack(k).astype(BF16), stack(hv)
            decay_of = lambda n: jnp.exp(level(n).astype(BF16))
            att = mask_ref[0] * _dot_nt(qp, kp)
            for lvl in range(HG_LEVELS):
                xl = jnp.where(second_half[lvl], qp, kp) * decay_of(lvl + 1)
                att = att + mask_ref[lvl + 1] * _dot_nt(xl, xl)
            o_intra = _dot(att.astype(BF16), vp.astype(BF16))
            e0 = level(0)
            q_in = qp * jnp.exp(e0.astype(BF16))
            k_out = kp * decay_of(7)
            for idx, sl in enumerate(sls):
                h = 2 * pair + idx
                r = slice(idx * c, (idx + 1) * c)
                st = s_ref[h]
                o = o_intra[r] + _dot_nt(q_in[r], st.astype(BF16))
                decay = jnp.exp(e0[(idx + 1) * c - 1:(idx + 1) * c, :])
                s_ref[h] = st * decay + _dot(vp[r].T.astype(BF16), k_out[r])
                y = _rms(o, ng[:, sl]) * gate[:, sl]
                o_ref[rows, sl] = y.astype(o_ref.dtype)
        return carry

    lax.fori_loop(0, n_chunks, chunk, 0, unroll=8)


def _hgrn(proj, lb, ng, bsz, t, tq):
    m = proj.shape[0]
    nq = t // tq
    sel = jnp.asarray(np.tile(_hgrn_level_matrix(), (1, N_SPLIT)), BF16)
    masks = jnp.asarray(_hgrn_pair_masks())
    w = HG_WIDTH

    def col_spec(off):
        blk = off // w
        return pl.BlockSpec((tq, w), lambda b, i: (b * nq + i, blk))

    vec = pl.BlockSpec((1, w), lambda b, i: (0, 0))
    return pl.pallas_call(
        functools.partial(_hgrn_body, n_chunks=tq // HG_CHUNK),
        grid=(bsz, nq),
        in_specs=[col_spec(OFF_HQ), col_spec(OFF_HF), col_spec(OFF_HI), col_spec(OFF_HG), vec, vec,
                  pl.BlockSpec(sel.shape, lambda b, i: (0, 0)),
                  pl.BlockSpec(masks.shape, lambda b, i: (0, 0, 0))],
        out_specs=pl.BlockSpec((tq, w), lambda b, i: (b * nq + i, 0)),
        out_shape=jax.ShapeDtypeStruct((m, w), BF16),
        scratch_shapes=[pltpu.VMEM((HG_HEADS, HG_DV, HG_DK), F32)],
        compiler_params=_cparams(("arbitrary", "arbitrary")),
        name="hgrn2",
    )(proj, proj, proj, proj, lb, ng, sel, masks)


def _mla_prep_body(cq_ref, ckv_ref, kr_ref, cos_ref, sin_ref, qn_ref, kvn_ref, wq_ref, wkv_ref, vone_ref,
                   q_out, k_out, v_out):
    scale = (MLA_NOPE + MLA_ROPE) ** -0.5 * math.log2(math.e)
    cqn = _rms(cq_ref[...].astype(F32), qn_ref[...]).astype(BF16)
    ckvn = _rms(ckv_ref[...].astype(F32), kvn_ref[...]).astype(BF16)
    qq = _dot(cqn, wq_ref[...])
    kv = _dot(ckvn, wkv_ref[...])
    cos = cos_ref[...]
    sin = sin_ref[...]
    lane = lax.broadcasted_iota(jnp.int32, (1, HEAD_PAD), 1)
    cq_t = scale * (cos + (lane < MLA_NOPE).astype(F32))
    sq_t = scale * sin
    krb = kr_ref[...].astype(F32)
    k_rope = krb * cos + pltpu.roll(krb, 64, 1) * sin
    hw = MLA_HEADS * HEAD_PAD
    vone = vone_ref[...]
    for h in range(MLA_HEADS):
        sl = slice(h * HEAD_PAD, (h + 1) * HEAD_PAD)
        sl2 = slice(hw + h * HEAD_PAD, hw + (h + 1) * HEAD_PAD)
        q_out[:, sl] = (qq[:, sl] * cq_t + qq[:, sl2] * sq_t).astype(BF16)
        k_out[:, sl] = (kv[:, sl] + k_rope).astype(BF16)
        v_out[:, sl] = (kv[:, sl2] + vone[:, sl]).astype(BF16)


def _mla_prep(proj, cos_t, sin_t, qn, kvn, wq, wkv, vone, tm):
    m = proj.shape[0]
    hw = MLA_HEADS * HEAD_PAD
    full = lambda a: pl.BlockSpec(a.shape, lambda i: (0, 0))
    out = jax.ShapeDtypeStruct((m, hw), BF16)
    ospec = pl.BlockSpec((tm, hw), lambda i: (i, 0))
    return pl.pallas_call(
        _mla_prep_body,
        grid=(m // tm,),
        in_specs=[pl.BlockSpec((tm, MLA_Q_LORA), lambda i: (i, OFF_CQ // MLA_Q_LORA)),
                  pl.BlockSpec((tm, MLA_KV_LORA), lambda i: (i, OFF_CKV // MLA_KV_LORA)),
                  pl.BlockSpec((tm, LANES), lambda i: (i, OFF_MISC // LANES)),
                  pl.BlockSpec((tm, LANES), lambda i: (i, 0)),
                  pl.BlockSpec((tm, LANES), lambda i: (i, 0)),
                  full(qn), full(kvn), full(wq), full(wkv), full(vone)],
        out_specs=[ospec, ospec, ospec],
        out_shape=[out, out, out],
        compiler_params=_cparams(("arbitrary",)),
        name="mla_prep",
    )(proj, proj, proj, cos_t, sin_t, qn, kvn, wq, wkv, vone)


def _attn_body(q_ref, k_ref, v_ref, o_ref, *, tq):
    i = pl.program_id(2)
    lane = lax.broadcasted_iota(jnp.int32, (1, HEAD_PAD), 1)
    qpos = lax.broadcasted_iota(jnp.int32, (tq, tq), 0)
    kpos = lax.broadcasted_iota(jnp.int32, (tq, tq), 1)
    causal = kpos <= qpos
    heads = range(ATTN_HEADS)
    slices = [slice(h * HEAD_PAD, (h + 1) * HEAD_PAD) for h in heads]
    block = lambda j: pl.ds(pl.multiple_of(j * tq, tq), tq)

    def scores(h, j, masked, m_prev):
        s = _dot_nt(q_ref[:, slices[h]], k_ref[block(j), slices[h]])
        if masked:
            s = jnp.where(causal, s, -jnp.inf)
        m_new = jnp.maximum(m_prev, jnp.max(s, axis=-1, keepdims=True))
        return m_new, jnp.exp2(m_prev - m_new), jnp.exp2((s - m_new).astype(BF16))

    def step(j, state, masked):
        new = []
        for h in heads:
            m, acc = state[h]
            m, alpha, p = scores(h, j, masked, m)
            new.append((m, alpha * acc + _dot(p, v_ref[block(j), slices[h]])))
        return tuple(new)

    init = (jnp.full((tq, 1), -jnp.inf, F32), jnp.zeros((tq, HEAD_PAD), F32))
    state = lax.fori_loop(0, i, functools.partial(step, masked=False), (init,) * ATTN_HEADS)
    state = step(i, state, True)
    for pair in range(ATTN_HEADS // 2):
        acc0, acc1 = state[2 * pair][1], state[2 * pair + 1][1]
        den0 = jnp.sum(jnp.where(lane == MLA_V, acc0, 0.0), axis=-1, keepdims=True)
        den1 = jnp.sum(jnp.where(lane == 0, acc1, 0.0), axis=-1, keepdims=True)
        o_ref[:, pair * HEAD_PAD:(pair + 1) * HEAD_PAD] = jnp.where(
            lane < MLA_V, acc0 / den0, acc1 / den1).astype(o_ref.dtype)


def _attn(q, k, v, bsz, t, tq):
    m = q.shape[0]
    nq = t // tq
    pw = ATTN_HEADS * HEAD_PAD
    return pl.pallas_call(
        functools.partial(_attn_body, tq=tq),
        grid=(bsz, MLA_HEADS // ATTN_HEADS, nq),
        in_specs=[pl.BlockSpec((tq, pw), lambda b, h, i: (b * nq + i, h)),
                  pl.BlockSpec((t, pw), lambda b, h, i: (b, h)),
                  pl.BlockSpec((t, pw), lambda b, h, i: (b, h))],
        out_specs=pl.BlockSpec((tq, ATTN_HEADS * MLA_V), lambda b, h, i: (b * nq + i, h)),
        out_shape=jax.ShapeDtypeStruct((m, MLA_WIDTH), BF16),
        compiler_params=_cparams(("arbitrary", "arbitrary", "arbitrary")),
        name="mla_attn",
    )(q, k, v)


def _softplus(x):
    return jnp.maximum(x, 0.0) + jnp.log1p(jnp.exp(-jnp.abs(x)))


def _ssd_body(z_ref, dt_ref, xbc_ref, cw_ref, cb_ref, dtb_ref, alog_ref, dsk_ref, ng_ref, tril_ref, e8_ref,
              o_ref, xp_ref, xc_ref, s_ref, *, tq):
    L = SSD_CHUNK
    gn = SSM_GROUPS * SSM_STATE
    half = SSM_WIDTH // SSM_GROUPS
    t_idx = pl.program_id(1)

    @pl.when(t_idx == 0)
    def _():
        s_ref[...] = jnp.zeros_like(s_ref)
        xp_ref[0:8, :] = jnp.zeros((8, SSM_CONV_DIM), F32)

    @pl.when(t_idx > 0)
    def _():
        xp_ref[0:8, :] = xp_ref[tq:tq + 8, :]

    xp_ref[8:8 + tq, :] = xbc_ref[...].astype(F32)
    cw = cw_ref[...]
    acc = cb_ref[...] + cw[3:4, :] * xp_ref[8:8 + tq, :]
    for w in range(SSM_CONV - 1):
        acc = acc + cw[w:w + 1, :] * xp_ref[5 + w:5 + w + tq, :]
    xc_ref[...] = acc * _sigmoid(acc)

    a_neg = -jnp.exp(alog_ref[...])
    dtb = dtb_ref[...]
    dsk = dsk_ref[...]
    ng = ng_ref[...]
    tril = tril_ref[...]
    e8 = e8_ref[...]
    ri = lax.broadcasted_iota(jnp.int32, (L, L), 0)
    ci = lax.broadcasted_iota(jnp.int32, (L, L), 1)
    tri = ci <= ri
    lane_gn = lax.broadcasted_iota(jnp.int32, (1, gn), 1)
    lane_w = lax.broadcasted_iota(jnp.int32, (1, SSM_WIDTH), 1)
    row_gn = lax.broadcasted_iota(jnp.int32, (gn, SSM_WIDTH), 0)
    col_w = lax.broadcasted_iota(jnp.int32, (gn, SSM_WIDTH), 1)
    blockdiag = (row_gn // SSM_STATE) == (col_w // half)
    head_mask = [(lane_w // SSM_HEAD_DIM) == h for h in range(SSM_HEADS)]

    def chunk(c, carry):
        rows = pl.ds(pl.multiple_of(c * L, L), L)
        xc = xc_ref[rows, :]
        xs = xc[:, :SSM_WIDTH]
        bm = xc[:, SSM_WIDTH:SSM_WIDTH + gn]
        cm = xc[:, SSM_WIDTH + gn:]
        dt = _softplus(dt_ref[rows, :].astype(F32) + dtb)
        a = dt * a_neg
        acum = _sel_rows(tril, a)
        dt_e = _sel_cols(dt, e8)
        ac_e = _sel_cols(acum, e8)
        x_dt = xs * dt_e
        last = ac_e[L - 1:L, :]
        x_dec = (x_dt * jnp.exp(last - ac_e)).astype(BF16)
        x_b = x_dt.astype(BF16)
        bm_t = bm.T.astype(BF16)
        cb0 = _dot(jnp.where(lane_gn < SSM_STATE, cm, 0.0).astype(BF16), bm_t)
        cb1 = _dot(jnp.where(lane_gn >= SSM_STATE, cm, 0.0).astype(BF16), bm_t)
        ac_t = acum.T
        ws, xblk = [], []
        for h in range(SSM_HEADS):
            hl = DT_LANE + h
            seg = acum[:, hl:hl + 1] - ac_t[hl:hl + 1, :]
            lmat = jnp.exp(jnp.where(tri, seg, -jnp.inf))
            ws.append((lmat * (cb0 if h < SSM_HEADS // SSM_GROUPS else cb1)).astype(BF16))
            xblk.append(jnp.where(head_mask[h], x_b, jnp.zeros_like(x_b)))
        y = _dot(jnp.concatenate(ws, axis=1), jnp.concatenate(xblk, axis=0))
        st = s_ref[...]
        y = y + jnp.exp(ac_e) * _dot(cm.astype(BF16), st.astype(BF16))
        s_ref[...] = jnp.exp(last) * st + jnp.where(blockdiag, _dot(bm_t, x_dec), 0.0)
        y = y + xs * dsk
        z = z_ref[rows, :].astype(F32)
        y = y * (z * _sigmoid(z))
        y0 = _rms(y[:, :half], ng[:, :half])
        y1 = _rms(y[:, half:], ng[:, half:])
        o_ref[rows, :half] = y0.astype(o_ref.dtype)
        o_ref[rows, half:] = y1.astype(o_ref.dtype)
        return carry

    lax.fori_loop(0, tq // L, chunk, 0, unroll=4)


def _ssd(proj, cw, cb, dtb, alog, dsk, ng, bsz, t, tq):
    m = proj.shape[0]
    nq = t // tq
    L = SSD_CHUNK
    tril = jnp.asarray(np.tile(np.tril(np.ones((L, L), np.float32)), (1, N_SPLIT)), BF16)
    e8 = np.zeros((LANES, SSM_WIDTH), np.float32)
    for h in range(SSM_HEADS):
        e8[DT_LANE + h, h * SSM_HEAD_DIM:(h + 1) * SSM_HEAD_DIM] = 1.0
    e8 = jnp.asarray(np.tile(e8, (N_SPLIT, 1)), BF16)
    full = lambda a: pl.BlockSpec(a.shape, lambda b, i: (0, 0))
    return pl.pallas_call(
        functools.partial(_ssd_body, tq=tq),
        grid=(bsz, nq),
        in_specs=[pl.BlockSpec((tq, SSM_WIDTH), lambda b, i: (b * nq + i, OFF_SZ // SSM_WIDTH)),
                  pl.BlockSpec((tq, LANES), lambda b, i: (b * nq + i, OFF_MISC // LANES)),
                  pl.BlockSpec((tq, SSM_CONV_DIM), lambda b, i: (b * nq + i, OFF_SXBC // SSM_CONV_DIM)),
                  full(cw), full(cb), full(dtb), full(alog), full(dsk), full(ng), full(tril), full(e8)],
        out_specs=pl.BlockSpec((tq, SSM_WIDTH), lambda b, i: (b * nq + i, 0)),
        out_shape=jax.ShapeDtypeStruct((m, SSM_WIDTH), BF16),
        scratch_shapes=[pltpu.VMEM((tq + 8, SSM_CONV_DIM), F32),
                        pltpu.VMEM((tq, SSM_CONV_DIM), F32),
                        pltpu.VMEM((SSM_GROUPS * SSM_STATE, SSM_WIDTH), F32)],
        compiler_params=_cparams(("arbitrary", "arbitrary")),
        name="ssd",
    )(proj, proj, proj, cw, cb, dtb, alog, dsk, ng, tril, e8)


def _merge_body(ya_ref, yb_ref, yc_ref, x_ref, g_ref, wg_ref, wa_ref, wb_ref, wc_ref, wo_ref, o_ref):
    d = D_MODEL
    x = x_ref[...]
    h = _rms(x, g_ref[...]).astype(BF16)
    gate = lambda b: _sigmoid(_dot(h, wg_ref[:, b * d:(b + 1) * d]))
    merged = (gate(0) * _dot(ya_ref[...], wa_ref[...])
              + gate(1) * _dot(yb_ref[...], wb_ref[...])
              + gate(2) * _dot(yc_ref[...], wc_ref[...]))
    o_ref[...] = x + _dot(merged.astype(BF16), wo_ref[...])


def _merge(ya, yb, yc, x2, g, wg, wa, wb, wc, wo, tm):
    m, d = x2.shape
    full = lambda a: pl.BlockSpec(a.shape, lambda i: (0, 0))
    row = lambda w: pl.BlockSpec((tm, w), lambda i: (i, 0))
    return pl.pallas_call(
        _merge_body,
        grid=(m // tm,),
        in_specs=[row(HG_WIDTH), row(MLA_WIDTH), row(SSM_WIDTH), row(d),
                  full(g), full(wg), full(wa), full(wb), full(wc), full(wo)],
        out_specs=row(d),
        out_shape=jax.ShapeDtypeStruct((m, d), F32),
        compiler_params=_cparams(("arbitrary",)),
        name="merge_out",
    )(ya, yb, yc, x2, g, wg, wa, wb, wc, wo)


def _mlp_body(x_ref, g_ref, wu_ref, wd_ref, fg_ref, o_ref, *, last_layer):
    x = x_ref[...]
    h = _rms(x, g_ref[...]).astype(BF16)
    u = jnp.maximum(_dot(h, wu_ref[...]), 0.0)
    y = x + _dot((u * u).astype(BF16), wd_ref[...])
    o_ref[...] = _rms(y, fg_ref[...]) if last_layer else y


def _mlp(x2, g, wu, wd, final_g, tm, last_layer):
    m, d = x2.shape
    const = lambda a: pl.BlockSpec(a.shape, lambda i: (0, 0), pipeline_mode=pl.Buffered(1))
    return pl.pallas_call(
        functools.partial(_mlp_body, last_layer=last_layer),
        grid=(m // tm,),
        in_specs=[pl.BlockSpec((tm, d), lambda i: (i, 0)), const(g), const(wu), const(wd), const(final_g)],
        out_specs=pl.BlockSpec((tm, d), lambda i: (i, 0)),
        out_shape=jax.ShapeDtypeStruct((m, d), F32),
        compiler_params=_cparams(("arbitrary",)),
        name="mlp",
    )(x2, g, wu, wd, final_g)


def _rot_cols(w):
    hr = MLA_ROPE // 2
    return jnp.concatenate([-w[..., hr:], w[..., :hr]], axis=-1)


def _pack_w_in(w_in):
    dep, d, _ = w_in.shape
    sizes = (512, 512, 512, 512, MLA_Q_LORA, MLA_KV_LORA, MLA_ROPE, SSM_WIDTH, SSM_CONV_DIM, SSM_HEADS)
    assert sum(sizes) == MIXER_IN_COLS
    pts = np.cumsum(sizes)[:-1].tolist()
    mixers = w_in[..., :MIXER_IN_COLS].astype(BF16)
    hq, hf, hi, hg, cq, ckv, kr, sz, sxbc, sdt = jnp.split(mixers, pts, axis=-1)
    z = lambda n: jnp.zeros((dep, d, n), BF16)
    misc = jnp.concatenate([_rot_cols(kr), sdt, z(32 - SSM_HEADS), kr, z(32)], axis=-1)
    packed = jnp.concatenate([hq, hf, hi, hg, sz, misc, cq, sxbc, ckv], axis=-1)
    assert packed.shape[-1] == PROJ_COLS
    return packed, w_in[..., MIXER_IN_COLS:].astype(BF16)


def _pack_w_uq(w):
    dep, r, _ = w.shape
    wh = w.reshape(dep, r, MLA_HEADS, MLA_NOPE + MLA_ROPE)
    nope, rope = wh[..., :MLA_NOPE], wh[..., MLA_NOPE:]
    z = lambda n: jnp.zeros((dep, r, MLA_HEADS, n), w.dtype)
    a = jnp.concatenate([nope, rope, z(32)], axis=-1).reshape(dep, r, MLA_HEADS * HEAD_PAD)
    b = jnp.concatenate([z(64), _rot_cols(rope), z(32)], axis=-1).reshape(dep, r, MLA_HEADS * HEAD_PAD)
    return jnp.concatenate([a, b], axis=-1).astype(BF16)


def _pack_w_ukv(w):
    dep, r, _ = w.shape
    wh = w.reshape(dep, r, MLA_HEADS // 2, 2, MLA_NOPE + MLA_V)
    kn, v = wh[..., :MLA_NOPE], wh[..., MLA_NOPE:]
    z = jnp.zeros_like(kn)
    ka = jnp.concatenate([kn, z], axis=-1).reshape(dep, r, MLA_HEADS * HEAD_PAD)
    va = jnp.stack([jnp.concatenate([v[..., 0, :], z[..., 0, :]], axis=-1),
                    jnp.concatenate([z[..., 1, :], v[..., 1, :]], axis=-1)], axis=-2)
    va = va.reshape(dep, r, MLA_HEADS * HEAD_PAD)
    return jnp.concatenate([ka, va], axis=-1).astype(BF16)


def _v_ones():
    v = np.zeros((1, MLA_HEADS * HEAD_PAD), np.float32)
    for h in range(MLA_HEADS):
        v[0, h * HEAD_PAD + (MLA_V if h % 2 == 0 else 0)] = 1.0
    return jnp.asarray(v)


def _rope_tables(positions):
    inv = ROPE_THETA ** (-jnp.arange(0, MLA_ROPE, 2, dtype=F32) / MLA_ROPE)
    ang = positions.astype(F32).reshape(-1, 1) * inv
    z = lambda n: jnp.zeros((ang.shape[0], n), F32)
    cos, sin = jnp.cos(ang), jnp.sin(ang)
    return (jnp.concatenate([z(64), cos, cos, z(32)], axis=-1),
            jnp.concatenate([z(64), sin, sin, z(32)], axis=-1))


def _row_tile(n, want):
    t = min(n, want)
    assert n % t == 0, (n, t)
    return t


def kernel(x, positions, mix_norm, w_in, hg_lb_logits, hg_norm, mla_q_norm, mla_kv_norm, mla_w_uq, mla_w_ukv,
           ssm_conv_w, ssm_conv_b, ssm_dt_bias, ssm_a_log, ssm_d, ssm_norm, w_br_hg, w_br_mla, w_br_ssm,
           w_out, mlp_norm, w_up, w_down, final_norm):
    bsz, t, d = x.shape
    depth = w_in.shape[0]
    assert d == D_MODEL and t % SSD_CHUNK == 0
    m = bsz * t
    tm = _row_tile(m, 512)
    tm_in = _row_tile(m, 1024)
    tq = _row_tile(t, 512)

    lbs = jnp.cumsum(jax.nn.softmax(hg_lb_logits.astype(F32), axis=0), axis=0)
    lbs = lbs - lbs[0:1]
    w_in_p, w_gate = _pack_w_in(w_in)
    wq_p = _pack_w_uq(mla_w_uq)
    wkv_p = _pack_w_ukv(mla_w_ukv)
    vone = _v_ones()
    cos_t, sin_t = _rope_tables(positions)
    pad_h = lambda a: jnp.pad(a.astype(F32), ((0, 0), (DT_LANE, LANES - DT_LANE - SSM_HEADS)))
    dtb_p, alog_p = pad_h(ssm_dt_bias), pad_h(ssm_a_log)
    dsk_e = jnp.repeat(ssm_d.astype(F32), SSM_HEAD_DIM, axis=-1)
    bf = lambda a: a.astype(BF16)
    wa, wb, wc, wo, wu, wd = bf(w_br_hg), bf(w_br_mla), bf(w_br_ssm), bf(w_out), bf(w_up), bf(w_down)
    r1 = lambda a: a.reshape(1, -1).astype(F32)

    x2 = x.reshape(m, d)
    for l in range(depth):
        proj = _inproj(x2, r1(mix_norm[l]), w_in_p[l], tm_in, PROJ_COLS // 2)
        ya = _hgrn(proj, r1(lbs[l]), r1(hg_norm[l]), bsz, t, tq)
        q, k, v = _mla_prep(proj, cos_t, sin_t, r1(mla_q_norm[l]), r1(mla_kv_norm[l]), wq_p[l], wkv_p[l],
                            vone, tm)
        yb = _attn(q, k, v, bsz, t, tq)
        yc = _ssd(proj, ssm_conv_w[l].astype(F32), r1(ssm_conv_b[l]), r1(dtb_p[l]), r1(alog_p[l]),
                  r1(dsk_e[l]), r1(ssm_norm[l]), bsz, t, tq)
        x2 = _merge(ya, yb, yc, x2, r1(mix_norm[l]), w_gate[l], wa[l], wb[l], wc[l], wo[l], tm)
        x2 = _mlp(x2, r1(mlp_norm[l]), wu[l], wd[l], r1(final_norm), tm, last_layer=(l == depth - 1))
    return x2.reshape(bsz, t, d)
```

```python
import functools
import math

import numpy as np
import jax
import jax.numpy as jnp
from jax import lax
from jax.experimental import pallas as pl
from jax.experimental.pallas import tpu as pltpu

F32 = jnp.float32
BF16 = jnp.bfloat16

D_MODEL = 1024
EPS = 1e-6
HG_HEADS = 4
HG_DK = 128
HG_DV = 128
HG_WIDTH = HG_HEADS * HG_DV
MLA_HEADS = 8
MLA_NOPE = 64
MLA_ROPE = 32
MLA_V = 64
MLA_Q_LORA = 384
MLA_KV_LORA = 256
MLA_WIDTH = MLA_HEADS * MLA_V
ROPE_THETA = 10000.0
SSM_HEADS = 8
SSM_HEAD_DIM = 64
SSM_WIDTH = SSM_HEADS * SSM_HEAD_DIM
SSM_GROUPS = 2
SSM_STATE = 64
SSM_CONV = 4
SSM_CONV_DIM = SSM_WIDTH + 2 * SSM_GROUPS * SSM_STATE
D_FF = 4 * D_MODEL
N_BRANCH = 3

LANES = 128
HEAD_PAD = 128

OFF_HQ, OFF_HF, OFF_HI, OFF_HG = 0, 512, 1024, 1536
OFF_SZ = 2048
OFF_MISC = 2560
DT_LANE = 32
OFF_CQ = 2688
OFF_SXBC = 3072
OFF_CKV = 3840
PROJ_COLS = 4096
MIXER_IN_COLS = 4008

HG_CHUNK = 64
HG_LEVELS = 6
SSD_CHUNK = 128
ATTN_HEADS = 8

VMEM_LIMIT = 48 * 1024 * 1024


def _cparams(sem, flags=None):
    return pltpu.CompilerParams(dimension_semantics=sem, vmem_limit_bytes=VMEM_LIMIT, flags=flags)


def _sigmoid(x):
    return jax.nn.sigmoid(x)


def _split3(x):
    hi = x.astype(BF16)
    r1 = x - hi.astype(F32)
    mid = r1.astype(BF16)
    r2 = r1 - mid.astype(F32)
    return hi, mid, r2.astype(BF16)


def _dot(a, b):
    return jnp.dot(a, b, preferred_element_type=F32)


def _dot_nt(a, b):
    return lax.dot_general(a, b, (((1,), (1,)), ((), ())), preferred_element_type=F32)


N_SPLIT = 3


def _sel_rows(sel3, x):
    return _dot(sel3, jnp.concatenate(_split3(x), axis=0))


def _sel_cols(x, sel3):
    return _dot(jnp.concatenate(_split3(x), axis=1), sel3)


def _rms(x, g):
    ms = jnp.mean(x * x, axis=-1, keepdims=True)
    return x * lax.rsqrt(ms + EPS) * g


def _inproj_body(x_ref, g_ref, w_ref, o_ref):
    h = _rms(x_ref[...], g_ref[...]).astype(BF16)
    o_ref[...] = _dot(h, w_ref[...]).astype(o_ref.dtype)


def _inproj(x2, g, w, tm, tn):
    m, d = x2.shape
    n = w.shape[1]
    return pl.pallas_call(
        _inproj_body,
        grid=(n // tn, m // tm),
        in_specs=[pl.BlockSpec((tm, d), lambda j, i: (i, 0)),
                  pl.BlockSpec((1, d), lambda j, i: (0, 0)),
                  pl.BlockSpec((d, tn), lambda j, i: (0, j))],
        out_specs=pl.BlockSpec((tm, tn), lambda j, i: (i, j)),
        out_shape=jax.ShapeDtypeStruct((m, n), BF16),
        compiler_params=_cparams(("arbitrary", "arbitrary")),
        name="inproj",
    )(x2, g, w)


def _hgrn_level_matrix():
    c = HG_CHUNK
    mat = np.zeros((8, c, c), np.float32)
    for i in range(c):
        mat[0, i, : i + 1] = 1.0
        mat[7, i, i + 1:] = 1.0
    for lvl in range(HG_LEVELS):
        m = 1 << lvl
        for r in range(c):
            start = (r // (2 * m)) * 2 * m
            mid = start + m
            if r >= mid:
                mat[lvl + 1, r, mid: r + 1] = 1.0
            else:
                mat[lvl + 1, r, r + 1: mid] = 1.0
    return mat.reshape(8 * c, c)


def _hgrn_pair_masks():
    n = 2 * HG_CHUNK
    row, col = np.meshgrid(np.arange(n), np.arange(n), indexing="ij")
    out = [row == col]
    for lvl in range(HG_LEVELS):
        same_block = (row >> (lvl + 1)) == (col >> (lvl + 1))
        out.append(same_block & (((row >> lvl) & 1) == 1) & (((col >> lvl) & 1) == 0))
    return np.stack(out).astype(np.float32)


def _hgrn_body(q_ref, f_ref, i_ref, g_ref, lb_ref, ng_ref, m_ref, mask_ref, o_ref, s_ref, *, n_chunks):
    c = HG_CHUNK

    @pl.when(pl.program_id(1) == 0)
    def _():
        s_ref[...] = jnp.zeros_like(s_ref)

    lb = lb_ref[...]
    log_lb = jnp.log(lb)
    log1m_lb = jnp.log1p(-lb)
    one_m_lb = 1.0 - lb
    ng = ng_ref[...]
    sel = m_ref[...]

    rowv = lax.broadcasted_iota(jnp.int32, (2 * c, HG_DK), 0)
    second_half = [((rowv >> lvl) & 1) == 1 for lvl in range(HG_LEVELS)]

    def chunk(ci, carry):
        r0 = pl.multiple_of(ci * c, c)
        rows = pl.ds(r0, c)
        hq = q_ref[rows, :].astype(F32)
        ff = f_ref[rows, :].astype(F32)
        hv = i_ref[rows, :].astype(F32)
        hg = g_ref[rows, :].astype(F32)
        q = hq * _sigmoid(hq)
        e_f = jnp.exp(-jnp.abs(ff))
        one_p = 1.0 + e_f
        log_sig = jnp.minimum(ff, 0.0) - jnp.log(one_p)
        b = log1m_lb + log_sig
        lf = jnp.maximum(log_lb, b) + jnp.log(1.0 + jnp.exp(-jnp.abs(log_lb - b)))
        k = one_m_lb * jnp.where(ff >= 0.0, e_f, 1.0) / one_p
        gate = hg * _sigmoid(hg)
        ex = _sel_rows(sel, lf)

        for pair in range(HG_HEADS // 2):
            sls = [slice(h * HG_DK, (h + 1) * HG_DK) for h in (2 * pair, 2 * pair + 1)]
            stack = lambda a, lo=0, hi=c: jnp.concatenate([a[lo:hi, sl] for sl in sls], axis=0)
            level = lambda n: stack(ex, n * c, (n + 1) * c)
            qp, kp, vp = stack(q).astype(BF16), stack(k).astype(BF16), stack(hv)
            decay_of = lambda n: jnp.exp(level(n).astype(BF16))
            att = mask_ref[0] * _dot_nt(qp, kp)
            for lvl in range(HG_LEVELS):
                xl = jnp.where(second_half[lvl], qp, kp) * decay_of(lvl + 1)
                att = att + mask_ref[lvl + 1] * _dot_nt(xl, xl)
            o_intra = _dot(att.astype(BF16), vp.astype(BF16))
            e0 = level(0)
            q_in = qp * jnp.exp(e0.astype(BF16))
            k_out = kp * decay_of(7)
            for idx, sl in enumerate(sls):
                h = 2 * pair + idx
                r = slice(idx * c, (idx + 1) * c)
                st = s_ref[h]
                o = o_intra[r] + _dot_nt(q_in[r], st.astype(BF16))
                decay = jnp.exp(e0[(idx + 1) * c - 1:(idx + 1) * c, :])
                s_ref[h] = st * decay + _dot(vp[r].T.astype(BF16), k_out[r])
                y = _rms(o, ng[:, sl]) * gate[:, sl]
                o_ref[rows, sl] = y.astype(o_ref.dtype)
        return carry

    lax.fori_loop(0, n_chunks, chunk, 0, unroll=8)


def _hgrn(proj, lb, ng, bsz, t, tq):
    m = proj.shape[0]
    nq = t // tq
    sel = jnp.asarray(np.tile(_hgrn_level_matrix(), (1, N_SPLIT)), BF16)
    masks = jnp.asarray(_hgrn_pair_masks())
    w = HG_WIDTH

    def col_spec(off):
        blk = off // w
        return pl.BlockSpec((tq, w), lambda b, i: (b * nq + i, blk))

    vec = pl.BlockSpec((1, w), lambda b, i: (0, 0))
    return pl.pallas_call(
        functools.partial(_hgrn_body, n_chunks=tq // HG_CHUNK),
        grid=(bsz, nq),
        in_specs=[col_spec(OFF_HQ), col_spec(OFF_HF), col_spec(OFF_HI), col_spec(OFF_HG), vec, vec,
                  pl.BlockSpec(sel.shape, lambda b, i: (0, 0)),
                  pl.BlockSpec(masks.shape, lambda b, i: (0, 0, 0))],
        out_specs=pl.BlockSpec((tq, w), lambda b, i: (b * nq + i, 0)),
        out_shape=jax.ShapeDtypeStruct((m, w), BF16),
        scratch_shapes=[pltpu.VMEM((HG_HEADS, HG_DV, HG_DK), F32)],
        compiler_params=_cparams(("arbitrary", "arbitrary")),
        name="hgrn2",
    )(proj, proj, proj, proj, lb, ng, sel, masks)


def _mla_prep_body(cq_ref, ckv_ref, kr_ref, cos_ref, sin_ref, qn_ref, kvn_ref, wq_ref, wkv_ref, vone_ref,
                   q_out, k_out, v_out):
    scale = (MLA_NOPE + MLA_ROPE) ** -0.5 * math.log2(math.e)
    cqn = _rms(cq_ref[...].astype(F32), qn_ref[...]).astype(BF16)
    ckvn = _rms(ckv_ref[...].astype(F32), kvn_ref[...]).astype(BF16)
    qq = _dot(cqn, wq_ref[...])
    kv = _dot(ckvn, wkv_ref[...])
    cos = cos_ref[...]
    sin = sin_ref[...]
    lane = lax.broadcasted_iota(jnp.int32, (1, HEAD_PAD), 1)
    cq_t = scale * (cos + (lane < MLA_NOPE).astype(F32))
    sq_t = scale * sin
    krb = kr_ref[...].astype(F32)
    k_rope = krb * cos + pltpu.roll(krb, 64, 1) * sin
    hw = MLA_HEADS * HEAD_PAD
    vone = vone_ref[...]
    for h in range(MLA_HEADS):
        sl = slice(h * HEAD_PAD, (h + 1) * HEAD_PAD)
        sl2 = slice(hw + h * HEAD_PAD, hw + (h + 1) * HEAD_PAD)
        q_out[:, sl] = (qq[:, sl] * cq_t + qq[:, sl2] * sq_t).astype(BF16)
        k_out[:, sl] = (kv[:, sl] + k_rope).astype(BF16)
        v_out[:, sl] = (kv[:, sl2] + vone[:, sl]).astype(BF16)


def _mla_prep(proj, cos_t, sin_t, qn, kvn, wq, wkv, vone, tm):
    m = proj.shape[0]
    hw = MLA_HEADS * HEAD_PAD
    full = lambda a: pl.BlockSpec(a.shape, lambda i: (0, 0))
    out = jax.ShapeDtypeStruct((m, hw), BF16)
    ospec = pl.BlockSpec((tm, hw), lambda i: (i, 0))
    return pl.pallas_call(
        _mla_prep_body,
        grid=(m // tm,),
        in_specs=[pl.BlockSpec((tm, MLA_Q_LORA), lambda i: (i, OFF_CQ // MLA_Q_LORA)),
                  pl.BlockSpec((tm, MLA_KV_LORA), lambda i: (i, OFF_CKV // MLA_KV_LORA)),
                  pl.BlockSpec((tm, LANES), lambda i: (i, OFF_MISC // LANES)),
                  pl.BlockSpec((tm, LANES), lambda i: (i, 0)),
                  pl.BlockSpec((tm, LANES), lambda i: (i, 0)),
                  full(qn), full(kvn), full(wq), full(wkv), full(vone)],
        out_specs=[ospec, ospec, ospec],
        out_shape=[out, out, out],
        compiler_params=_cparams(("arbitrary",)),
        name="mla_prep",
    )(proj, proj, proj, cos_t, sin_t, qn, kvn, wq, wkv, vone)


def _attn_body(q_ref, k_ref, v_ref, o_ref, *, tq):
    i = pl.program_id(2)
    lane = lax.broadcasted_iota(jnp.int32, (1, HEAD_PAD), 1)
    qpos = lax.broadcasted_iota(jnp.int32, (tq, tq), 0)
    kpos = lax.broadcasted_iota(jnp.int32, (tq, tq), 1)
    causal = kpos <= qpos
    heads = range(ATTN_HEADS)
    slices = [slice(h * HEAD_PAD, (h + 1) * HEAD_PAD) for h in heads]
    block = lambda j: pl.ds(pl.multiple_of(j * tq, tq), tq)

    def scores(h, j, masked, m_prev):
        s = _dot_nt(q_ref[:, slices[h]], k_ref[block(j), slices[h]])
        if masked:
            s = jnp.where(causal, s, -jnp.inf)
        m_new = jnp.maximum(m_prev, jnp.max(s, axis=-1, keepdims=True))
        return m_new, jnp.exp2(m_prev - m_new), jnp.exp2((s - m_new).astype(BF16))

    def step(j, state, masked):
        new = []
        for h in heads:
            m, acc = state[h]
            m, alpha, p = scores(h, j, masked, m)
            new.append((m, alpha * acc + _dot(p, v_ref[block(j), slices[h]])))
        return tuple(new)

    def finish(state):
        for pair in range(ATTN_HEADS // 2):
            acc0, acc1 = state[2 * pair][1], state[2 * pair + 1][1]
            den0 = jnp.sum(jnp.where(lane == MLA_V, acc0, 0.0), axis=-1, keepdims=True)
            den1 = jnp.sum(jnp.where(lane == 0, acc1, 0.0), axis=-1, keepdims=True)
            o_ref[:, pair * HEAD_PAD:(pair + 1) * HEAD_PAD] = jnp.where(
                lane < MLA_V, acc0 / den0, acc1 / den1).astype(o_ref.dtype)

    def two_steps(j2, state):
        return step(2 * j2 + 1, step(2 * j2, state, False), False)

    init = (jnp.full((tq, 1), -jnp.inf, F32), jnp.zeros((tq, HEAD_PAD), F32))
    state = lax.fori_loop(0, i // 2, two_steps, (init,) * ATTN_HEADS)

    @pl.when(i % 2 == 0)
    def _():
        finish(step(i, state, True))

    @pl.when(i % 2 == 1)
    def _():
        finish(step(i, step(i - 1, state, False), True))


def _attn(q, k, v, bsz, t, tq):
    m = q.shape[0]
    nq = t // tq
    pw = ATTN_HEADS * HEAD_PAD
    return pl.pallas_call(
        functools.partial(_attn_body, tq=tq),
        grid=(bsz, MLA_HEADS // ATTN_HEADS, nq),
        in_specs=[pl.BlockSpec((tq, pw), lambda b, h, i: (b * nq + i, h)),
                  pl.BlockSpec((t, pw), lambda b, h, i: (b, h)),
                  pl.BlockSpec((t, pw), lambda b, h, i: (b, h))],
        out_specs=pl.BlockSpec((tq, ATTN_HEADS * MLA_V), lambda b, h, i: (b * nq + i, h)),
        out_shape=jax.ShapeDtypeStruct((m, MLA_WIDTH), BF16),
        compiler_params=_cparams(("arbitrary", "arbitrary", "arbitrary")),
        name="mla_attn",
    )(q, k, v)


def _softplus(x):
    return jnp.maximum(x, 0.0) + jnp.log1p(jnp.exp(-jnp.abs(x)))


def _ssd_body(z_ref, dt_ref, xbc_ref, cw_ref, cb_ref, dtb_ref, alog_ref, dsk_ref, ng_ref, tril_ref, e8_ref,
              o_ref, xp_ref, xc_ref, s_ref, *, tq):
    L = SSD_CHUNK
    gn = SSM_GROUPS * SSM_STATE
    half = SSM_WIDTH // SSM_GROUPS
    t_idx = pl.program_id(1)

    @pl.when(t_idx == 0)
    def _():
        s_ref[...] = jnp.zeros_like(s_ref)
        xp_ref[0:8, :] = jnp.zeros((8, SSM_CONV_DIM), F32)

    @pl.when(t_idx > 0)
    def _():
        xp_ref[0:8, :] = xp_ref[tq:tq + 8, :]

    xp_ref[8:8 + tq, :] = xbc_ref[...].astype(F32)
    cw = cw_ref[...]
    acc = cb_ref[...] + cw[3:4, :] * xp_ref[8:8 + tq, :]
    for w in range(SSM_CONV - 1):
        acc = acc + cw[w:w + 1, :] * xp_ref[5 + w:5 + w + tq, :]
    xc_ref[...] = acc * _sigmoid(acc)

    a_neg = -jnp.exp(alog_ref[...])
    dtb = dtb_ref[...]
    dsk = dsk_ref[...]
    ng = ng_ref[...]
    tril = tril_ref[...]
    e8 = e8_ref[...]
    ri = lax.broadcasted_iota(jnp.int32, (L, L), 0)
    ci = lax.broadcasted_iota(jnp.int32, (L, L), 1)
    tri = ci <= ri
    lane_gn = lax.broadcasted_iota(jnp.int32, (1, gn), 1)
    lane_w = lax.broadcasted_iota(jnp.int32, (1, SSM_WIDTH), 1)
    row_gn = lax.broadcasted_iota(jnp.int32, (gn, SSM_WIDTH), 0)
    col_w = lax.broadcasted_iota(jnp.int32, (gn, SSM_WIDTH), 1)
    blockdiag = (row_gn // SSM_STATE) == (col_w // half)
    head_mask = [(lane_w // SSM_HEAD_DIM) == h for h in range(SSM_HEADS)]

    def chunk(c, carry):
        rows = pl.ds(pl.multiple_of(c * L, L), L)
        xc = xc_ref[rows, :]
        xs = xc[:, :SSM_WIDTH]
        bm = xc[:, SSM_WIDTH:SSM_WIDTH + gn]
        cm = xc[:, SSM_WIDTH + gn:]
        dt = _softplus(dt_ref[rows, :].astype(F32) + dtb)
        a = dt * a_neg
        acum = _sel_rows(tril, a)
        dt_e = _sel_cols(dt, e8)
        ac_e = _sel_cols(acum, e8)
        x_dt = xs * dt_e
        last = ac_e[L - 1:L, :]
        x_dec = (x_dt * jnp.exp(last - ac_e)).astype(BF16)
        x_b = x_dt.astype(BF16)
        bm_t = bm.T.astype(BF16)
        cb0 = _dot(jnp.where(lane_gn < SSM_STATE, cm, 0.0).astype(BF16), bm_t)
        cb1 = _dot(jnp.where(lane_gn >= SSM_STATE, cm, 0.0).astype(BF16), bm_t)
        ac_t = acum.T
        ws, xblk = [], []
        for h in range(SSM_HEADS):
            hl = DT_LANE + h
            seg = acum[:, hl:hl + 1] - ac_t[hl:hl + 1, :]
            lmat = jnp.exp(jnp.where(tri, seg, -jnp.inf))
            ws.append((lmat * (cb0 if h < SSM_HEADS // SSM_GROUPS else cb1)).astype(BF16))
            xblk.append(jnp.where(head_mask[h], x_b, jnp.zeros_like(x_b)))
        y = _dot(jnp.concatenate(ws, axis=1), jnp.concatenate(xblk, axis=0))
        st = s_ref[...]
        y = y + jnp.exp(ac_e) * _dot(cm.astype(BF16), st.astype(BF16))
        s_ref[...] = jnp.exp(last) * st + jnp.where(blockdiag, _dot(bm_t, x_dec), 0.0)
        y = y + xs * dsk
        z = z_ref[rows, :].astype(F32)
        y = y * (z * _sigmoid(z))
        y0 = _rms(y[:, :half], ng[:, :half])
        y1 = _rms(y[:, half:], ng[:, half:])
        o_ref[rows, :half] = y0.astype(o_ref.dtype)
        o_ref[rows, half:] = y1.astype(o_ref.dtype)
        return carry

    lax.fori_loop(0, tq // L, chunk, 0, unroll=4)


def _ssd(proj, cw, cb, dtb, alog, dsk, ng, bsz, t, tq):
    m = proj.shape[0]
    nq = t // tq
    L = SSD_CHUNK
    tril = jnp.asarray(np.tile(np.tril(np.ones((L, L), np.float32)), (1, N_SPLIT)), BF16)
    e8 = np.zeros((LANES, SSM_WIDTH), np.float32)
    for h in range(SSM_HEADS):
        e8[DT_LANE + h, h * SSM_HEAD_DIM:(h + 1) * SSM_HEAD_DIM] = 1.0
    e8 = jnp.asarray(np.tile(e8, (N_SPLIT, 1)), BF16)
    full = lambda a: pl.BlockSpec(a.shape, lambda b, i: (0, 0))
    return pl.pallas_call(
        functools.partial(_ssd_body, tq=tq),
        grid=(bsz, nq),
        in_specs=[pl.BlockSpec((tq, SSM_WIDTH), lambda b, i: (b * nq + i, OFF_SZ // SSM_WIDTH)),
                  pl.BlockSpec((tq, LANES), lambda b, i: (b * nq + i, OFF_MISC // LANES)),
                  pl.BlockSpec((tq, SSM_CONV_DIM), lambda b, i: (b * nq + i, OFF_SXBC // SSM_CONV_DIM)),
                  full(cw), full(cb), full(dtb), full(alog), full(dsk), full(ng), full(tril), full(e8)],
        out_specs=pl.BlockSpec((tq, SSM_WIDTH), lambda b, i: (b * nq + i, 0)),
        out_shape=jax.ShapeDtypeStruct((m, SSM_WIDTH), BF16),
        scratch_shapes=[pltpu.VMEM((tq + 8, SSM_CONV_DIM), F32),
                        pltpu.VMEM((tq, SSM_CONV_DIM), F32),
                        pltpu.VMEM((SSM_GROUPS * SSM_STATE, SSM_WIDTH), F32)],
        compiler_params=_cparams(("arbitrary", "arbitrary")),
        name="ssd",
    )(proj, proj, proj, cw, cb, dtb, alog, dsk, ng, tril, e8)


def _merge_body(ya_ref, yb_ref, yc_ref, x_ref, g_ref, wg_ref, wa_ref, wb_ref, wc_ref, wo_ref, o_ref):
    d = D_MODEL
    x = x_ref[...]
    h = _rms(x, g_ref[...]).astype(BF16)
    gate = lambda b: _sigmoid(_dot(h, wg_ref[:, b * d:(b + 1) * d]))
    merged = (gate(0) * _dot(ya_ref[...], wa_ref[...])
              + gate(1) * _dot(yb_ref[...], wb_ref[...])
              + gate(2) * _dot(yc_ref[...], wc_ref[...]))
    o_ref[...] = x + _dot(merged.astype(BF16), wo_ref[...])


def _merge(ya, yb, yc, x2, g, wg, wa, wb, wc, wo, tm):
    m, d = x2.shape
    full = lambda a: pl.BlockSpec(a.shape, lambda i: (0, 0))
    row = lambda w: pl.BlockSpec((tm, w), lambda i: (i, 0))
    return pl.pallas_call(
        _merge_body,
        grid=(m // tm,),
        in_specs=[row(HG_WIDTH), row(MLA_WIDTH), row(SSM_WIDTH), row(d),
                  full(g), full(wg), full(wa), full(wb), full(wc), full(wo)],
        out_specs=row(d),
        out_shape=jax.ShapeDtypeStruct((m, d), F32),
        compiler_params=_cparams(("arbitrary",)),
        name="merge_out",
    )(ya, yb, yc, x2, g, wg, wa, wb, wc, wo)


def _mlp_body(x_ref, g_ref, wu_ref, wd_ref, fg_ref, o_ref, *, last_layer):
    x = x_ref[...]
    h = _rms(x, g_ref[...]).astype(BF16)
    u = jnp.maximum(_dot(h, wu_ref[...]), 0.0)
    y = x + _dot((u * u).astype(BF16), wd_ref[...])
    o_ref[...] = _rms(y, fg_ref[...]) if last_layer else y


def _mlp(x2, g, wu, wd, final_g, tm, last_layer):
    m, d = x2.shape
    const = lambda a: pl.BlockSpec(a.shape, lambda i: (0, 0), pipeline_mode=pl.Buffered(1))
    return pl.pallas_call(
        functools.partial(_mlp_body, last_layer=last_layer),
        grid=(m // tm,),
        in_specs=[pl.BlockSpec((tm, d), lambda i: (i, 0)), const(g), const(wu), const(wd), const(final_g)],
        out_specs=pl.BlockSpec((tm, d), lambda i: (i, 0)),
        out_shape=jax.ShapeDtypeStruct((m, d), F32),
        compiler_params=_cparams(("arbitrary",)),
        name="mlp",
    )(x2, g, wu, wd, final_g)


def _rot_cols(w):
    hr = MLA_ROPE // 2
    return jnp.concatenate([-w[..., hr:], w[..., :hr]], axis=-1)


def _pack_w_in(w_in):
    dep, d, _ = w_in.shape
    sizes = (512, 512, 512, 512, MLA_Q_LORA, MLA_KV_LORA, MLA_ROPE, SSM_WIDTH, SSM_CONV_DIM, SSM_HEADS)
    assert sum(sizes) == MIXER_IN_COLS
    pts = np.cumsum(sizes)[:-1].tolist()
    mixers = w_in[..., :MIXER_IN_COLS].astype(BF16)
    hq, hf, hi, hg, cq, ckv, kr, sz, sxbc, sdt = jnp.split(mixers, pts, axis=-1)
    z = lambda n: jnp.zeros((dep, d, n), BF16)
    misc = jnp.concatenate([_rot_cols(kr), sdt, z(32 - SSM_HEADS), kr, z(32)], axis=-1)
    packed = jnp.concatenate([hq, hf, hi, hg, sz, misc, cq, sxbc, ckv], axis=-1)
    assert packed.shape[-1] == PROJ_COLS
    return packed, w_in[..., MIXER_IN_COLS:].astype(BF16)


def _pack_w_uq(w):
    dep, r, _ = w.shape
    wh = w.reshape(dep, r, MLA_HEADS, MLA_NOPE + MLA_ROPE)
    nope, rope = wh[..., :MLA_NOPE], wh[..., MLA_NOPE:]
    z = lambda n: jnp.zeros((dep, r, MLA_HEADS, n), w.dtype)
    a = jnp.concatenate([nope, rope, z(32)], axis=-1).reshape(dep, r, MLA_HEADS * HEAD_PAD)
    b = jnp.concatenate([z(64), _rot_cols(rope), z(32)], axis=-1).reshape(dep, r, MLA_HEADS * HEAD_PAD)
    return jnp.concatenate([a, b], axis=-1).astype(BF16)


def _pack_w_ukv(w):
    dep, r, _ = w.shape
    wh = w.reshape(dep, r, MLA_HEADS // 2, 2, MLA_NOPE + MLA_V)
    kn, v = wh[..., :MLA_NOPE], wh[..., MLA_NOPE:]
    z = jnp.zeros_like(kn)
    ka = jnp.concatenate([kn, z], axis=-1).reshape(dep, r, MLA_HEADS * HEAD_PAD)
    va = jnp.stack([jnp.concatenate([v[..., 0, :], z[..., 0, :]], axis=-1),
                    jnp.concatenate([z[..., 1, :], v[..., 1, :]], axis=-1)], axis=-2)
    va = va.reshape(dep, r, MLA_HEADS * HEAD_PAD)
    return jnp.concatenate([ka, va], axis=-1).astype(BF16)


def _v_ones():
    v = np.zeros((1, MLA_HEADS * HEAD_PAD), np.float32)
    for h in range(MLA_HEADS):
        v[0, h * HEAD_PAD + (MLA_V if h % 2 == 0 else 0)] = 1.0
    return jnp.asarray(v)


def _rope_tables(positions):
    inv = ROPE_THETA ** (-jnp.arange(0, MLA_ROPE, 2, dtype=F32) / MLA_ROPE)
    ang = positions.astype(F32).reshape(-1, 1) * inv
    z = lambda n: jnp.zeros((ang.shape[0], n), F32)
    cos, sin = jnp.cos(ang), jnp.sin(ang)
    return (jnp.concatenate([z(64), cos, cos, z(32)], axis=-1),
            jnp.concatenate([z(64), sin, sin, z(32)], axis=-1))


def _row_tile(n, want):
    t = min(n, want)
    assert n % t == 0, (n, t)
    return t


def kernel(x, positions, mix_norm, w_in, hg_lb_logits, hg_norm, mla_q_norm, mla_kv_norm, mla_w_uq, mla_w_ukv,
           ssm_conv_w, ssm_conv_b, ssm_dt_bias, ssm_a_log, ssm_d, ssm_norm, w_br_hg, w_br_mla, w_br_ssm,
           w_out, mlp_norm, w_up, w_down, final_norm):
    bsz, t, d = x.shape
    depth = w_in.shape[0]
    assert d == D_MODEL and t % SSD_CHUNK == 0
    m = bsz * t
    tm = _row_tile(m, 512)
    tm_in = _row_tile(m, 1024)
    tq = _row_tile(t, 512)

    lbs = jnp.cumsum(jax.nn.softmax(hg_lb_logits.astype(F32), axis=0), axis=0)
    lbs = lbs - lbs[0:1]
    w_in_p, w_gate = _pack_w_in(w_in)
    wq_p = _pack_w_uq(mla_w_uq)
    wkv_p = _pack_w_ukv(mla_w_ukv)
    vone = _v_ones()
    cos_t, sin_t = _rope_tables(positions)
    pad_h = lambda a: jnp.pad(a.astype(F32), ((0, 0), (DT_LANE, LANES - DT_LANE - SSM_HEADS)))
    dtb_p, alog_p = pad_h(ssm_dt_bias), pad_h(ssm_a_log)
    dsk_e = jnp.repeat(ssm_d.astype(F32), SSM_HEAD_DIM, axis=-1)
    bf = lambda a: a.astype(BF16)
    wa, wb, wc, wo, wu, wd = bf(w_br_hg), bf(w_br_mla), bf(w_br_ssm), bf(w_out), bf(w_up), bf(w_down)
    r1 = lambda a: a.reshape(1, -1).astype(F32)

    x2 = x.reshape(m, d)
    for l in range(depth):
        proj = _inproj(x2, r1(mix_norm[l]), w_in_p[l], tm_in, PROJ_COLS // 2)
        ya = _hgrn(proj, r1(lbs[l]), r1(hg_norm[l]), bsz, t, tq)
        q, k, v = _mla_prep(proj, cos_t, sin_t, r1(mla_q_norm[l]), r1(mla_kv_norm[l]), wq_p[l], wkv_p[l],
                            vone, tm)
        yb = _attn(q, k, v, bsz, t, tq)
        yc = _ssd(proj, ssm_conv_w[l].astype(F32), r1(ssm_conv_b[l]), r1(dtb_p[l]), r1(alog_p[l]),
                  r1(dsk_e[l]), r1(ssm_norm[l]), bsz, t, tq)
        x2 = _merge(ya, yb, yc, x2, r1(mix_norm[l]), w_gate[l], wa[l], wb[l], wc[l], wo[l], tm)
        x2 = _mlp(x2, r1(mlp_norm[l]), wu[l], wd[l], r1(final_norm), tm, last_layer=(l == depth - 1))
    return x2.reshape(bsz, t, d)
```

```python
import functools
import math

import numpy as np
import jax
import jax.numpy as jnp
from jax import lax
from jax.experimental import pallas as pl
from jax.experimental.pallas import tpu as pltpu

F32 = jnp.float32
BF16 = jnp.bfloat16

D_MODEL = 1024
EPS = 1e-6
HG_HEADS = 4
HG_DK = 128
HG_DV = 128
HG_WIDTH = HG_HEADS * HG_DV
MLA_HEADS = 8
MLA_NOPE = 64
MLA_ROPE = 32
MLA_V = 64
MLA_Q_LORA = 384
MLA_KV_LORA = 256
MLA_WIDTH = MLA_HEADS * MLA_V
ROPE_THETA = 10000.0
SSM_HEADS = 8
SSM_HEAD_DIM = 64
SSM_WIDTH = SSM_HEADS * SSM_HEAD_DIM
SSM_GROUPS = 2
SSM_STATE = 64
SSM_CONV = 4
SSM_CONV_DIM = SSM_WIDTH + 2 * SSM_GROUPS * SSM_STATE
D_FF = 4 * D_MODEL
N_BRANCH = 3

LANES = 128
HEAD_PAD = 128

OFF_HQ, OFF_HF, OFF_HI, OFF_HG = 0, 512, 1024, 1536
OFF_SZ = 2048
OFF_MISC = 2560
DT_LANE = 32
OFF_CQ = 2688
OFF_SXBC = 3072
OFF_CKV = 3840
PROJ_COLS = 4096
MIXER_IN_COLS = 4008

HG_CHUNK = 64
HG_LEVELS = 6
SSD_CHUNK = 128
ATTN_HEADS = 8

VMEM_LIMIT = 48 * 1024 * 1024


def _cparams(sem, flags=None):
    return pltpu.CompilerParams(dimension_semantics=sem, vmem_limit_bytes=VMEM_LIMIT, flags=flags)


def _sigmoid(x):
    return jax.nn.sigmoid(x)


def _split3(x):
    hi = x.astype(BF16)
    r1 = x - hi.astype(F32)
    mid = r1.astype(BF16)
    r2 = r1 - mid.astype(F32)
    return hi, mid, r2.astype(BF16)


def _dot(a, b):
    return jnp.dot(a, b, preferred_element_type=F32)


def _dot_nt(a, b):
    return lax.dot_general(a, b, (((1,), (1,)), ((), ())), preferred_element_type=F32)


N_SPLIT = 3


def _sel_rows(sel3, x):
    return _dot(sel3, jnp.concatenate(_split3(x), axis=0))


def _sel_cols(x, sel3):
    return _dot(jnp.concatenate(_split3(x), axis=1), sel3)


def _rms(x, g):
    ms = jnp.mean(x * x, axis=-1, keepdims=True)
    return x * lax.rsqrt(ms + EPS) * g


def _inproj_body(x_ref, g_ref, w_ref, o_ref):
    h = _rms(x_ref[...], g_ref[...]).astype(BF16)
    o_ref[...] = _dot(h, w_ref[...]).astype(o_ref.dtype)


def _inproj(x2, g, w, tm, tn):
    m, d = x2.shape
    n = w.shape[1]
    return pl.pallas_call(
        _inproj_body,
        grid=(n // tn, m // tm),
        in_specs=[pl.BlockSpec((tm, d), lambda j, i: (i, 0)),
                  pl.BlockSpec((1, d), lambda j, i: (0, 0)),
                  pl.BlockSpec((d, tn), lambda j, i: (0, j))],
        out_specs=pl.BlockSpec((tm, tn), lambda j, i: (i, j)),
        out_shape=jax.ShapeDtypeStruct((m, n), BF16),
        compiler_params=_cparams(("arbitrary", "arbitrary")),
        name="inproj",
    )(x2, g, w)


def _hgrn_level_matrix():
    c = HG_CHUNK
    mat = np.zeros((8, c, c), np.float32)
    for i in range(c):
        mat[0, i, : i + 1] = 1.0
        mat[7, i, i + 1:] = 1.0
    for lvl in range(HG_LEVELS):
        m = 1 << lvl
        for r in range(c):
            start = (r // (2 * m)) * 2 * m
            mid = start + m
            if r >= mid:
                mat[lvl + 1, r, mid: r + 1] = 1.0
            else:
                mat[lvl + 1, r, r + 1: mid] = 1.0
    return mat.reshape(8 * c, c)


def _hgrn_pair_masks():
    n = 2 * HG_CHUNK
    row, col = np.meshgrid(np.arange(n), np.arange(n), indexing="ij")
    out = [row == col]
    for lvl in range(HG_LEVELS):
        same_block = (row >> (lvl + 1)) == (col >> (lvl + 1))
        out.append(same_block & (((row >> lvl) & 1) == 1) & (((col >> lvl) & 1) == 0))
    return np.stack(out).astype(np.float32)


def _hgrn_body(q_ref, f_ref, i_ref, g_ref, lb_ref, ng_ref, m_ref, mask_ref, o_ref, s_ref, *, n_chunks):
    c = HG_CHUNK

    @pl.when(pl.program_id(1) == 0)
    def _():
        s_ref[...] = jnp.zeros_like(s_ref)

    lb = lb_ref[...]
    log_lb = jnp.log(lb)
    log1m_lb = jnp.log1p(-lb)
    one_m_lb = 1.0 - lb
    ng = ng_ref[...]
    sel = m_ref[...]

    rowv = lax.broadcasted_iota(jnp.int32, (2 * c, HG_DK), 0)
    second_half = [((rowv >> lvl) & 1) == 1 for lvl in range(HG_LEVELS)]

    def chunk(ci, carry):
        r0 = pl.multiple_of(ci * c, c)
        rows = pl.ds(r0, c)
        hq = q_ref[rows, :].astype(F32)
        ff = f_ref[rows, :].astype(F32)
        hv = i_ref[rows, :].astype(F32)
        hg = g_ref[rows, :].astype(F32)
        q = hq * _sigmoid(hq)
        e_f = jnp.exp(-jnp.abs(ff))
        one_p = 1.0 + e_f
        log_sig = jnp.minimum(ff, 0.0) - jnp.log(one_p)
        b = log1m_lb + log_sig
        lf = jnp.maximum(log_lb, b) + jnp.log(1.0 + jnp.exp(-jnp.abs(log_lb - b)))
        k = one_m_lb * jnp.where(ff >= 0.0, e_f, 1.0) / one_p
        gate = hg * _sigmoid(hg)
        ex = _sel_rows(sel, lf)

        for pair in range(HG_HEADS // 2):
            sls = [slice(h * HG_DK, (h + 1) * HG_DK) for h in (2 * pair, 2 * pair + 1)]
            stack = lambda a, lo=0, hi=c: jnp.concatenate([a[lo:hi, sl] for sl in sls], axis=0)
            level = lambda n: stack(ex, n * c, (n + 1) * c)
            qp, kp, vp = stack(q).astype(BF16), stack(k).astype(BF16), stack(hv)
            decay_of = lambda n: jnp.exp(level(n).astype(BF16))
            att = mask_ref[0] * _dot(qp, stack(k).T.astype(BF16))
            for lvl in range(HG_LEVELS):
                xl = jnp.where(second_half[lvl], qp, kp) * decay_of(lvl + 1)
                att = att + mask_ref[lvl + 1] * _dot(xl, xl.astype(F32).T.astype(BF16))
            o_intra = _dot(att.astype(BF16), vp.astype(BF16))
            e0 = level(0)
            q_in = qp * jnp.exp(e0.astype(BF16))
            k_out = kp * decay_of(7)
            for idx, sl in enumerate(sls):
                h = 2 * pair + idx
                r = slice(idx * c, (idx + 1) * c)
                st = s_ref[h]
                o = o_intra[r] + _dot(q_in[r], st.T.astype(BF16))
                decay = jnp.exp(e0[(idx + 1) * c - 1:(idx + 1) * c, :])
                s_ref[h] = st * decay + _dot(vp[r].T.astype(BF16), k_out[r])
                y = _rms(o, ng[:, sl]) * gate[:, sl]
                o_ref[rows, sl] = y.astype(o_ref.dtype)
        return carry

    lax.fori_loop(0, n_chunks, chunk, 0, unroll=8)


def _hgrn(proj, lb, ng, bsz, t, tq):
    m = proj.shape[0]
    nq = t // tq
    sel = jnp.asarray(np.tile(_hgrn_level_matrix(), (1, N_SPLIT)), BF16)
    masks = jnp.asarray(_hgrn_pair_masks())
    w = HG_WIDTH

    def col_spec(off):
        blk = off // w
        return pl.BlockSpec((tq, w), lambda b, i: (b * nq + i, blk))

    vec = pl.BlockSpec((1, w), lambda b, i: (0, 0))
    return pl.pallas_call(
        functools.partial(_hgrn_body, n_chunks=tq // HG_CHUNK),
        grid=(bsz, nq),
        in_specs=[col_spec(OFF_HQ), col_spec(OFF_HF), col_spec(OFF_HI), col_spec(OFF_HG), vec, vec,
                  pl.BlockSpec(sel.shape, lambda b, i: (0, 0)),
                  pl.BlockSpec(masks.shape, lambda b, i: (0, 0, 0))],
        out_specs=pl.BlockSpec((tq, w), lambda b, i: (b * nq + i, 0)),
        out_shape=jax.ShapeDtypeStruct((m, w), BF16),
        scratch_shapes=[pltpu.VMEM((HG_HEADS, HG_DV, HG_DK), F32)],
        compiler_params=_cparams(("arbitrary", "arbitrary")),
        name="hgrn2",
    )(proj, proj, proj, proj, lb, ng, sel, masks)


def _mla_prep_body(cq_ref, ckv_ref, kr_ref, cos_ref, sin_ref, qn_ref, kvn_ref, wq_ref, wkv_ref, vone_ref,
                   q_out, k_out, v_out):
    scale = (MLA_NOPE + MLA_ROPE) ** -0.5 * math.log2(math.e)
    cqn = _rms(cq_ref[...].astype(F32), qn_ref[...]).astype(BF16)
    ckvn = _rms(ckv_ref[...].astype(F32), kvn_ref[...]).astype(BF16)
    qq = _dot(cqn, wq_ref[...])
    kv = _dot(ckvn, wkv_ref[...])
    cos = cos_ref[...]
    sin = sin_ref[...]
    lane = lax.broadcasted_iota(jnp.int32, (1, HEAD_PAD), 1)
    cq_t = scale * (cos + (lane < MLA_NOPE).astype(F32))
    sq_t = scale * sin
    krb = kr_ref[...].astype(F32)
    k_rope = krb * cos + pltpu.roll(krb, 64, 1) * sin
    hw = MLA_HEADS * HEAD_PAD
    vone = vone_ref[...]
    for h in range(MLA_HEADS):
        sl = slice(h * HEAD_PAD, (h + 1) * HEAD_PAD)
        sl2 = slice(hw + h * HEAD_PAD, hw + (h + 1) * HEAD_PAD)
        q_out[:, sl] = (qq[:, sl] * cq_t + qq[:, sl2] * sq_t).astype(BF16)
        k_out[:, sl] = (kv[:, sl] + k_rope).astype(BF16)
        v_out[:, sl] = (kv[:, sl2] + vone[:, sl]).astype(BF16)


def _mla_prep(proj, cos_t, sin_t, qn, kvn, wq, wkv, vone, tm):
    m = proj.shape[0]
    hw = MLA_HEADS * HEAD_PAD
    full = lambda a: pl.BlockSpec(a.shape, lambda i: (0, 0))
    out = jax.ShapeDtypeStruct((m, hw), BF16)
    ospec = pl.BlockSpec((tm, hw), lambda i: (i, 0))
    return pl.pallas_call(
        _mla_prep_body,
        grid=(m // tm,),
        in_specs=[pl.BlockSpec((tm, MLA_Q_LORA), lambda i: (i, OFF_CQ // MLA_Q_LORA)),
                  pl.BlockSpec((tm, MLA_KV_LORA), lambda i: (i, OFF_CKV // MLA_KV_LORA)),
                  pl.BlockSpec((tm, LANES), lambda i: (i, OFF_MISC // LANES)),
                  pl.BlockSpec((tm, LANES), lambda i: (i, 0)),
                  pl.BlockSpec((tm, LANES), lambda i: (i, 0)),
                  full(qn), full(kvn), full(wq), full(wkv), full(vone)],
        out_specs=[ospec, ospec, ospec],
        out_shape=[out, out, out],
        compiler_params=_cparams(("arbitrary",)),
        name="mla_prep",
    )(proj, proj, proj, cos_t, sin_t, qn, kvn, wq, wkv, vone)


def _attn_body(q_ref, k_ref, v_ref, o_ref, *, tq):
    i = pl.program_id(2)
    lane = lax.broadcasted_iota(jnp.int32, (1, HEAD_PAD), 1)
    qpos = lax.broadcasted_iota(jnp.int32, (tq, tq), 0)
    kpos = lax.broadcasted_iota(jnp.int32, (tq, tq), 1)
    causal = kpos <= qpos
    heads = range(ATTN_HEADS)
    slices = [slice(h * HEAD_PAD, (h + 1) * HEAD_PAD) for h in heads]
    block = lambda j: pl.ds(pl.multiple_of(j * tq, tq), tq)

    def scores(h, j, masked, m_prev):
        s = _dot_nt(q_ref[:, slices[h]], k_ref[block(j), slices[h]])
        if masked:
            s = jnp.where(causal, s, -jnp.inf)
        m_new = jnp.maximum(m_prev, jnp.max(s, axis=-1, keepdims=True))
        return m_new, jnp.exp2(m_prev - m_new), jnp.exp2((s - m_new).astype(BF16))

    def step(j, state, masked):
        new = []
        for h in heads:
            m, acc = state[h]
            m, alpha, p = scores(h, j, masked, m)
            new.append((m, alpha * acc + _dot(p, v_ref[block(j), slices[h]])))
        return tuple(new)

    def finish(state):
        for pair in range(ATTN_HEADS // 2):
            acc0, acc1 = state[2 * pair][1], state[2 * pair + 1][1]
            den0 = jnp.sum(jnp.where(lane == MLA_V, acc0, 0.0), axis=-1, keepdims=True)
            den1 = jnp.sum(jnp.where(lane == 0, acc1, 0.0), axis=-1, keepdims=True)
            o_ref[:, pair * HEAD_PAD:(pair + 1) * HEAD_PAD] = jnp.where(
                lane < MLA_V, acc0 / den0, acc1 / den1).astype(o_ref.dtype)

    def two_steps(j2, state):
        return step(2 * j2 + 1, step(2 * j2, state, False), False)

    init = (jnp.full((tq, 1), -jnp.inf, F32), jnp.zeros((tq, HEAD_PAD), F32))
    state = lax.fori_loop(0, i // 2, two_steps, (init,) * ATTN_HEADS)

    @pl.when(i % 2 == 0)
    def _():
        finish(step(i, state, True))

    @pl.when(i % 2 == 1)
    def _():
        finish(step(i, step(i - 1, state, False), True))


def _attn(q, k, v, bsz, t, tq):
    m = q.shape[0]
    nq = t // tq
    pw = ATTN_HEADS * HEAD_PAD
    return pl.pallas_call(
        functools.partial(_attn_body, tq=tq),
        grid=(bsz, MLA_HEADS // ATTN_HEADS, nq),
        in_specs=[pl.BlockSpec((tq, pw), lambda b, h, i: (b * nq + i, h)),
                  pl.BlockSpec((t, pw), lambda b, h, i: (b, h)),
                  pl.BlockSpec((t, pw), lambda b, h, i: (b, h))],
        out_specs=pl.BlockSpec((tq, ATTN_HEADS * MLA_V), lambda b, h, i: (b * nq + i, h)),
        out_shape=jax.ShapeDtypeStruct((m, MLA_WIDTH), BF16),
        compiler_params=_cparams(("arbitrary", "arbitrary", "arbitrary")),
        name="mla_attn",
    )(q, k, v)


def _softplus(x):
    return jnp.maximum(x, 0.0) + jnp.log1p(jnp.exp(-jnp.abs(x)))


def _ssd_body(z_ref, dt_ref, xbc_ref, cw_ref, cb_ref, dtb_ref, alog_ref, dsk_ref, ng_ref, tril_ref, e8_ref,
              o_ref, xp_ref, xc_ref, s_ref, *, tq):
    L = SSD_CHUNK
    gn = SSM_GROUPS * SSM_STATE
    half = SSM_WIDTH // SSM_GROUPS
    t_idx = pl.program_id(1)

    @pl.when(t_idx == 0)
    def _():
        s_ref[...] = jnp.zeros_like(s_ref)
        xp_ref[0:8, :] = jnp.zeros((8, SSM_CONV_DIM), F32)

    @pl.when(t_idx > 0)
    def _():
        xp_ref[0:8, :] = xp_ref[tq:tq + 8, :]

    xp_ref[8:8 + tq, :] = xbc_ref[...].astype(F32)
    cw = cw_ref[...]
    acc = cb_ref[...] + cw[3:4, :] * xp_ref[8:8 + tq, :]
    for w in range(SSM_CONV - 1):
        acc = acc + cw[w:w + 1, :] * xp_ref[5 + w:5 + w + tq, :]
    xc_ref[...] = acc * _sigmoid(acc)

    a_neg = -jnp.exp(alog_ref[...])
    dtb = dtb_ref[...]
    dsk = dsk_ref[...]
    ng = ng_ref[...]
    tril = tril_ref[...]
    e8 = e8_ref[...]
    ri = lax.broadcasted_iota(jnp.int32, (L, L), 0)
    ci = lax.broadcasted_iota(jnp.int32, (L, L), 1)
    tri = ci <= ri
    lane_gn = lax.broadcasted_iota(jnp.int32, (1, gn), 1)
    lane_w = lax.broadcasted_iota(jnp.int32, (1, SSM_WIDTH), 1)
    row_gn = lax.broadcasted_iota(jnp.int32, (gn, SSM_WIDTH), 0)
    col_w = lax.broadcasted_iota(jnp.int32, (gn, SSM_WIDTH), 1)
    blockdiag = (row_gn // SSM_STATE) == (col_w // half)
    head_mask = [(lane_w // SSM_HEAD_DIM) == h for h in range(SSM_HEADS)]

    def chunk(c, carry):
        rows = pl.ds(pl.multiple_of(c * L, L), L)
        xc = xc_ref[rows, :]
        xs = xc[:, :SSM_WIDTH]
        bm = xc[:, SSM_WIDTH:SSM_WIDTH + gn]
        cm = xc[:, SSM_WIDTH + gn:]
        dt = _softplus(dt_ref[rows, :].astype(F32) + dtb)
        a = dt * a_neg
        acum = _sel_rows(tril, a)
        dt_e = _sel_cols(dt, e8)
        ac_e = _sel_cols(acum, e8)
        x_dt = xs * dt_e
        last = ac_e[L - 1:L, :]
        x_dec = (x_dt * jnp.exp(last - ac_e)).astype(BF16)
        x_b = x_dt.astype(BF16)
        bm_t = bm.T.astype(BF16)
        cb0 = _dot(jnp.where(lane_gn < SSM_STATE, cm, 0.0).astype(BF16), bm_t)
        cb1 = _dot(jnp.where(lane_gn >= SSM_STATE, cm, 0.0).astype(BF16), bm_t)
        ac_t = acum.T
        ws, xblk = [], []
        for h in range(SSM_HEADS):
            hl = DT_LANE + h
            seg = acum[:, hl:hl + 1] - ac_t[hl:hl + 1, :]
            lmat = jnp.exp(jnp.where(tri, seg, -jnp.inf))
            ws.append((lmat * (cb0 if h < SSM_HEADS // SSM_GROUPS else cb1)).astype(BF16))
            xblk.append(jnp.where(head_mask[h], x_b, jnp.zeros_like(x_b)))
        y = _dot(jnp.concatenate(ws, axis=1), jnp.concatenate(xblk, axis=0))
        st = s_ref[...]
        y = y + jnp.exp(ac_e) * _dot(cm.astype(BF16), st.astype(BF16))
        s_ref[...] = jnp.exp(last) * st + jnp.where(blockdiag, _dot(bm_t, x_dec), 0.0)
        y = y + xs * dsk
        z = z_ref[rows, :].astype(F32)
        y = y * (z * _sigmoid(z))
        y0 = _rms(y[:, :half], ng[:, :half])
        y1 = _rms(y[:, half:], ng[:, half:])
        o_ref[rows, :half] = y0.astype(o_ref.dtype)
        o_ref[rows, half:] = y1.astype(o_ref.dtype)
        return carry

    lax.fori_loop(0, tq // L, chunk, 0, unroll=4)


def _ssd(proj, cw, cb, dtb, alog, dsk, ng, bsz, t, tq):
    m = proj.shape[0]
    nq = t // tq
    L = SSD_CHUNK
    tril = jnp.asarray(np.tile(np.tril(np.ones((L, L), np.float32)), (1, N_SPLIT)), BF16)
    e8 = np.zeros((LANES, SSM_WIDTH), np.float32)
    for h in range(SSM_HEADS):
        e8[DT_LANE + h, h * SSM_HEAD_DIM:(h + 1) * SSM_HEAD_DIM] = 1.0
    e8 = jnp.asarray(np.tile(e8, (N_SPLIT, 1)), BF16)
    full = lambda a: pl.BlockSpec(a.shape, lambda b, i: (0, 0))
    return pl.pallas_call(
        functools.partial(_ssd_body, tq=tq),
        grid=(bsz, nq),
        in_specs=[pl.BlockSpec((tq, SSM_WIDTH), lambda b, i: (b * nq + i, OFF_SZ // SSM_WIDTH)),
                  pl.BlockSpec((tq, LANES), lambda b, i: (b * nq + i, OFF_MISC // LANES)),
                  pl.BlockSpec((tq, SSM_CONV_DIM), lambda b, i: (b * nq + i, OFF_SXBC // SSM_CONV_DIM)),
                  full(cw), full(cb), full(dtb), full(alog), full(dsk), full(ng), full(tril), full(e8)],
        out_specs=pl.BlockSpec((tq, SSM_WIDTH), lambda b, i: (b * nq + i, 0)),
        out_shape=jax.ShapeDtypeStruct((m, SSM_WIDTH), BF16),
        scratch_shapes=[pltpu.VMEM((tq + 8, SSM_CONV_DIM), F32),
                        pltpu.VMEM((tq, SSM_CONV_DIM), F32),
                        pltpu.VMEM((SSM_GROUPS * SSM_STATE, SSM_WIDTH), F32)],
        compiler_params=_cparams(("arbitrary", "arbitrary")),
        name="ssd",
    )(proj, proj, proj, cw, cb, dtb, alog, dsk, ng, tril, e8)


def _merge_body(ya_ref, yb_ref, yc_ref, x_ref, g_ref, wg_ref, wa_ref, wb_ref, wc_ref, wo_ref, o_ref):
    d = D_MODEL
    x = x_ref[...]
    h = _rms(x, g_ref[...]).astype(BF16)
    gate = lambda b: _sigmoid(_dot(h, wg_ref[:, b * d:(b + 1) * d]))
    merged = (gate(0) * _dot(ya_ref[...], wa_ref[...])
              + gate(1) * _dot(yb_ref[...], wb_ref[...])
              + gate(2) * _dot(yc_ref[...], wc_ref[...]))
    o_ref[...] = x + _dot(merged.astype(BF16), wo_ref[...])


def _merge(ya, yb, yc, x2, g, wg, wa, wb, wc, wo, tm):
    m, d = x2.shape
    full = lambda a: pl.BlockSpec(a.shape, lambda i: (0, 0))
    row = lambda w: pl.BlockSpec((tm, w), lambda i: (i, 0))
    return pl.pallas_call(
        _merge_body,
        grid=(m // tm,),
        in_specs=[row(HG_WIDTH), row(MLA_WIDTH), row(SSM_WIDTH), row(d),
                  full(g), full(wg), full(wa), full(wb), full(wc), full(wo)],
        out_specs=row(d),
        out_shape=jax.ShapeDtypeStruct((m, d), F32),
        compiler_params=_cparams(("arbitrary",)),
        name="merge_out",
    )(ya, yb, yc, x2, g, wg, wa, wb, wc, wo)


def _mlp_body(x_ref, g_ref, wu_ref, wd_ref, fg_ref, o_ref, *, last_layer):
    x = x_ref[...]
    h = _rms(x, g_ref[...]).astype(BF16)
    u = jnp.maximum(_dot(h, wu_ref[...]), 0.0)
    y = x + _dot((u * u).astype(BF16), wd_ref[...])
    o_ref[...] = _rms(y, fg_ref[...]) if last_layer else y


def _mlp(x2, g, wu, wd, final_g, tm, last_layer):
    m, d = x2.shape
    const = lambda a: pl.BlockSpec(a.shape, lambda i: (0, 0), pipeline_mode=pl.Buffered(1))
    return pl.pallas_call(
        functools.partial(_mlp_body, last_layer=last_layer),
        grid=(m // tm,),
        in_specs=[pl.BlockSpec((tm, d), lambda i: (i, 0)), const(g), const(wu), const(wd), const(final_g)],
        out_specs=pl.BlockSpec((tm, d), lambda i: (i, 0)),
        out_shape=jax.ShapeDtypeStruct((m, d), F32),
        compiler_params=_cparams(("arbitrary",)),
        name="mlp",
    )(x2, g, wu, wd, final_g)


def _rot_cols(w):
    hr = MLA_ROPE // 2
    return jnp.concatenate([-w[..., hr:], w[..., :hr]], axis=-1)


def _pack_w_in(w_in):
    dep, d, _ = w_in.shape
    sizes = (512, 512, 512, 512, MLA_Q_LORA, MLA_KV_LORA, MLA_ROPE, SSM_WIDTH, SSM_CONV_DIM, SSM_HEADS)
    assert sum(sizes) == MIXER_IN_COLS
    pts = np.cumsum(sizes)[:-1].tolist()
    mixers = w_in[..., :MIXER_IN_COLS].astype(BF16)
    hq, hf, hi, hg, cq, ckv, kr, sz, sxbc, sdt = jnp.split(mixers, pts, axis=-1)
    z = lambda n: jnp.zeros((dep, d, n), BF16)
    misc = jnp.concatenate([_rot_cols(kr), sdt, z(32 - SSM_HEADS), kr, z(32)], axis=-1)
    packed = jnp.concatenate([hq, hf, hi, hg, sz, misc, cq, sxbc, ckv], axis=-1)
    assert packed.shape[-1] == PROJ_COLS
    return packed, w_in[..., MIXER_IN_COLS:].astype(BF16)


def _pack_w_uq(w):
    dep, r, _ = w.shape
    wh = w.reshape(dep, r, MLA_HEADS, MLA_NOPE + MLA_ROPE)
    nope, rope = wh[..., :MLA_NOPE], wh[..., MLA_NOPE:]
    z = lambda n: jnp.zeros((dep, r, MLA_HEADS, n), w.dtype)
    a = jnp.concatenate([nope, rope, z(32)], axis=-1).reshape(dep, r, MLA_HEADS * HEAD_PAD)
    b = jnp.concatenate([z(64), _rot_cols(rope), z(32)], axis=-1).reshape(dep, r, MLA_HEADS * HEAD_PAD)
    return jnp.concatenate([a, b], axis=-1).astype(BF16)


def _pack_w_ukv(w):
    dep, r, _ = w.shape
    wh = w.reshape(dep, r, MLA_HEADS // 2, 2, MLA_NOPE + MLA_V)
    kn, v = wh[..., :MLA_NOPE], wh[..., MLA_NOPE:]
    z = jnp.zeros_like(kn)
    ka = jnp.concatenate([kn, z], axis=-1).reshape(dep, r, MLA_HEADS * HEAD_PAD)
    va = jnp.stack([jnp.concatenate([v[..., 0, :], z[..., 0, :]], axis=-1),
                    jnp.concatenate([z[..., 1, :], v[..., 1, :]], axis=-1)], axis=-2)
    va = va.reshape(dep, r, MLA_HEADS * HEAD_PAD)
    return jnp.concatenate([ka, va], axis=-1).astype(BF16)


def _v_ones():
    v = np.zeros((1, MLA_HEADS * HEAD_PAD), np.float32)
    for h in range(MLA_HEADS):
        v[0, h * HEAD_PAD + (MLA_V if h % 2 == 0 else 0)] = 1.0
    return jnp.asarray(v)


def _rope_tables(positions):
    inv = ROPE_THETA ** (-jnp.arange(0, MLA_ROPE, 2, dtype=F32) / MLA_ROPE)
    ang = positions.astype(F32).reshape(-1, 1) * inv
    z = lambda n: jnp.zeros((ang.shape[0], n), F32)
    cos, sin = jnp.cos(ang), jnp.sin(ang)
    return (jnp.concatenate([z(64), cos, cos, z(32)], axis=-1),
            jnp.concatenate([z(64), sin, sin, z(32)], axis=-1))


def _row_tile(n, want):
    t = min(n, want)
    assert n % t == 0, (n, t)
    return t


def kernel(x, positions, mix_norm, w_in, hg_lb_logits, hg_norm, mla_q_norm, mla_kv_norm, mla_w_uq, mla_w_ukv,
           ssm_conv_w, ssm_conv_b, ssm_dt_bias, ssm_a_log, ssm_d, ssm_norm, w_br_hg, w_br_mla, w_br_ssm,
           w_out, mlp_norm, w_up, w_down, final_norm):
    bsz, t, d = x.shape
    depth = w_in.shape[0]
    assert d == D_MODEL and t % SSD_CHUNK == 0
    m = bsz * t
    tm = _row_tile(m, 512)
    tm_in = _row_tile(m, 1024)
    tq = _row_tile(t, 512)

    lbs = jnp.cumsum(jax.nn.softmax(hg_lb_logits.astype(F32), axis=0), axis=0)
    lbs = lbs - lbs[0:1]
    w_in_p, w_gate = _pack_w_in(w_in)
    wq_p = _pack_w_uq(mla_w_uq)
    wkv_p = _pack_w_ukv(mla_w_ukv)
    vone = _v_ones()
    cos_t, sin_t = _rope_tables(positions)
    pad_h = lambda a: jnp.pad(a.astype(F32), ((0, 0), (DT_LANE, LANES - DT_LANE - SSM_HEADS)))
    dtb_p, alog_p = pad_h(ssm_dt_bias), pad_h(ssm_a_log)
    dsk_e = jnp.repeat(ssm_d.astype(F32), SSM_HEAD_DIM, axis=-1)
    bf = lambda a: a.astype(BF16)
    wa, wb, wc, wo, wu, wd = bf(w_br_hg), bf(w_br_mla), bf(w_br_ssm), bf(w_out), bf(w_up), bf(w_down)
    r1 = lambda a: a.reshape(1, -1).astype(F32)

    x2 = x.reshape(m, d)
    for l in range(depth):
        proj = _inproj(x2, r1(mix_norm[l]), w_in_p[l], tm_in, PROJ_COLS // 2)
        ya = _hgrn(proj, r1(lbs[l]), r1(hg_norm[l]), bsz, t, tq)
        q, k, v = _mla_prep(proj, cos_t, sin_t, r1(mla_q_norm[l]), r1(mla_kv_norm[l]), wq_p[l], wkv_p[l],
                            vone, tm)
        yb = _attn(q, k, v, bsz, t, tq)
        yc = _ssd(proj, ssm_conv_w[l].astype(F32), r1(ssm_conv_b[l]), r1(dtb_p[l]), r1(alog_p[l]),
                  r1(dsk_e[l]), r1(ssm_norm[l]), bsz, t, tq)
        x2 = _merge(ya, yb, yc, x2, r1(mix_norm[l]), w_gate[l], wa[l], wb[l], wc[l], wo[l], tm)
        x2 = _mlp(x2, r1(mlp_norm[l]), wu[l], wd[l], r1(final_norm), tm, last_layer=(l == depth - 1))
    return x2.reshape(bsz, t, d)
```

```python
import functools
import math

import numpy as np
import jax
import jax.numpy as jnp
from jax import lax
from jax.experimental import pallas as pl
from jax.experimental.pallas import tpu as pltpu

F32 = jnp.float32
BF16 = jnp.bfloat16

D_MODEL = 1024
EPS = 1e-6
HG_HEADS = 4
HG_DK = 128
HG_DV = 128
HG_WIDTH = HG_HEADS * HG_DV
MLA_HEADS = 8
MLA_NOPE = 64
MLA_ROPE = 32
MLA_V = 64
MLA_Q_LORA = 384
MLA_KV_LORA = 256
MLA_WIDTH = MLA_HEADS * MLA_V
ROPE_THETA = 10000.0
SSM_HEADS = 8
SSM_HEAD_DIM = 64
SSM_WIDTH = SSM_HEADS * SSM_HEAD_DIM
SSM_GROUPS = 2
SSM_STATE = 64
SSM_CONV = 4
SSM_CONV_DIM = SSM_WIDTH + 2 * SSM_GROUPS * SSM_STATE
D_FF = 4 * D_MODEL
N_BRANCH = 3

LANES = 128
HEAD_PAD = 128

OFF_HQ, OFF_HF, OFF_HI, OFF_HG = 0, 512, 1024, 1536
OFF_SZ = 2048
OFF_MISC = 2560
DT_LANE = 32
OFF_CQ = 2688
OFF_SXBC = 3072
OFF_CKV = 3840
PROJ_COLS = 4096
MIXER_IN_COLS = 4008

HG_CHUNK = 64
HG_LEVELS = 6
SSD_CHUNK = 128
ATTN_HEADS = 4

VMEM_LIMIT = 48 * 1024 * 1024


def _cparams(sem, flags=None):
    return pltpu.CompilerParams(dimension_semantics=sem, vmem_limit_bytes=VMEM_LIMIT, flags=flags)


def _sigmoid(x):
    return jax.nn.sigmoid(x)


def _split3(x):
    hi = x.astype(BF16)
    r1 = x - hi.astype(F32)
    mid = r1.astype(BF16)
    r2 = r1 - mid.astype(F32)
    return hi, mid, r2.astype(BF16)


def _dot(a, b):
    return jnp.dot(a, b, preferred_element_type=F32)


def _dot_nt(a, b):
    return lax.dot_general(a, b, (((1,), (1,)), ((), ())), preferred_element_type=F32)


N_SPLIT = 3


def _sel_rows(sel3, x):
    return _dot(sel3, jnp.concatenate(_split3(x), axis=0))


def _sel_cols(x, sel3):
    return _dot(jnp.concatenate(_split3(x), axis=1), sel3)


def _rms(x, g):
    ms = jnp.mean(x * x, axis=-1, keepdims=True)
    return x * lax.rsqrt(ms + EPS) * g


def _inproj_body(x_ref, g_ref, w_ref, o_ref):
    h = _rms(x_ref[...], g_ref[...]).astype(BF16)
    o_ref[...] = _dot(h, w_ref[...]).astype(o_ref.dtype)


def _inproj(x2, g, w, tm, tn):
    m, d = x2.shape
    n = w.shape[1]
    return pl.pallas_call(
        _inproj_body,
        grid=(n // tn, m // tm),
        in_specs=[pl.BlockSpec((tm, d), lambda j, i: (i, 0)),
                  pl.BlockSpec((1, d), lambda j, i: (0, 0)),
                  pl.BlockSpec((d, tn), lambda j, i: (0, j))],
        out_specs=pl.BlockSpec((tm, tn), lambda j, i: (i, j)),
        out_shape=jax.ShapeDtypeStruct((m, n), BF16),
        compiler_params=_cparams(("arbitrary", "arbitrary")),
        name="inproj",
    )(x2, g, w)


def _hgrn_level_matrix():
    c = HG_CHUNK
    mat = np.zeros((8, c, c), np.float32)
    for i in range(c):
        mat[0, i, : i + 1] = 1.0
        mat[7, i, i + 1:] = 1.0
    for lvl in range(HG_LEVELS):
        m = 1 << lvl
        for r in range(c):
            start = (r // (2 * m)) * 2 * m
            mid = start + m
            if r >= mid:
                mat[lvl + 1, r, mid: r + 1] = 1.0
            else:
                mat[lvl + 1, r, r + 1: mid] = 1.0
    return mat.reshape(8 * c, c)


def _hgrn_pair_masks():
    n = 2 * HG_CHUNK
    row, col = np.meshgrid(np.arange(n), np.arange(n), indexing="ij")
    out = [row == col]
    for lvl in range(HG_LEVELS):
        same_block = (row >> (lvl + 1)) == (col >> (lvl + 1))
        out.append(same_block & (((row >> lvl) & 1) == 1) & (((col >> lvl) & 1) == 0))
    return np.stack(out).astype(np.float32)


def _hgrn_body(q_ref, f_ref, i_ref, g_ref, lb_ref, ng_ref, m_ref, mask_ref, o_ref, s_ref, *, n_chunks):
    c = HG_CHUNK

    @pl.when(pl.program_id(1) == 0)
    def _():
        s_ref[...] = jnp.zeros_like(s_ref)

    lb = lb_ref[...]
    log_lb = jnp.log(lb)
    log1m_lb = jnp.log1p(-lb)
    one_m_lb = 1.0 - lb
    ng = ng_ref[...]
    sel = m_ref[...]

    rowv = lax.broadcasted_iota(jnp.int32, (2 * c, HG_DK), 0)
    second_half = [((rowv >> lvl) & 1) == 1 for lvl in range(HG_LEVELS)]

    def chunk(ci, carry):
        r0 = pl.multiple_of(ci * c, c)
        rows = pl.ds(r0, c)
        hq = q_ref[rows, :].astype(F32)
        ff = f_ref[rows, :].astype(F32)
        hv = i_ref[rows, :].astype(F32)
        hg = g_ref[rows, :].astype(F32)
        q = hq * _sigmoid(hq)
        e_f = jnp.exp(-jnp.abs(ff))
        one_p = 1.0 + e_f
        log_sig = jnp.minimum(ff, 0.0) - jnp.log(one_p)
        b = log1m_lb + log_sig
        lf = jnp.maximum(log_lb, b) + jnp.log(1.0 + jnp.exp(-jnp.abs(log_lb - b)))
        k = one_m_lb * jnp.where(ff >= 0.0, e_f, 1.0) / one_p
        gate = hg * _sigmoid(hg)
        ex = _sel_rows(sel, lf)

        for pair in range(HG_HEADS // 2):
            sls = [slice(h * HG_DK, (h + 1) * HG_DK) for h in (2 * pair, 2 * pair + 1)]
            stack = lambda a, lo=0, hi=c: jnp.concatenate([a[lo:hi, sl] for sl in sls], axis=0)
            level = lambda n: stack(ex, n * c, (n + 1) * c)
            qp, kp, vp = stack(q).astype(BF16), stack(k).astype(BF16), stack(hv)
            decay_of = lambda n: jnp.exp(level(n).astype(BF16))
            att = mask_ref[0] * _dot(qp, stack(k).T.astype(BF16))
            for lvl in range(HG_LEVELS):
                xl = jnp.where(second_half[lvl], qp, kp) * decay_of(lvl + 1)
                att = att + mask_ref[lvl + 1] * _dot(xl, xl.astype(F32).T.astype(BF16))
            o_intra = _dot(att.astype(BF16), vp.astype(BF16))
            e0 = level(0)
            q_in = qp * jnp.exp(e0.astype(BF16))
            k_out = kp * decay_of(7)
            for idx, sl in enumerate(sls):
                h = 2 * pair + idx
                r = slice(idx * c, (idx + 1) * c)
                st = s_ref[h]
                o = o_intra[r] + _dot(q_in[r], st.T.astype(BF16))
                decay = jnp.exp(e0[(idx + 1) * c - 1:(idx + 1) * c, :])
                s_ref[h] = st * decay + _dot(vp[r].T.astype(BF16), k_out[r])
                y = _rms(o, ng[:, sl]) * gate[:, sl]
                o_ref[rows, sl] = y.astype(o_ref.dtype)
        return carry

    lax.fori_loop(0, n_chunks, chunk, 0, unroll=8)


def _hgrn(proj, lb, ng, bsz, t, tq):
    m = proj.shape[0]
    nq = t // tq
    sel = jnp.asarray(np.tile(_hgrn_level_matrix(), (1, N_SPLIT)), BF16)
    masks = jnp.asarray(_hgrn_pair_masks())
    w = HG_WIDTH

    def col_spec(off):
        blk = off // w
        return pl.BlockSpec((tq, w), lambda b, i: (b * nq + i, blk))

    vec = pl.BlockSpec((1, w), lambda b, i: (0, 0))
    return pl.pallas_call(
        functools.partial(_hgrn_body, n_chunks=tq // HG_CHUNK),
        grid=(bsz, nq),
        in_specs=[col_spec(OFF_HQ), col_spec(OFF_HF), col_spec(OFF_HI), col_spec(OFF_HG), vec, vec,
                  pl.BlockSpec(sel.shape, lambda b, i: (0, 0)),
                  pl.BlockSpec(masks.shape, lambda b, i: (0, 0, 0))],
        out_specs=pl.BlockSpec((tq, w), lambda b, i: (b * nq + i, 0)),
        out_shape=jax.ShapeDtypeStruct((m, w), BF16),
        scratch_shapes=[pltpu.VMEM((HG_HEADS, HG_DV, HG_DK), F32)],
        compiler_params=_cparams(("arbitrary", "arbitrary")),
        name="hgrn2",
    )(proj, proj, proj, proj, lb, ng, sel, masks)


def _mla_prep_body(cq_ref, ckv_ref, kr_ref, cos_ref, sin_ref, qn_ref, kvn_ref, wq_ref, wkv_ref, vone_ref,
                   q_out, k_out, v_out):
    scale = (MLA_NOPE + MLA_ROPE) ** -0.5 * math.log2(math.e)
    cqn = _rms(cq_ref[...].astype(F32), qn_ref[...]).astype(BF16)
    ckvn = _rms(ckv_ref[...].astype(F32), kvn_ref[...]).astype(BF16)
    qq = _dot(cqn, wq_ref[...])
    kv = _dot(ckvn, wkv_ref[...])
    cos = cos_ref[...]
    sin = sin_ref[...]
    lane = lax.broadcasted_iota(jnp.int32, (1, HEAD_PAD), 1)
    cq_t = scale * (cos + (lane < MLA_NOPE).astype(F32))
    sq_t = scale * sin
    krb = kr_ref[...].astype(F32)
    k_rope = krb * cos + pltpu.roll(krb, 64, 1) * sin
    hw = MLA_HEADS * HEAD_PAD
    vone = vone_ref[...]
    for h in range(MLA_HEADS):
        sl = slice(h * HEAD_PAD, (h + 1) * HEAD_PAD)
        sl2 = slice(hw + h * HEAD_PAD, hw + (h + 1) * HEAD_PAD)
        q_out[:, sl] = (qq[:, sl] * cq_t + qq[:, sl2] * sq_t).astype(BF16)
        k_out[:, sl] = (kv[:, sl] + k_rope).astype(BF16)
        v_out[:, sl] = (kv[:, sl2] + vone[:, sl]).astype(BF16)


def _mla_prep(proj, cos_t, sin_t, qn, kvn, wq, wkv, vone, tm):
    m = proj.shape[0]
    hw = MLA_HEADS * HEAD_PAD
    full = lambda a: pl.BlockSpec(a.shape, lambda i: (0, 0))
    out = jax.ShapeDtypeStruct((m, hw), BF16)
    ospec = pl.BlockSpec((tm, hw), lambda i: (i, 0))
    return pl.pallas_call(
        _mla_prep_body,
        grid=(m // tm,),
        in_specs=[pl.BlockSpec((tm, MLA_Q_LORA), lambda i: (i, OFF_CQ // MLA_Q_LORA)),
                  pl.BlockSpec((tm, MLA_KV_LORA), lambda i: (i, OFF_CKV // MLA_KV_LORA)),
                  pl.BlockSpec((tm, LANES), lambda i: (i, OFF_MISC // LANES)),
                  pl.BlockSpec((tm, LANES), lambda i: (i, 0)),
                  pl.BlockSpec((tm, LANES), lambda i: (i, 0)),
                  full(qn), full(kvn), full(wq), full(wkv), full(vone)],
        out_specs=[ospec, ospec, ospec],
        out_shape=[out, out, out],
        compiler_params=_cparams(("arbitrary",)),
        name="mla_prep",
    )(proj, proj, proj, cos_t, sin_t, qn, kvn, wq, wkv, vone)


def _attn_body(q_ref, k_ref, v_ref, o_ref, *, tq, tk):
    i = pl.program_id(2)
    lane = lax.broadcasted_iota(jnp.int32, (1, HEAD_PAD), 1)
    qpos = lax.broadcasted_iota(jnp.int32, (tq, tk), 0)
    kpos = lax.broadcasted_iota(jnp.int32, (tq, tk), 1)
    heads = range(ATTN_HEADS)
    slices = [slice(h * HEAD_PAD, (h + 1) * HEAD_PAD) for h in heads]
    block = lambda j: pl.ds(pl.multiple_of(j * tk, tk), tk)

    def scores(h, j, diag_block, m_prev):
        s = _dot_nt(q_ref[:, slices[h]], k_ref[block(j), slices[h]])
        if diag_block is not None:
            s = jnp.where(kpos + diag_block * tk <= qpos, s, -jnp.inf)
        m_new = jnp.maximum(m_prev, jnp.max(s, axis=-1, keepdims=True))
        return m_new, jnp.exp2(m_prev - m_new), jnp.exp2((s - m_new).astype(BF16))

    def step(j, state, diag_block=None):
        new = []
        for h in heads:
            m, acc = state[h]
            m, alpha, p = scores(h, j, diag_block, m)
            new.append((m, alpha * acc + _dot(p, v_ref[block(j), slices[h]])))
        return tuple(new)

    def finish(state):
        for pair in range(ATTN_HEADS // 2):
            acc0, acc1 = state[2 * pair][1], state[2 * pair + 1][1]
            den0 = jnp.sum(jnp.where(lane == MLA_V, acc0, 0.0), axis=-1, keepdims=True)
            den1 = jnp.sum(jnp.where(lane == 0, acc1, 0.0), axis=-1, keepdims=True)
            o_ref[:, pair * HEAD_PAD:(pair + 1) * HEAD_PAD] = jnp.where(
                lane < MLA_V, acc0 / den0, acc1 / den1).astype(o_ref.dtype)

    def two_steps(j2, state):
        return step(2 * j2 + 1, step(2 * j2, state))

    init = (jnp.full((tq, 1), -jnp.inf, F32), jnp.zeros((tq, HEAD_PAD), F32))
    state = lax.fori_loop(0, i, two_steps, (init,) * ATTN_HEADS)
    finish(step(2 * i + 1, step(2 * i, state, 0), 1))


def _attn(q, k, v, bsz, t, tq):
    m = q.shape[0]
    nq = t // tq
    pw = ATTN_HEADS * HEAD_PAD
    return pl.pallas_call(
        functools.partial(_attn_body, tq=tq, tk=tq // 2),
        grid=(bsz, MLA_HEADS // ATTN_HEADS, nq),
        in_specs=[pl.BlockSpec((tq, pw), lambda b, h, i: (b * nq + i, h)),
                  pl.BlockSpec((t, pw), lambda b, h, i: (b, h)),
                  pl.BlockSpec((t, pw), lambda b, h, i: (b, h))],
        out_specs=pl.BlockSpec((tq, ATTN_HEADS * MLA_V), lambda b, h, i: (b * nq + i, h)),
        out_shape=jax.ShapeDtypeStruct((m, MLA_WIDTH), BF16),
        compiler_params=_cparams(("arbitrary", "arbitrary", "arbitrary")),
        name="mla_attn",
    )(q, k, v)


def _softplus(x):
    return jnp.maximum(x, 0.0) + jnp.log1p(jnp.exp(-jnp.abs(x)))


def _ssd_body(z_ref, dt_ref, xbc_ref, cw_ref, cb_ref, dtb_ref, alog_ref, dsk_ref, ng_ref, tril_ref, e8_ref,
              o_ref, xp_ref, xc_ref, s_ref, *, tq):
    L = SSD_CHUNK
    gn = SSM_GROUPS * SSM_STATE
    half = SSM_WIDTH // SSM_GROUPS
    t_idx = pl.program_id(1)

    @pl.when(t_idx == 0)
    def _():
        s_ref[...] = jnp.zeros_like(s_ref)
        xp_ref[0:8, :] = jnp.zeros((8, SSM_CONV_DIM), F32)

    @pl.when(t_idx > 0)
    def _():
        xp_ref[0:8, :] = xp_ref[tq:tq + 8, :]

    xp_ref[8:8 + tq, :] = xbc_ref[...].astype(F32)
    cw = cw_ref[...]
    acc = cb_ref[...] + cw[3:4, :] * xp_ref[8:8 + tq, :]
    for w in range(SSM_CONV - 1):
        acc = acc + cw[w:w + 1, :] * xp_ref[5 + w:5 + w + tq, :]
    xc_ref[...] = acc * _sigmoid(acc)

    a_neg = -jnp.exp(alog_ref[...])
    dtb = dtb_ref[...]
    dsk = dsk_ref[...]
    ng = ng_ref[...]
    tril = tril_ref[...]
    e8 = e8_ref[...]
    ri = lax.broadcasted_iota(jnp.int32, (L, L), 0)
    ci = lax.broadcasted_iota(jnp.int32, (L, L), 1)
    tri = ci <= ri
    lane_gn = lax.broadcasted_iota(jnp.int32, (1, gn), 1)
    lane_w = lax.broadcasted_iota(jnp.int32, (1, SSM_WIDTH), 1)
    row_gn = lax.broadcasted_iota(jnp.int32, (gn, SSM_WIDTH), 0)
    col_w = lax.broadcasted_iota(jnp.int32, (gn, SSM_WIDTH), 1)
    blockdiag = (row_gn // SSM_STATE) == (col_w // half)
    head_mask = [(lane_w // SSM_HEAD_DIM) == h for h in range(SSM_HEADS)]

    def chunk(c, carry):
        rows = pl.ds(pl.multiple_of(c * L, L), L)
        xc = xc_ref[rows, :]
        xs = xc[:, :SSM_WIDTH]
        bm = xc[:, SSM_WIDTH:SSM_WIDTH + gn]
        cm = xc[:, SSM_WIDTH + gn:]
        dt = _softplus(dt_ref[rows, :].astype(F32) + dtb)
        a = dt * a_neg
        acum = _sel_rows(tril, a)
        dt_e = _sel_cols(dt, e8)
        ac_e = _sel_cols(acum, e8)
        x_dt = xs * dt_e
        last = ac_e[L - 1:L, :]
        x_dec = (x_dt * jnp.exp(last - ac_e)).astype(BF16)
        x_b = x_dt.astype(BF16)
        bm_t = bm.T.astype(BF16)
        cb0 = _dot(jnp.where(lane_gn < SSM_STATE, cm, 0.0).astype(BF16), bm_t)
        cb1 = _dot(jnp.where(lane_gn >= SSM_STATE, cm, 0.0).astype(BF16), bm_t)
        ac_t = acum.T
        ws, xblk = [], []
        for h in range(SSM_HEADS):
            hl = DT_LANE + h
            seg = acum[:, hl:hl + 1] - ac_t[hl:hl + 1, :]
            lmat = jnp.exp(jnp.where(tri, seg, -jnp.inf))
            ws.append((lmat * (cb0 if h < SSM_HEADS // SSM_GROUPS else cb1)).astype(BF16))
            xblk.append(jnp.where(head_mask[h], x_b, jnp.zeros_like(x_b)))
        y = _dot(jnp.concatenate(ws, axis=1), jnp.concatenate(xblk, axis=0))
        st = s_ref[...]
        y = y + jnp.exp(ac_e) * _dot(cm.astype(BF16), st.astype(BF16))
        s_ref[...] = jnp.exp(last) * st + jnp.where(blockdiag, _dot(bm_t, x_dec), 0.0)
        y = y + xs * dsk
        z = z_ref[rows, :].astype(F32)
        y = y * (z * _sigmoid(z))
        y0 = _rms(y[:, :half], ng[:, :half])
        y1 = _rms(y[:, half:], ng[:, half:])
        o_ref[rows, :half] = y0.astype(o_ref.dtype)
        o_ref[rows, half:] = y1.astype(o_ref.dtype)
        return carry

    lax.fori_loop(0, tq // L, chunk, 0, unroll=4)


def _ssd(proj, cw, cb, dtb, alog, dsk, ng, bsz, t, tq):
    m = proj.shape[0]
    nq = t // tq
    L = SSD_CHUNK
    tril = jnp.asarray(np.tile(np.tril(np.ones((L, L), np.float32)), (1, N_SPLIT)), BF16)
    e8 = np.zeros((LANES, SSM_WIDTH), np.float32)
    for h in range(SSM_HEADS):
        e8[DT_LANE + h, h * SSM_HEAD_DIM:(h + 1) * SSM_HEAD_DIM] = 1.0
    e8 = jnp.asarray(np.tile(e8, (N_SPLIT, 1)), BF16)
    full = lambda a: pl.BlockSpec(a.shape, lambda b, i: (0, 0))
    return pl.pallas_call(
        functools.partial(_ssd_body, tq=tq),
        grid=(bsz, nq),
        in_specs=[pl.BlockSpec((tq, SSM_WIDTH), lambda b, i: (b * nq + i, OFF_SZ // SSM_WIDTH)),
                  pl.BlockSpec((tq, LANES), lambda b, i: (b * nq + i, OFF_MISC // LANES)),
                  pl.BlockSpec((tq, SSM_CONV_DIM), lambda b, i: (b * nq + i, OFF_SXBC // SSM_CONV_DIM)),
                  full(cw), full(cb), full(dtb), full(alog), full(dsk), full(ng), full(tril), full(e8)],
        out_specs=pl.BlockSpec((tq, SSM_WIDTH), lambda b, i: (b * nq + i, 0)),
        out_shape=jax.ShapeDtypeStruct((m, SSM_WIDTH), BF16),
        scratch_shapes=[pltpu.VMEM((tq + 8, SSM_CONV_DIM), F32),
                        pltpu.VMEM((tq, SSM_CONV_DIM), F32),
                        pltpu.VMEM((SSM_GROUPS * SSM_STATE, SSM_WIDTH), F32)],
        compiler_params=_cparams(("arbitrary", "arbitrary")),
        name="ssd",
    )(proj, proj, proj, cw, cb, dtb, alog, dsk, ng, tril, e8)


def _merge_body(ya_ref, yb_ref, yc_ref, x_ref, g_ref, wg_ref, wa_ref, wb_ref, wc_ref, wo_ref, o_ref):
    d = D_MODEL
    x = x_ref[...]
    h = _rms(x, g_ref[...]).astype(BF16)
    gate = lambda b: _sigmoid(_dot(h, wg_ref[:, b * d:(b + 1) * d]))
    merged = (gate(0) * _dot(ya_ref[...], wa_ref[...])
              + gate(1) * _dot(yb_ref[...], wb_ref[...])
              + gate(2) * _dot(yc_ref[...], wc_ref[...]))
    o_ref[...] = x + _dot(merged.astype(BF16), wo_ref[...])


def _merge(ya, yb, yc, x2, g, wg, wa, wb, wc, wo, tm):
    m, d = x2.shape
    full = lambda a: pl.BlockSpec(a.shape, lambda i: (0, 0))
    row = lambda w: pl.BlockSpec((tm, w), lambda i: (i, 0))
    return pl.pallas_call(
        _merge_body,
        grid=(m // tm,),
        in_specs=[row(HG_WIDTH), row(MLA_WIDTH), row(SSM_WIDTH), row(d),
                  full(g), full(wg), full(wa), full(wb), full(wc), full(wo)],
        out_specs=row(d),
        out_shape=jax.ShapeDtypeStruct((m, d), F32),
        compiler_params=_cparams(("arbitrary",)),
        name="merge_out",
    )(ya, yb, yc, x2, g, wg, wa, wb, wc, wo)


def _mlp_body(x_ref, g_ref, wu_ref, wd_ref, fg_ref, o_ref, *, last_layer):
    x = x_ref[...]
    h = _rms(x, g_ref[...]).astype(BF16)
    u = jnp.maximum(_dot(h, wu_ref[...]), 0.0)
    y = x + _dot((u * u).astype(BF16), wd_ref[...])
    o_ref[...] = _rms(y, fg_ref[...]) if last_layer else y


def _mlp(x2, g, wu, wd, final_g, tm, last_layer):
    m, d = x2.shape
    const = lambda a: pl.BlockSpec(a.shape, lambda i: (0, 0), pipeline_mode=pl.Buffered(1))
    return pl.pallas_call(
        functools.partial(_mlp_body, last_layer=last_layer),
        grid=(m // tm,),
        in_specs=[pl.BlockSpec((tm, d), lambda i: (i, 0)), const(g), const(wu), const(wd), const(final_g)],
        out_specs=pl.BlockSpec((tm, d), lambda i: (i, 0)),
        out_shape=jax.ShapeDtypeStruct((m, d), F32),
        compiler_params=_cparams(("arbitrary",)),
        name="mlp",
    )(x2, g, wu, wd, final_g)


def _rot_cols(w):
    hr = MLA_ROPE // 2
    return jnp.concatenate([-w[..., hr:], w[..., :hr]], axis=-1)


def _pack_w_in(w_in):
    dep, d, _ = w_in.shape
    sizes = (512, 512, 512, 512, MLA_Q_LORA, MLA_KV_LORA, MLA_ROPE, SSM_WIDTH, SSM_CONV_DIM, SSM_HEADS)
    assert sum(sizes) == MIXER_IN_COLS
    pts = np.cumsum(sizes)[:-1].tolist()
    mixers = w_in[..., :MIXER_IN_COLS].astype(BF16)
    hq, hf, hi, hg, cq, ckv, kr, sz, sxbc, sdt = jnp.split(mixers, pts, axis=-1)
    z = lambda n: jnp.zeros((dep, d, n), BF16)
    misc = jnp.concatenate([_rot_cols(kr), sdt, z(32 - SSM_HEADS), kr, z(32)], axis=-1)
    packed = jnp.concatenate([hq, hf, hi, hg, sz, misc, cq, sxbc, ckv], axis=-1)
    assert packed.shape[-1] == PROJ_COLS
    return packed, w_in[..., MIXER_IN_COLS:].astype(BF16)


def _pack_w_uq(w):
    dep, r, _ = w.shape
    wh = w.reshape(dep, r, MLA_HEADS, MLA_NOPE + MLA_ROPE)
    nope, rope = wh[..., :MLA_NOPE], wh[..., MLA_NOPE:]
    z = lambda n: jnp.zeros((dep, r, MLA_HEADS, n), w.dtype)
    a = jnp.concatenate([nope, rope, z(32)], axis=-1).reshape(dep, r, MLA_HEADS * HEAD_PAD)
    b = jnp.concatenate([z(64), _rot_cols(rope), z(32)], axis=-1).reshape(dep, r, MLA_HEADS * HEAD_PAD)
    return jnp.concatenate([a, b], axis=-1).astype(BF16)


def _pack_w_ukv(w):
    dep, r, _ = w.shape
    wh = w.reshape(dep, r, MLA_HEADS // 2, 2, MLA_NOPE + MLA_V)
    kn, v = wh[..., :MLA_NOPE], wh[..., MLA_NOPE:]
    z = jnp.zeros_like(kn)
    ka = jnp.concatenate([kn, z], axis=-1).reshape(dep, r, MLA_HEADS * HEAD_PAD)
    va = jnp.stack([jnp.concatenate([v[..., 0, :], z[..., 0, :]], axis=-1),
                    jnp.concatenate([z[..., 1, :], v[..., 1, :]], axis=-1)], axis=-2)
    va = va.reshape(dep, r, MLA_HEADS * HEAD_PAD)
    return jnp.concatenate([ka, va], axis=-1).astype(BF16)


def _v_ones():
    v = np.zeros((1, MLA_HEADS * HEAD_PAD), np.float32)
    for h in range(MLA_HEADS):
        v[0, h * HEAD_PAD + (MLA_V if h % 2 == 0 else 0)] = 1.0
    return jnp.asarray(v)


def _rope_tables(positions):
    inv = ROPE_THETA ** (-jnp.arange(0, MLA_ROPE, 2, dtype=F32) / MLA_ROPE)
    ang = positions.astype(F32).reshape(-1, 1) * inv
    z = lambda n: jnp.zeros((ang.shape[0], n), F32)
    cos, sin = jnp.cos(ang), jnp.sin(ang)
    return (jnp.concatenate([z(64), cos, cos, z(32)], axis=-1),
            jnp.concatenate([z(64), sin, sin, z(32)], axis=-1))


def _row_tile(n, want):
    t = min(n, want)
    assert n % t == 0, (n, t)
    return t


def kernel(x, positions, mix_norm, w_in, hg_lb_logits, hg_norm, mla_q_norm, mla_kv_norm, mla_w_uq, mla_w_ukv,
           ssm_conv_w, ssm_conv_b, ssm_dt_bias, ssm_a_log, ssm_d, ssm_norm, w_br_hg, w_br_mla, w_br_ssm,
           w_out, mlp_norm, w_up, w_down, final_norm):
    bsz, t, d = x.shape
    depth = w_in.shape[0]
    assert d == D_MODEL and t % SSD_CHUNK == 0
    m = bsz * t
    tm = _row_tile(m, 512)
    tm_in = _row_tile(m, 1024)
    tq = _row_tile(t, 512)
    tq_attn = _row_tile(t, 1024)

    lbs = jnp.cumsum(jax.nn.softmax(hg_lb_logits.astype(F32), axis=0), axis=0)
    lbs = lbs - lbs[0:1]
    w_in_p, w_gate = _pack_w_in(w_in)
    wq_p = _pack_w_uq(mla_w_uq)
    wkv_p = _pack_w_ukv(mla_w_ukv)
    vone = _v_ones()
    cos_t, sin_t = _rope_tables(positions)
    pad_h = lambda a: jnp.pad(a.astype(F32), ((0, 0), (DT_LANE, LANES - DT_LANE - SSM_HEADS)))
    dtb_p, alog_p = pad_h(ssm_dt_bias), pad_h(ssm_a_log)
    dsk_e = jnp.repeat(ssm_d.astype(F32), SSM_HEAD_DIM, axis=-1)
    bf = lambda a: a.astype(BF16)
    wa, wb, wc, wo, wu, wd = bf(w_br_hg), bf(w_br_mla), bf(w_br_ssm), bf(w_out), bf(w_up), bf(w_down)
    r1 = lambda a: a.reshape(1, -1).astype(F32)

    x2 = x.reshape(m, d)
    for l in range(depth):
        proj = _inproj(x2, r1(mix_norm[l]), w_in_p[l], tm_in, PROJ_COLS // 2)
        ya = _hgrn(proj, r1(lbs[l]), r1(hg_norm[l]), bsz, t, tq)
        q, k, v = _mla_prep(proj, cos_t, sin_t, r1(mla_q_norm[l]), r1(mla_kv_norm[l]), wq_p[l], wkv_p[l],
                            vone, tm)
        yb = _attn(q, k, v, bsz, t, tq_attn)
        yc = _ssd(proj, ssm_conv_w[l].astype(F32), r1(ssm_conv_b[l]), r1(dtb_p[l]), r1(alog_p[l]),
                  r1(dsk_e[l]), r1(ssm_norm[l]), bsz, t, tq)
        x2 = _merge(ya, yb, yc, x2, r1(mix_norm[l]), w_gate[l], wa[l], wb[l], wc[l], wo[l], tm)
        x2 = _mlp(x2, r1(mlp_norm[l]), wu[l], wd[l], r1(final_norm), tm, last_layer=(l == depth - 1))
    return x2.reshape(bsz, t, d)
```

```python
import functools
import math

import numpy as np
import jax
import jax.numpy as jnp
from jax import lax
from jax.experimental import pallas as pl
from jax.experimental.pallas import tpu as pltpu

F32 = jnp.float32
BF16 = jnp.bfloat16

D_MODEL = 1024
EPS = 1e-6
HG_HEADS = 4
HG_DK = 128
HG_DV = 128
HG_WIDTH = HG_HEADS * HG_DV
MLA_HEADS = 8
MLA_NOPE = 64
MLA_ROPE = 32
MLA_V = 64
MLA_Q_LORA = 384
MLA_KV_LORA = 256
MLA_WIDTH = MLA_HEADS * MLA_V
ROPE_THETA = 10000.0
SSM_HEADS = 8
SSM_HEAD_DIM = 64
SSM_WIDTH = SSM_HEADS * SSM_HEAD_DIM
SSM_GROUPS = 2
SSM_STATE = 64
SSM_CONV = 4
SSM_CONV_DIM = SSM_WIDTH + 2 * SSM_GROUPS * SSM_STATE
D_FF = 4 * D_MODEL
N_BRANCH = 3

LANES = 128
HEAD_PAD = 128

OFF_HQ, OFF_HF, OFF_HI, OFF_HG = 0, 512, 1024, 1536
OFF_SZ = 2048
OFF_MISC = 2560
DT_LANE = 32
OFF_CQ = 2688
OFF_SXBC = 3072
OFF_CKV = 3840
PROJ_COLS = 4096
MIXER_IN_COLS = 4008

HG_CHUNK = 64
HG_LEVELS = 6
SSD_CHUNK = 128
ATTN_HEADS = 4

VMEM_LIMIT = 48 * 1024 * 1024


def _cparams(sem, flags=None):
    return pltpu.CompilerParams(dimension_semantics=sem, vmem_limit_bytes=VMEM_LIMIT, flags=flags)


def _sigmoid(x):
    return jax.nn.sigmoid(x)


def _split3(x):
    hi = x.astype(BF16)
    r1 = x - hi.astype(F32)
    mid = r1.astype(BF16)
    r2 = r1 - mid.astype(F32)
    return hi, mid, r2.astype(BF16)


def _dot(a, b):
    return jnp.dot(a, b, preferred_element_type=F32)


def _dot_nt(a, b):
    return lax.dot_general(a, b, (((1,), (1,)), ((), ())), preferred_element_type=F32)


N_SPLIT = 3


def _sel_rows(sel3, x):
    return _dot(sel3, jnp.concatenate(_split3(x), axis=0))


def _sel_cols(x, sel3):
    return _dot(jnp.concatenate(_split3(x), axis=1), sel3)


def _rms(x, g):
    ms = jnp.mean(x * x, axis=-1, keepdims=True)
    return x * lax.rsqrt(ms + EPS) * g


def _inproj_body(x_ref, g_ref, w_ref, o_ref):
    h = _rms(x_ref[...], g_ref[...]).astype(BF16)
    o_ref[...] = _dot(h, w_ref[...]).astype(o_ref.dtype)


def _inproj(x2, g, w, tm, tn):
    m, d = x2.shape
    n = w.shape[1]
    return pl.pallas_call(
        _inproj_body,
        grid=(n // tn, m // tm),
        in_specs=[pl.BlockSpec((tm, d), lambda j, i: (i, 0)),
                  pl.BlockSpec((1, d), lambda j, i: (0, 0)),
                  pl.BlockSpec((d, tn), lambda j, i: (0, j))],
        out_specs=pl.BlockSpec((tm, tn), lambda j, i: (i, j)),
        out_shape=jax.ShapeDtypeStruct((m, n), BF16),
        compiler_params=_cparams(("arbitrary", "arbitrary")),
        name="inproj",
    )(x2, g, w)


def _hgrn_level_matrix():
    c = HG_CHUNK
    mat = np.zeros((8, c, c), np.float32)
    for i in range(c):
        mat[0, i, : i + 1] = 1.0
        mat[7, i, i + 1:] = 1.0
    for lvl in range(HG_LEVELS):
        m = 1 << lvl
        for r in range(c):
            start = (r // (2 * m)) * 2 * m
            mid = start + m
            if r >= mid:
                mat[lvl + 1, r, mid: r + 1] = 1.0
            else:
                mat[lvl + 1, r, r + 1: mid] = 1.0
    return mat.reshape(8 * c, c)


def _hgrn_pair_masks():
    n = 2 * HG_CHUNK
    row, col = np.meshgrid(np.arange(n), np.arange(n), indexing="ij")
    out = [row == col]
    for lvl in range(HG_LEVELS):
        same_block = (row >> (lvl + 1)) == (col >> (lvl + 1))
        out.append(same_block & (((row >> lvl) & 1) == 1) & (((col >> lvl) & 1) == 0))
    return np.stack(out).astype(np.float32)


def _hgrn_body(q_ref, f_ref, i_ref, g_ref, lb_ref, ng_ref, m_ref, mask_ref, o_ref, s_ref, *, n_chunks):
    c = HG_CHUNK

    @pl.when(pl.program_id(1) == 0)
    def _():
        s_ref[...] = jnp.zeros_like(s_ref)

    lb = lb_ref[...]
    log_lb = jnp.log(lb)
    log1m_lb = jnp.log1p(-lb)
    one_m_lb = 1.0 - lb
    ng = ng_ref[...]
    sel = m_ref[...]

    rowv = lax.broadcasted_iota(jnp.int32, (2 * c, HG_DK), 0)
    second_half = [((rowv >> lvl) & 1) == 1 for lvl in range(HG_LEVELS)]

    def chunk(ci, carry):
        r0 = pl.multiple_of(ci * c, c)
        rows = pl.ds(r0, c)
        hq = q_ref[rows, :].astype(F32)
        ff = f_ref[rows, :].astype(F32)
        hv = i_ref[rows, :].astype(F32)
        hg = g_ref[rows, :].astype(F32)
        q = hq * _sigmoid(hq)
        e_f = jnp.exp(-jnp.abs(ff))
        one_p = 1.0 + e_f
        log_sig = jnp.minimum(ff, 0.0) - jnp.log(one_p)
        b = log1m_lb + log_sig
        lf = jnp.maximum(log_lb, b) + jnp.log(1.0 + jnp.exp(-jnp.abs(log_lb - b)))
        k = one_m_lb * jnp.where(ff >= 0.0, e_f, 1.0) / one_p
        gate = hg * _sigmoid(hg)
        ex = _sel_rows(sel, lf)

        for pair in range(HG_HEADS // 2):
            sls = [slice(h * HG_DK, (h + 1) * HG_DK) for h in (2 * pair, 2 * pair + 1)]
            stack = lambda a, lo=0, hi=c: jnp.concatenate([a[lo:hi, sl] for sl in sls], axis=0)
            level = lambda n: stack(ex, n * c, (n + 1) * c)
            qp, kp, vp = stack(q).astype(BF16), stack(k).astype(BF16), stack(hv)
            decay_of = lambda n: jnp.exp(level(n).astype(BF16))
            att = mask_ref[0] * _dot(qp, stack(k).T.astype(BF16))
            for lvl in range(HG_LEVELS):
                xl = jnp.where(second_half[lvl], qp, kp) * decay_of(lvl + 1)
                att = att + mask_ref[lvl + 1] * _dot(xl, xl.astype(F32).T.astype(BF16))
            o_intra = _dot(att.astype(BF16), vp.astype(BF16))
            e0 = level(0)
            q_in = qp * jnp.exp(e0.astype(BF16))
            k_out = kp * decay_of(7)
            for idx, sl in enumerate(sls):
                h = 2 * pair + idx
                r = slice(idx * c, (idx + 1) * c)
                st = s_ref[h]
                o = o_intra[r] + _dot(q_in[r], st.T.astype(BF16))
                decay = jnp.exp(e0[(idx + 1) * c - 1:(idx + 1) * c, :])
                s_ref[h] = st * decay + _dot(vp[r].T.astype(BF16), k_out[r])
                y = _rms(o, ng[:, sl]) * gate[:, sl]
                o_ref[rows, sl] = y.astype(o_ref.dtype)
        return carry

    lax.fori_loop(0, n_chunks, chunk, 0, unroll=8)


def _hgrn(proj, lb, ng, bsz, t, tq):
    m = proj.shape[0]
    nq = t // tq
    sel = jnp.asarray(np.tile(_hgrn_level_matrix(), (1, N_SPLIT)), BF16)
    masks = jnp.asarray(_hgrn_pair_masks())
    w = HG_WIDTH

    def col_spec(off):
        blk = off // w
        return pl.BlockSpec((tq, w), lambda b, i: (b * nq + i, blk))

    vec = pl.BlockSpec((1, w), lambda b, i: (0, 0))
    return pl.pallas_call(
        functools.partial(_hgrn_body, n_chunks=tq // HG_CHUNK),
        grid=(bsz, nq),
        in_specs=[col_spec(OFF_HQ), col_spec(OFF_HF), col_spec(OFF_HI), col_spec(OFF_HG), vec, vec,
                  pl.BlockSpec(sel.shape, lambda b, i: (0, 0)),
                  pl.BlockSpec(masks.shape, lambda b, i: (0, 0, 0))],
        out_specs=pl.BlockSpec((tq, w), lambda b, i: (b * nq + i, 0)),
        out_shape=jax.ShapeDtypeStruct((m, w), BF16),
        scratch_shapes=[pltpu.VMEM((HG_HEADS, HG_DV, HG_DK), F32)],
        compiler_params=_cparams(("arbitrary", "arbitrary")),
        name="hgrn2",
    )(proj, proj, proj, proj, lb, ng, sel, masks)


def _mla_prep_body(cq_ref, ckv_ref, kr_ref, cos_ref, sin_ref, qn_ref, kvn_ref, wq_ref, wkv_ref, vone_ref,
                   q_out, k_out, v_out):
    scale = (MLA_NOPE + MLA_ROPE) ** -0.5 * math.log2(math.e)
    cqn = _rms(cq_ref[...].astype(F32), qn_ref[...]).astype(BF16)
    ckvn = _rms(ckv_ref[...].astype(F32), kvn_ref[...]).astype(BF16)
    qq = _dot(cqn, wq_ref[...])
    kv = _dot(ckvn, wkv_ref[...])
    cos = cos_ref[...]
    sin = sin_ref[...]
    lane = lax.broadcasted_iota(jnp.int32, (1, HEAD_PAD), 1)
    cq_t = scale * (cos + (lane < MLA_NOPE).astype(F32))
    sq_t = scale * sin
    krb = kr_ref[...].astype(F32)
    k_rope = krb * cos + pltpu.roll(krb, 64, 1) * sin
    hw = MLA_HEADS * HEAD_PAD
    vone = vone_ref[...]
    for h in range(MLA_HEADS):
        sl = slice(h * HEAD_PAD, (h + 1) * HEAD_PAD)
        sl2 = slice(hw + h * HEAD_PAD, hw + (h + 1) * HEAD_PAD)
        q_out[:, sl] = (qq[:, sl] * cq_t + qq[:, sl2] * sq_t).astype(BF16)
        k_out[:, sl] = (kv[:, sl] + k_rope).astype(BF16)
        v_out[:, sl] = (kv[:, sl2] + vone[:, sl]).astype(BF16)


def _mla_prep(proj, cos_t, sin_t, qn, kvn, wq, wkv, vone, tm):
    m = proj.shape[0]
    hw = MLA_HEADS * HEAD_PAD
    full = lambda a: pl.BlockSpec(a.shape, lambda i: (0, 0))
    out = jax.ShapeDtypeStruct((m, hw), BF16)
    ospec = pl.BlockSpec((tm, hw), lambda i: (i, 0))
    return pl.pallas_call(
        _mla_prep_body,
        grid=(m // tm,),
        in_specs=[pl.BlockSpec((tm, MLA_Q_LORA), lambda i: (i, OFF_CQ // MLA_Q_LORA)),
                  pl.BlockSpec((tm, MLA_KV_LORA), lambda i: (i, OFF_CKV // MLA_KV_LORA)),
                  pl.BlockSpec((tm, LANES), lambda i: (i, OFF_MISC // LANES)),
                  pl.BlockSpec((tm, LANES), lambda i: (i, 0)),
                  pl.BlockSpec((tm, LANES), lambda i: (i, 0)),
                  full(qn), full(kvn), full(wq), full(wkv), full(vone)],
        out_specs=[ospec, ospec, ospec],
        out_shape=[out, out, out],
        compiler_params=_cparams(("arbitrary",)),
        name="mla_prep",
    )(proj, proj, proj, cos_t, sin_t, qn, kvn, wq, wkv, vone)


def _attn_body(q_ref, k_ref, v_ref, o_ref, *, tq, tk):
    i = pl.program_id(2)
    lane = lax.broadcasted_iota(jnp.int32, (1, HEAD_PAD), 1)
    qpos = lax.broadcasted_iota(jnp.int32, (tq, tk), 0)
    kpos = lax.broadcasted_iota(jnp.int32, (tq, tk), 1)
    causal = kpos <= qpos
    heads = range(ATTN_HEADS)
    slices = [slice(h * HEAD_PAD, (h + 1) * HEAD_PAD) for h in heads]
    block = lambda j: pl.ds(pl.multiple_of(j * tk, tk), tk)

    def update(h, j, qrows, mask, m_prev, acc):
        s = _dot_nt(q_ref[qrows, slices[h]], k_ref[block(j), slices[h]])
        if mask is not None:
            s = jnp.where(mask, s, -jnp.inf)
        m_new = jnp.maximum(m_prev, jnp.max(s, axis=-1, keepdims=True))
        p = jnp.exp2((s - m_new).astype(BF16))
        return m_new, jnp.exp2(m_prev - m_new) * acc + _dot(p, v_ref[block(j), slices[h]])

    def step(j, state, mask=None):
        return tuple(update(h, j, slice(None), mask, *state[h]) for h in heads)

    def lower_half_step(j, state):
        new = []
        for h in heads:
            m, acc = state[h]
            m_lo, acc_lo = update(h, j, slice(tk, tq), causal[:tk], m[tk:], acc[tk:])
            new.append((jnp.concatenate([m[:tk], m_lo], axis=0), jnp.concatenate([acc[:tk], acc_lo], axis=0)))
        return tuple(new)

    def finish(state):
        for pair in range(ATTN_HEADS // 2):
            acc0, acc1 = state[2 * pair][1], state[2 * pair + 1][1]
            den0 = jnp.sum(jnp.where(lane == MLA_V, acc0, 0.0), axis=-1, keepdims=True)
            den1 = jnp.sum(jnp.where(lane == 0, acc1, 0.0), axis=-1, keepdims=True)
            o_ref[:, pair * HEAD_PAD:(pair + 1) * HEAD_PAD] = jnp.where(
                lane < MLA_V, acc0 / den0, acc1 / den1).astype(o_ref.dtype)

    def two_steps(j2, state):
        return step(2 * j2 + 1, step(2 * j2, state))

    init = (jnp.full((tq, 1), -jnp.inf, F32), jnp.zeros((tq, HEAD_PAD), F32))
    state = lax.fori_loop(0, i, two_steps, (init,) * ATTN_HEADS)
    finish(lower_half_step(2 * i + 1, step(2 * i, state, causal)))


def _attn(q, k, v, bsz, t, tq):
    m = q.shape[0]
    nq = t // tq
    pw = ATTN_HEADS * HEAD_PAD
    return pl.pallas_call(
        functools.partial(_attn_body, tq=tq, tk=tq // 2),
        grid=(bsz, MLA_HEADS // ATTN_HEADS, nq),
        in_specs=[pl.BlockSpec((tq, pw), lambda b, h, i: (b * nq + i, h)),
                  pl.BlockSpec((t, pw), lambda b, h, i: (b, h)),
                  pl.BlockSpec((t, pw), lambda b, h, i: (b, h))],
        out_specs=pl.BlockSpec((tq, ATTN_HEADS * MLA_V), lambda b, h, i: (b * nq + i, h)),
        out_shape=jax.ShapeDtypeStruct((m, MLA_WIDTH), BF16),
        compiler_params=_cparams(("arbitrary", "arbitrary", "arbitrary")),
        name="mla_attn",
    )(q, k, v)


def _softplus(x):
    return jnp.maximum(x, 0.0) + jnp.log1p(jnp.exp(-jnp.abs(x)))


def _ssd_body(z_ref, dt_ref, xbc_ref, cw_ref, cb_ref, dtb_ref, alog_ref, dsk_ref, ng_ref, tril_ref, e8_ref,
              o_ref, xp_ref, xc_ref, s_ref, *, tq):
    L = SSD_CHUNK
    gn = SSM_GROUPS * SSM_STATE
    half = SSM_WIDTH // SSM_GROUPS
    t_idx = pl.program_id(1)

    @pl.when(t_idx == 0)
    def _():
        s_ref[...] = jnp.zeros_like(s_ref)
        xp_ref[0:8, :] = jnp.zeros((8, SSM_CONV_DIM), F32)

    @pl.when(t_idx > 0)
    def _():
        xp_ref[0:8, :] = xp_ref[tq:tq + 8, :]

    xp_ref[8:8 + tq, :] = xbc_ref[...].astype(F32)
    cw = cw_ref[...]
    acc = cb_ref[...] + cw[3:4, :] * xp_ref[8:8 + tq, :]
    for w in range(SSM_CONV - 1):
        acc = acc + cw[w:w + 1, :] * xp_ref[5 + w:5 + w + tq, :]
    xc_ref[...] = acc * _sigmoid(acc)

    a_neg = -jnp.exp(alog_ref[...])
    dtb = dtb_ref[...]
    dsk = dsk_ref[...]
    ng = ng_ref[...]
    tril = tril_ref[...]
    e8 = e8_ref[...]
    ri = lax.broadcasted_iota(jnp.int32, (L, L), 0)
    ci = lax.broadcasted_iota(jnp.int32, (L, L), 1)
    tri = ci <= ri
    lane_gn = lax.broadcasted_iota(jnp.int32, (1, gn), 1)
    lane_w = lax.broadcasted_iota(jnp.int32, (1, SSM_WIDTH), 1)
    row_gn = lax.broadcasted_iota(jnp.int32, (gn, SSM_WIDTH), 0)
    col_w = lax.broadcasted_iota(jnp.int32, (gn, SSM_WIDTH), 1)
    blockdiag = (row_gn // SSM_STATE) == (col_w // half)
    head_mask = [(lane_w // SSM_HEAD_DIM) == h for h in range(SSM_HEADS)]

    def chunk(c, carry):
        rows = pl.ds(pl.multiple_of(c * L, L), L)
        xc = xc_ref[rows, :]
        xs = xc[:, :SSM_WIDTH]
        bm = xc[:, SSM_WIDTH:SSM_WIDTH + gn]
        cm = xc[:, SSM_WIDTH + gn:]
        dt = _softplus(dt_ref[rows, :].astype(F32) + dtb)
        a = dt * a_neg
        acum = _sel_rows(tril, a)
        dt_e = _sel_cols(dt, e8)
        ac_e = _sel_cols(acum, e8)
        x_dt = xs * dt_e
        last = ac_e[L - 1:L, :]
        x_dec = (x_dt * jnp.exp(last - ac_e)).astype(BF16)
        x_b = x_dt.astype(BF16)
        bm_t = bm.T.astype(BF16)
        cb0 = _dot(jnp.where(lane_gn < SSM_STATE, cm, 0.0).astype(BF16), bm_t)
        cb1 = _dot(jnp.where(lane_gn >= SSM_STATE, cm, 0.0).astype(BF16), bm_t)
        ac_t = acum.T
        ws, xblk = [], []
        for h in range(SSM_HEADS):
            hl = DT_LANE + h
            seg = acum[:, hl:hl + 1] - ac_t[hl:hl + 1, :]
            lmat = jnp.exp(jnp.where(tri, seg, -jnp.inf))
            ws.append((lmat * (cb0 if h < SSM_HEADS // SSM_GROUPS else cb1)).astype(BF16))
            xblk.append(jnp.where(head_mask[h], x_b, jnp.zeros_like(x_b)))
        y = _dot(jnp.concatenate(ws, axis=1), jnp.concatenate(xblk, axis=0))
        st = s_ref[...]
        y = y + jnp.exp(ac_e) * _dot(cm.astype(BF16), st.astype(BF16))
        s_ref[...] = jnp.exp(last) * st + jnp.where(blockdiag, _dot(bm_t, x_dec), 0.0)
        y = y + xs * dsk
        z = z_ref[rows, :].astype(F32)
        y = y * (z * _sigmoid(z))
        y0 = _rms(y[:, :half], ng[:, :half])
        y1 = _rms(y[:, half:], ng[:, half:])
        o_ref[rows, :half] = y0.astype(o_ref.dtype)
        o_ref[rows, half:] = y1.astype(o_ref.dtype)
        return carry

    lax.fori_loop(0, tq // L, chunk, 0, unroll=4)


def _ssd(proj, cw, cb, dtb, alog, dsk, ng, bsz, t, tq):
    m = proj.shape[0]
    nq = t // tq
    L = SSD_CHUNK
    tril = jnp.asarray(np.tile(np.tril(np.ones((L, L), np.float32)), (1, N_SPLIT)), BF16)
    e8 = np.zeros((LANES, SSM_WIDTH), np.float32)
    for h in range(SSM_HEADS):
        e8[DT_LANE + h, h * SSM_HEAD_DIM:(h + 1) * SSM_HEAD_DIM] = 1.0
    e8 = jnp.asarray(np.tile(e8, (N_SPLIT, 1)), BF16)
    full = lambda a: pl.BlockSpec(a.shape, lambda b, i: (0, 0))
    return pl.pallas_call(
        functools.partial(_ssd_body, tq=tq),
        grid=(bsz, nq),
        in_specs=[pl.BlockSpec((tq, SSM_WIDTH), lambda b, i: (b * nq + i, OFF_SZ // SSM_WIDTH)),
                  pl.BlockSpec((tq, LANES), lambda b, i: (b * nq + i, OFF_MISC // LANES)),
                  pl.BlockSpec((tq, SSM_CONV_DIM), lambda b, i: (b * nq + i, OFF_SXBC // SSM_CONV_DIM)),
                  full(cw), full(cb), full(dtb), full(alog), full(dsk), full(ng), full(tril), full(e8)],
        out_specs=pl.BlockSpec((tq, SSM_WIDTH), lambda b, i: (b * nq + i, 0)),
        out_shape=jax.ShapeDtypeStruct((m, SSM_WIDTH), BF16),
        scratch_shapes=[pltpu.VMEM((tq + 8, SSM_CONV_DIM), F32),
                        pltpu.VMEM((tq, SSM_CONV_DIM), F32),
                        pltpu.VMEM((SSM_GROUPS * SSM_STATE, SSM_WIDTH), F32)],
        compiler_params=_cparams(("arbitrary", "arbitrary")),
        name="ssd",
    )(proj, proj, proj, cw, cb, dtb, alog, dsk, ng, tril, e8)


def _merge_body(ya_ref, yb_ref, yc_ref, x_ref, g_ref, wg_ref, wa_ref, wb_ref, wc_ref, wo_ref, o_ref):
    d = D_MODEL
    x = x_ref[...]
    h = _rms(x, g_ref[...]).astype(BF16)
    gate = lambda b: _sigmoid(_dot(h, wg_ref[:, b * d:(b + 1) * d]))
    merged = (gate(0) * _dot(ya_ref[...], wa_ref[...])
              + gate(1) * _dot(yb_ref[...], wb_ref[...])
              + gate(2) * _dot(yc_ref[...], wc_ref[...]))
    o_ref[...] = x + _dot(merged.astype(BF16), wo_ref[...])


def _merge(ya, yb, yc, x2, g, wg, wa, wb, wc, wo, tm):
    m, d = x2.shape
    full = lambda a: pl.BlockSpec(a.shape, lambda i: (0, 0))
    row = lambda w: pl.BlockSpec((tm, w), lambda i: (i, 0))
    return pl.pallas_call(
        _merge_body,
        grid=(m // tm,),
        in_specs=[row(HG_WIDTH), row(MLA_WIDTH), row(SSM_WIDTH), row(d),
                  full(g), full(wg), full(wa), full(wb), full(wc), full(wo)],
        out_specs=row(d),
        out_shape=jax.ShapeDtypeStruct((m, d), F32),
        compiler_params=_cparams(("arbitrary",)),
        name="merge_out",
    )(ya, yb, yc, x2, g, wg, wa, wb, wc, wo)


def _mlp_body(x_ref, g_ref, wu_ref, wd_ref, fg_ref, o_ref, *, last_layer):
    x = x_ref[...]
    h = _rms(x, g_ref[...]).astype(BF16)
    u = jnp.maximum(_dot(h, wu_ref[...]), 0.0)
    y = x + _dot((u * u).astype(BF16), wd_ref[...])
    o_ref[...] = _rms(y, fg_ref[...]) if last_layer else y


def _mlp(x2, g, wu, wd, final_g, tm, last_layer):
    m, d = x2.shape
    const = lambda a: pl.BlockSpec(a.shape, lambda i: (0, 0), pipeline_mode=pl.Buffered(1))
    return pl.pallas_call(
        functools.partial(_mlp_body, last_layer=last_layer),
        grid=(m // tm,),
        in_specs=[pl.BlockSpec((tm, d), lambda i: (i, 0)), const(g), const(wu), const(wd), const(final_g)],
        out_specs=pl.BlockSpec((tm, d), lambda i: (i, 0)),
        out_shape=jax.ShapeDtypeStruct((m, d), F32),
        compiler_params=_cparams(("arbitrary",)),
        name="mlp",
    )(x2, g, wu, wd, final_g)


def _rot_cols(w):
    hr = MLA_ROPE // 2
    return jnp.concatenate([-w[..., hr:], w[..., :hr]], axis=-1)


def _pack_w_in(w_in):
    dep, d, _ = w_in.shape
    sizes = (512, 512, 512, 512, MLA_Q_LORA, MLA_KV_LORA, MLA_ROPE, SSM_WIDTH, SSM_CONV_DIM, SSM_HEADS)
    assert sum(sizes) == MIXER_IN_COLS
    pts = np.cumsum(sizes)[:-1].tolist()
    mixers = w_in[..., :MIXER_IN_COLS].astype(BF16)
    hq, hf, hi, hg, cq, ckv, kr, sz, sxbc, sdt = jnp.split(mixers, pts, axis=-1)
    z = lambda n: jnp.zeros((dep, d, n), BF16)
    misc = jnp.concatenate([_rot_cols(kr), sdt, z(32 - SSM_HEADS), kr, z(32)], axis=-1)
    packed = jnp.concatenate([hq, hf, hi, hg, sz, misc, cq, sxbc, ckv], axis=-1)
    assert packed.shape[-1] == PROJ_COLS
    return packed, w_in[..., MIXER_IN_COLS:].astype(BF16)


def _pack_w_uq(w):
    dep, r, _ = w.shape
    wh = w.reshape(dep, r, MLA_HEADS, MLA_NOPE + MLA_ROPE)
    nope, rope = wh[..., :MLA_NOPE], wh[..., MLA_NOPE:]
    z = lambda n: jnp.zeros((dep, r, MLA_HEADS, n), w.dtype)
    a = jnp.concatenate([nope, rope, z(32)], axis=-1).reshape(dep, r, MLA_HEADS * HEAD_PAD)
    b = jnp.concatenate([z(64), _rot_cols(rope), z(32)], axis=-1).reshape(dep, r, MLA_HEADS * HEAD_PAD)
    return jnp.concatenate([a, b], axis=-1).astype(BF16)


def _pack_w_ukv(w):
    dep, r, _ = w.shape
    wh = w.reshape(dep, r, MLA_HEADS // 2, 2, MLA_NOPE + MLA_V)
    kn, v = wh[..., :MLA_NOPE], wh[..., MLA_NOPE:]
    z = jnp.zeros_like(kn)
    ka = jnp.concatenate([kn, z], axis=-1).reshape(dep, r, MLA_HEADS * HEAD_PAD)
    va = jnp.stack([jnp.concatenate([v[..., 0, :], z[..., 0, :]], axis=-1),
                    jnp.concatenate([z[..., 1, :], v[..., 1, :]], axis=-1)], axis=-2)
    va = va.reshape(dep, r, MLA_HEADS * HEAD_PAD)
    return jnp.concatenate([ka, va], axis=-1).astype(BF16)


def _v_ones():
    v = np.zeros((1, MLA_HEADS * HEAD_PAD), np.float32)
    for h in range(MLA_HEADS):
        v[0, h * HEAD_PAD + (MLA_V if h % 2 == 0 else 0)] = 1.0
    return jnp.asarray(v)


def _rope_tables(positions):
    inv = ROPE_THETA ** (-jnp.arange(0, MLA_ROPE, 2, dtype=F32) / MLA_ROPE)
    ang = positions.astype(F32).reshape(-1, 1) * inv
    z = lambda n: jnp.zeros((ang.shape[0], n), F32)
    cos, sin = jnp.cos(ang), jnp.sin(ang)
    return (jnp.concatenate([z(64), cos, cos, z(32)], axis=-1),
            jnp.concatenate([z(64), sin, sin, z(32)], axis=-1))


def _row_tile(n, want):
    t = min(n, want)
    assert n % t == 0, (n, t)
    return t


def kernel(x, positions, mix_norm, w_in, hg_lb_logits, hg_norm, mla_q_norm, mla_kv_norm, mla_w_uq, mla_w_ukv,
           ssm_conv_w, ssm_conv_b, ssm_dt_bias, ssm_a_log, ssm_d, ssm_norm, w_br_hg, w_br_mla, w_br_ssm,
           w_out, mlp_norm, w_up, w_down, final_norm):
    bsz, t, d = x.shape
    depth = w_in.shape[0]
    assert d == D_MODEL and t % SSD_CHUNK == 0
    m = bsz * t
    tm = _row_tile(m, 512)
    tm_in = _row_tile(m, 1024)
    tq = _row_tile(t, 512)
    tq_attn = _row_tile(t, 1024)

    lbs = jnp.cumsum(jax.nn.softmax(hg_lb_logits.astype(F32), axis=0), axis=0)
    lbs = lbs - lbs[0:1]
    w_in_p, w_gate = _pack_w_in(w_in)
    wq_p = _pack_w_uq(mla_w_uq)
    wkv_p = _pack_w_ukv(mla_w_ukv)
    vone = _v_ones()
    cos_t, sin_t = _rope_tables(positions)
    pad_h = lambda a: jnp.pad(a.astype(F32), ((0, 0), (DT_LANE, LANES - DT_LANE - SSM_HEADS)))
    dtb_p, alog_p = pad_h(ssm_dt_bias), pad_h(ssm_a_log)
    dsk_e = jnp.repeat(ssm_d.astype(F32), SSM_HEAD_DIM, axis=-1)
    bf = lambda a: a.astype(BF16)
    wa, wb, wc, wo, wu, wd = bf(w_br_hg), bf(w_br_mla), bf(w_br_ssm), bf(w_out), bf(w_up), bf(w_down)
    r1 = lambda a: a.reshape(1, -1).astype(F32)

    x2 = x.reshape(m, d)
    for l in range(depth):
        proj = _inproj(x2, r1(mix_norm[l]), w_in_p[l], tm_in, PROJ_COLS // 2)
        ya = _hgrn(proj, r1(lbs[l]), r1(hg_norm[l]), bsz, t, tq)
        q, k, v = _mla_prep(proj, cos_t, sin_t, r1(mla_q_norm[l]), r1(mla_kv_norm[l]), wq_p[l], wkv_p[l],
                            vone, tm)
        yb = _attn(q, k, v, bsz, t, tq_attn)
        yc = _ssd(proj, ssm_conv_w[l].astype(F32), r1(ssm_conv_b[l]), r1(dtb_p[l]), r1(alog_p[l]),
                  r1(dsk_e[l]), r1(ssm_norm[l]), bsz, t, tq)
        x2 = _merge(ya, yb, yc, x2, r1(mix_norm[l]), w_gate[l], wa[l], wb[l], wc[l], wo[l], tm)
        x2 = _mlp(x2, r1(mlp_norm[l]), wu[l], wd[l], r1(final_norm), tm, last_layer=(l == depth - 1))
    return x2.reshape(bsz, t, d)
```

```python
import functools
import math

import numpy as np
import jax
import jax.numpy as jnp
from jax import lax
from jax.experimental import pallas as pl
from jax.experimental.pallas import tpu as pltpu

F32 = jnp.float32
BF16 = jnp.bfloat16

D_MODEL = 1024
EPS = 1e-6
HG_HEADS = 4
HG_DK = 128
HG_DV = 128
HG_WIDTH = HG_HEADS * HG_DV
MLA_HEADS = 8
MLA_NOPE = 64
MLA_ROPE = 32
MLA_V = 64
MLA_Q_LORA = 384
MLA_KV_LORA = 256
MLA_WIDTH = MLA_HEADS * MLA_V
ROPE_THETA = 10000.0
SSM_HEADS = 8
SSM_HEAD_DIM = 64
SSM_WIDTH = SSM_HEADS * SSM_HEAD_DIM
SSM_GROUPS = 2
SSM_STATE = 64
SSM_CONV = 4
SSM_CONV_DIM = SSM_WIDTH + 2 * SSM_GROUPS * SSM_STATE
D_FF = 4 * D_MODEL
N_BRANCH = 3

LANES = 128
HEAD_PAD = 128

OFF_HQ, OFF_HF, OFF_HI, OFF_HG = 0, 512, 1024, 1536
OFF_SZ = 2048
OFF_MISC = 2560
DT_LANE = 32
OFF_CQ = 2688
OFF_SXBC = 3072
OFF_CKV = 3840
PROJ_COLS = 4096
HALF_COLS = PROJ_COLS // 2
MIXER_IN_COLS = 4008

HG_CHUNK = 64
HG_LEVELS = 6
SSD_CHUNK = 128
ATTN_HEADS = 4

ROW_TILE = 512
VMEM_LIMIT = 48 * 1024 * 1024


def _cparams(sem, flags=None):
    return pltpu.CompilerParams(dimension_semantics=sem, vmem_limit_bytes=VMEM_LIMIT, flags=flags)


def _sigmoid(x):
    return jax.nn.sigmoid(x)


def _split3(x):
    hi = x.astype(BF16)
    r1 = x - hi.astype(F32)
    mid = r1.astype(BF16)
    r2 = r1 - mid.astype(F32)
    return hi, mid, r2.astype(BF16)


def _dot(a, b):
    return jnp.dot(a, b, preferred_element_type=F32)


def _dot_nt(a, b):
    return lax.dot_general(a, b, (((1,), (1,)), ((), ())), preferred_element_type=F32)


N_SPLIT = 3


def _sel_rows(sel3, x):
    return _dot(sel3, jnp.concatenate(_split3(x), axis=0))


def _sel_cols(x, sel3):
    return _dot(jnp.concatenate(_split3(x), axis=1), sel3)


def _rms(x, g):
    ms = jnp.mean(x * x, axis=-1, keepdims=True)
    return x * lax.rsqrt(ms + EPS) * g


def _inproj_body(x_ref, g_ref, w_ref, o_ref):
    h = _rms(x_ref[...], g_ref[...]).astype(BF16)
    o_ref[...] = _dot(h, w_ref[...]).astype(o_ref.dtype)


def _inproj(x2, g, w, tm, tn):
    m, d = x2.shape
    n = w.shape[1]
    return pl.pallas_call(
        _inproj_body,
        grid=(n // tn, m // tm),
        in_specs=[pl.BlockSpec((tm, d), lambda j, i: (i, 0)),
                  pl.BlockSpec((1, d), lambda j, i: (0, 0)),
                  pl.BlockSpec((d, tn), lambda j, i: (0, j))],
        out_specs=pl.BlockSpec((tm, tn), lambda j, i: (i, j)),
        out_shape=jax.ShapeDtypeStruct((m, n), BF16),
        compiler_params=_cparams(("arbitrary", "arbitrary")),
        name="inproj",
    )(x2, g, w)


def _hgrn_level_matrix():
    c = HG_CHUNK
    mat = np.zeros((8, c, c), np.float32)
    for i in range(c):
        mat[0, i, : i + 1] = 1.0
        mat[7, i, i + 1:] = 1.0
    for lvl in range(HG_LEVELS):
        m = 1 << lvl
        for r in range(c):
            start = (r // (2 * m)) * 2 * m
            mid = start + m
            if r >= mid:
                mat[lvl + 1, r, mid: r + 1] = 1.0
            else:
                mat[lvl + 1, r, r + 1: mid] = 1.0
    return mat.reshape(8 * c, c)


def _hgrn_pair_masks():
    n = 2 * HG_CHUNK
    row, col = np.meshgrid(np.arange(n), np.arange(n), indexing="ij")
    out = [row == col]
    for lvl in range(HG_LEVELS):
        same_block = (row >> (lvl + 1)) == (col >> (lvl + 1))
        out.append(same_block & (((row >> lvl) & 1) == 1) & (((col >> lvl) & 1) == 0))
    return np.stack(out).astype(np.float32)


def _hgrn_body(x_ref, lb_ref, ng_ref, m_ref, mask_ref, o_ref, s_ref, *, n_chunks):
    c = HG_CHUNK

    @pl.when(pl.program_id(1) == 0)
    def _():
        s_ref[...] = jnp.zeros_like(s_ref)

    lb = lb_ref[...]
    log_lb = jnp.log(lb)
    log1m_lb = jnp.log1p(-lb)
    one_m_lb = 1.0 - lb
    ng = ng_ref[...]
    sel = m_ref[...]

    rowv = lax.broadcasted_iota(jnp.int32, (2 * c, HG_DK), 0)
    second_half = [((rowv >> lvl) & 1) == 1 for lvl in range(HG_LEVELS)]

    def chunk(ci, carry):
        r0 = pl.multiple_of(ci * c, c)
        rows = pl.ds(r0, c)
        slab = lambda off: x_ref[rows, off:off + HG_WIDTH].astype(F32)
        hq, ff, hv, hg = slab(OFF_HQ), slab(OFF_HF), slab(OFF_HI), slab(OFF_HG)
        q = hq * _sigmoid(hq)
        e_f = jnp.exp(-jnp.abs(ff))
        one_p = 1.0 + e_f
        log_sig = jnp.minimum(ff, 0.0) - jnp.log(one_p)
        b = log1m_lb + log_sig
        lf = jnp.maximum(log_lb, b) + jnp.log(1.0 + jnp.exp(-jnp.abs(log_lb - b)))
        k = one_m_lb * jnp.where(ff >= 0.0, e_f, 1.0) / one_p
        gate = hg * _sigmoid(hg)
        ex = _sel_rows(sel, lf)

        for pair in range(HG_HEADS // 2):
            sls = [slice(h * HG_DK, (h + 1) * HG_DK) for h in (2 * pair, 2 * pair + 1)]
            stack = lambda a, lo=0, hi=c: jnp.concatenate([a[lo:hi, sl] for sl in sls], axis=0)
            level = lambda n: stack(ex, n * c, (n + 1) * c)
            qp, kp, vp = stack(q).astype(BF16), stack(k).astype(BF16), stack(hv)
            decay_of = lambda n: jnp.exp(level(n).astype(BF16))
            att = mask_ref[0] * _dot(qp, stack(k).T.astype(BF16))
            for lvl in range(HG_LEVELS):
                xl = jnp.where(second_half[lvl], qp, kp) * decay_of(lvl + 1)
                att = att + mask_ref[lvl + 1] * _dot(xl, xl.astype(F32).T.astype(BF16))
            o_intra = _dot(att.astype(BF16), vp.astype(BF16))
            e0 = level(0)
            q_in = qp * jnp.exp(e0.astype(BF16))
            k_out = kp * decay_of(7)
            for idx, sl in enumerate(sls):
                h = 2 * pair + idx
                r = slice(idx * c, (idx + 1) * c)
                st = s_ref[h]
                o = o_intra[r] + _dot(q_in[r], st.T.astype(BF16))
                decay = jnp.exp(e0[(idx + 1) * c - 1:(idx + 1) * c, :])
                s_ref[h] = st * decay + _dot(vp[r].T.astype(BF16), k_out[r])
                y = _rms(o, ng[:, sl]) * gate[:, sl]
                o_ref[rows, sl] = y.astype(o_ref.dtype)
        return carry

    lax.fori_loop(0, n_chunks, chunk, 0, unroll=True)


def _hgrn(proj, lb, ng, bsz, t, tq):
    m = proj.shape[0]
    nq = t // tq
    sel = jnp.asarray(np.tile(_hgrn_level_matrix(), (1, N_SPLIT)), BF16)
    masks = jnp.asarray(_hgrn_pair_masks())
    w = HG_WIDTH

    vec = pl.BlockSpec((1, w), lambda b, i: (0, 0))
    return pl.pallas_call(
        functools.partial(_hgrn_body, n_chunks=tq // HG_CHUNK),
        grid=(bsz, nq),
        in_specs=[pl.BlockSpec((tq, HALF_COLS), lambda b, i: (b * nq + i, 0)), vec, vec,
                  pl.BlockSpec(sel.shape, lambda b, i: (0, 0)),
                  pl.BlockSpec(masks.shape, lambda b, i: (0, 0, 0))],
        out_specs=pl.BlockSpec((tq, w), lambda b, i: (b * nq + i, 0)),
        out_shape=jax.ShapeDtypeStruct((m, w), BF16),
        scratch_shapes=[pltpu.VMEM((HG_HEADS, HG_DV, HG_DK), F32)],
        compiler_params=_cparams(("arbitrary", "arbitrary")),
        name="hgrn2",
    )(proj, lb, ng, sel, masks)


def _mla_prep_body(x_ref, cos_ref, sin_ref, qn_ref, kvn_ref, wq_ref, wkv_ref, vone_ref,
                   q_out, k_out, v_out):
    scale = (MLA_NOPE + MLA_ROPE) ** -0.5 * math.log2(math.e)
    slab = lambda off, width: x_ref[:, off - HALF_COLS:off - HALF_COLS + width].astype(F32)
    cqn = _rms(slab(OFF_CQ, MLA_Q_LORA), qn_ref[...]).astype(BF16)
    ckvn = _rms(slab(OFF_CKV, MLA_KV_LORA), kvn_ref[...]).astype(BF16)
    qq = _dot(cqn, wq_ref[...])
    kv = _dot(ckvn, wkv_ref[...])
    cos = cos_ref[...]
    sin = sin_ref[...]
    lane = lax.broadcasted_iota(jnp.int32, (1, HEAD_PAD), 1)
    cq_t = scale * (cos + (lane < MLA_NOPE).astype(F32))
    sq_t = scale * sin
    krb = slab(OFF_MISC, LANES)
    k_rope = krb * cos + pltpu.roll(krb, 64, 1) * sin
    hw = MLA_HEADS * HEAD_PAD
    vone = vone_ref[...]
    for h in range(MLA_HEADS):
        sl = slice(h * HEAD_PAD, (h + 1) * HEAD_PAD)
        sl2 = slice(hw + h * HEAD_PAD, hw + (h + 1) * HEAD_PAD)
        q_out[:, sl] = (qq[:, sl] * cq_t + qq[:, sl2] * sq_t).astype(BF16)
        k_out[:, sl] = (kv[:, sl] + k_rope).astype(BF16)
        v_out[:, sl] = (kv[:, sl2] + vone[:, sl]).astype(BF16)


def _mla_prep(proj, cos_t, sin_t, qn, kvn, wq, wkv, vone, tm):
    m = proj.shape[0]
    hw = MLA_HEADS * HEAD_PAD
    full = lambda a: pl.BlockSpec(a.shape, lambda i: (0, 0))
    out = jax.ShapeDtypeStruct((m, hw), BF16)
    ospec = pl.BlockSpec((tm, hw), lambda i: (i, 0))
    return pl.pallas_call(
        _mla_prep_body,
        grid=(m // tm,),
        in_specs=[pl.BlockSpec((tm, HALF_COLS), lambda i: (i, 1)),
                  pl.BlockSpec((tm, LANES), lambda i: (i, 0)),
                  pl.BlockSpec((tm, LANES), lambda i: (i, 0)),
                  full(qn), full(kvn), full(wq), full(wkv), full(vone)],
        out_specs=[ospec, ospec, ospec],
        out_shape=[out, out, out],
        compiler_params=_cparams(("arbitrary",)),
        name="mla_prep",
    )(proj, cos_t, sin_t, qn, kvn, wq, wkv, vone)


def _attn_body(q_ref, k_ref, v_ref, o_ref, *, tq, tk):
    i = pl.program_id(2)
    lane = lax.broadcasted_iota(jnp.int32, (1, HEAD_PAD), 1)
    qpos = lax.broadcasted_iota(jnp.int32, (tq, tk), 0)
    kpos = lax.broadcasted_iota(jnp.int32, (tq, tk), 1)
    causal = kpos <= qpos
    heads = range(ATTN_HEADS)
    slices = [slice(h * HEAD_PAD, (h + 1) * HEAD_PAD) for h in heads]
    block = lambda j: pl.ds(pl.multiple_of(j * tk, tk), tk)

    def update(h, j, qrows, mask, m_prev, acc):
        s = _dot_nt(q_ref[qrows, slices[h]], k_ref[block(j), slices[h]])
        if mask is not None:
            s = jnp.where(mask, s, -jnp.inf)
        m_new = jnp.maximum(m_prev, jnp.max(s, axis=-1, keepdims=True))
        p = jnp.exp2((s - m_new).astype(BF16))
        return m_new, jnp.exp2(m_prev - m_new) * acc + _dot(p, v_ref[block(j), slices[h]])

    def step(j, state, mask=None):
        return tuple(update(h, j, slice(None), mask, *state[h]) for h in heads)

    def lower_half_step(j, state):
        new = []
        for h in heads:
            m, acc = state[h]
            m_lo, acc_lo = update(h, j, slice(tk, tq), causal[:tk], m[tk:], acc[tk:])
            new.append((jnp.concatenate([m[:tk], m_lo], axis=0), jnp.concatenate([acc[:tk], acc_lo], axis=0)))
        return tuple(new)

    def finish(state):
        for pair in range(ATTN_HEADS // 2):
            acc0, acc1 = state[2 * pair][1], state[2 * pair + 1][1]
            den0 = jnp.sum(jnp.where(lane == MLA_V, acc0, 0.0), axis=-1, keepdims=True)
            den1 = jnp.sum(jnp.where(lane == 0, acc1, 0.0), axis=-1, keepdims=True)
            o_ref[:, pair * HEAD_PAD:(pair + 1) * HEAD_PAD] = jnp.where(
                lane < MLA_V, acc0 / den0, acc1 / den1).astype(o_ref.dtype)

    def two_steps(j2, state):
        return step(2 * j2 + 1, step(2 * j2, state))

    init = (jnp.full((tq, 1), -jnp.inf, F32), jnp.zeros((tq, HEAD_PAD), F32))
    state = lax.fori_loop(0, i, two_steps, (init,) * ATTN_HEADS)
    finish(lower_half_step(2 * i + 1, step(2 * i, state, causal)))


def _attn(q, k, v, bsz, t, tq):
    m = q.shape[0]
    nq = t // tq
    pw = ATTN_HEADS * HEAD_PAD
    return pl.pallas_call(
        functools.partial(_attn_body, tq=tq, tk=tq // 2),
        grid=(bsz, MLA_HEADS // ATTN_HEADS, nq),
        in_specs=[pl.BlockSpec((tq, pw), lambda b, h, i: (b * nq + i, h)),
                  pl.BlockSpec((t, pw), lambda b, h, i: (b, h)),
                  pl.BlockSpec((t, pw), lambda b, h, i: (b, h))],
        out_specs=pl.BlockSpec((tq, ATTN_HEADS * MLA_V), lambda b, h, i: (b * nq + i, h)),
        out_shape=jax.ShapeDtypeStruct((m, MLA_WIDTH), BF16),
        compiler_params=_cparams(("arbitrary", "arbitrary", "arbitrary")),
        name="mla_attn",
    )(q, k, v)


def _softplus(x):
    return jnp.maximum(x, 0.0) + jnp.log1p(jnp.exp(-jnp.abs(x)))


def _ssd_body(x_ref, cw_ref, cb_ref, dtb_ref, alog_ref, dsk_ref, ng_ref, tril_ref, e8_ref,
              o_ref, xp_ref, xc_ref, s_ref, *, tq):
    L = SSD_CHUNK
    gn = SSM_GROUPS * SSM_STATE
    half = SSM_WIDTH // SSM_GROUPS
    t_idx = pl.program_id(1)

    @pl.when(t_idx == 0)
    def _():
        s_ref[...] = jnp.zeros_like(s_ref)
        xp_ref[0:8, :] = jnp.zeros((8, SSM_CONV_DIM), F32)

    @pl.when(t_idx > 0)
    def _():
        xp_ref[0:8, :] = xp_ref[tq:tq + 8, :]

    slab = lambda rows, off, width: x_ref[rows, off - HALF_COLS:off - HALF_COLS + width].astype(F32)
    xp_ref[8:8 + tq, :] = slab(slice(None), OFF_SXBC, SSM_CONV_DIM)
    cw = cw_ref[...]
    acc = cb_ref[...] + cw[3:4, :] * xp_ref[8:8 + tq, :]
    for w in range(SSM_CONV - 1):
        acc = acc + cw[w:w + 1, :] * xp_ref[5 + w:5 + w + tq, :]
    xc_ref[...] = acc * _sigmoid(acc)

    a_neg = -jnp.exp(alog_ref[...])
    dtb = dtb_ref[...]
    dsk = dsk_ref[...]
    ng = ng_ref[...]
    tril = tril_ref[...]
    e8 = e8_ref[...]
    ri = lax.broadcasted_iota(jnp.int32, (L, L), 0)
    ci = lax.broadcasted_iota(jnp.int32, (L, L), 1)
    tri = ci <= ri
    lane_gn = lax.broadcasted_iota(jnp.int32, (1, gn), 1)
    lane_w = lax.broadcasted_iota(jnp.int32, (1, SSM_WIDTH), 1)
    row_gn = lax.broadcasted_iota(jnp.int32, (gn, SSM_WIDTH), 0)
    col_w = lax.broadcasted_iota(jnp.int32, (gn, SSM_WIDTH), 1)
    blockdiag = (row_gn // SSM_STATE) == (col_w // half)
    head_mask = [(lane_w // SSM_HEAD_DIM) == h for h in range(SSM_HEADS)]

    def chunk(c, carry):
        rows = pl.ds(pl.multiple_of(c * L, L), L)
        xc = xc_ref[rows, :]
        xs = xc[:, :SSM_WIDTH]
        bm = xc[:, SSM_WIDTH:SSM_WIDTH + gn]
        cm = xc[:, SSM_WIDTH + gn:]
        dt = _softplus(slab(rows, OFF_MISC, LANES) + dtb)
        a = dt * a_neg
        acum = _sel_rows(tril, a)
        dt_e = _sel_cols(dt, e8)
        ac_e = _sel_cols(acum, e8)
        x_dt = xs * dt_e
        last = ac_e[L - 1:L, :]
        x_dec = (x_dt * jnp.exp(last - ac_e)).astype(BF16)
        x_b = x_dt.astype(BF16)
        bm_t = bm.T.astype(BF16)
        cb0 = _dot(jnp.where(lane_gn < SSM_STATE, cm, 0.0).astype(BF16), bm_t)
        cb1 = _dot(jnp.where(lane_gn >= SSM_STATE, cm, 0.0).astype(BF16), bm_t)
        ac_t = acum.T
        ws, xblk = [], []
        for h in range(SSM_HEADS):
            hl = DT_LANE + h
            seg = acum[:, hl:hl + 1] - ac_t[hl:hl + 1, :]
            lmat = jnp.exp(jnp.where(tri, seg, -jnp.inf))
            ws.append((lmat * (cb0 if h < SSM_HEADS // SSM_GROUPS else cb1)).astype(BF16))
            xblk.append(jnp.where(head_mask[h], x_b, jnp.zeros_like(x_b)))
        y = _dot(jnp.concatenate(ws, axis=1), jnp.concatenate(xblk, axis=0))
        st = s_ref[...]
        y = y + jnp.exp(ac_e) * _dot(cm.astype(BF16), st.astype(BF16))
        s_ref[...] = jnp.exp(last) * st + jnp.where(blockdiag, _dot(bm_t, x_dec), 0.0)
        y = y + xs * dsk
        z = slab(rows, OFF_SZ, SSM_WIDTH)
        y = y * (z * _sigmoid(z))
        y0 = _rms(y[:, :half], ng[:, :half])
        y1 = _rms(y[:, half:], ng[:, half:])
        o_ref[rows, :half] = y0.astype(o_ref.dtype)
        o_ref[rows, half:] = y1.astype(o_ref.dtype)
        return carry

    lax.fori_loop(0, tq // L, chunk, 0, unroll=True)


def _ssd(proj, cw, cb, dtb, alog, dsk, ng, bsz, t, tq):
    m = proj.shape[0]
    nq = t // tq
    L = SSD_CHUNK
    tril = jnp.asarray(np.tile(np.tril(np.ones((L, L), np.float32)), (1, N_SPLIT)), BF16)
    e8 = np.zeros((LANES, SSM_WIDTH), np.float32)
    for h in range(SSM_HEADS):
        e8[DT_LANE + h, h * SSM_HEAD_DIM:(h + 1) * SSM_HEAD_DIM] = 1.0
    e8 = jnp.asarray(np.tile(e8, (N_SPLIT, 1)), BF16)
    full = lambda a: pl.BlockSpec(a.shape, lambda b, i: (0, 0))
    return pl.pallas_call(
        functools.partial(_ssd_body, tq=tq),
        grid=(bsz, nq),
        in_specs=[pl.BlockSpec((tq, HALF_COLS), lambda b, i: (b * nq + i, 1)),
                  full(cw), full(cb), full(dtb), full(alog), full(dsk), full(ng), full(tril), full(e8)],
        out_specs=pl.BlockSpec((tq, SSM_WIDTH), lambda b, i: (b * nq + i, 0)),
        out_shape=jax.ShapeDtypeStruct((m, SSM_WIDTH), BF16),
        scratch_shapes=[pltpu.VMEM((tq + 8, SSM_CONV_DIM), F32),
                        pltpu.VMEM((tq, SSM_CONV_DIM), F32),
                        pltpu.VMEM((SSM_GROUPS * SSM_STATE, SSM_WIDTH), F32)],
        compiler_params=_cparams(("arbitrary", "arbitrary")),
        name="ssd",
    )(proj, cw, cb, dtb, alog, dsk, ng, tril, e8)


def _merge_body(ya_ref, yb_ref, yc_ref, x_ref, g_ref, wg_ref, wa_ref, wb_ref, wc_ref, wo_ref, o_ref):
    d = D_MODEL
    x = x_ref[...]
    h = _rms(x, g_ref[...]).astype(BF16)
    gate = lambda b: _sigmoid(_dot(h, wg_ref[:, b * d:(b + 1) * d]))
    merged = (gate(0) * _dot(ya_ref[...], wa_ref[...])
              + gate(1) * _dot(yb_ref[...], wb_ref[...])
              + gate(2) * _dot(yc_ref[...], wc_ref[...]))
    o_ref[...] = x + _dot(merged.astype(BF16), wo_ref[...])


def _merge(ya, yb, yc, x2, g, wg, wa, wb, wc, wo, tm):
    m, d = x2.shape
    full = lambda a: pl.BlockSpec(a.shape, lambda i: (0, 0))
    row = lambda w: pl.BlockSpec((tm, w), lambda i: (i, 0))
    return pl.pallas_call(
        _merge_body,
        grid=(m // tm,),
        in_specs=[row(HG_WIDTH), row(MLA_WIDTH), row(SSM_WIDTH), row(d),
                  full(g), full(wg), full(wa), full(wb), full(wc), full(wo)],
        out_specs=row(d),
        out_shape=jax.ShapeDtypeStruct((m, d), F32),
        compiler_params=_cparams(("arbitrary",)),
        name="merge_out",
    )(ya, yb, yc, x2, g, wg, wa, wb, wc, wo)


def _mlp_body(x_ref, g_ref, wu_ref, wd_ref, fg_ref, o_ref, *, last_layer):
    x = x_ref[...]
    h = _rms(x, g_ref[...]).astype(BF16)
    u = jnp.maximum(_dot(h, wu_ref[...]), 0.0)
    y = x + _dot((u * u).astype(BF16), wd_ref[...])
    o_ref[...] = _rms(y, fg_ref[...]) if last_layer else y


def _mlp(x2, g, wu, wd, final_g, tm, last_layer):
    m, d = x2.shape
    const = lambda a: pl.BlockSpec(a.shape, lambda i: (0, 0), pipeline_mode=pl.Buffered(1))
    return pl.pallas_call(
        functools.partial(_mlp_body, last_layer=last_layer),
        grid=(m // tm,),
        in_specs=[pl.BlockSpec((tm, d), lambda i: (i, 0)), const(g), const(wu), const(wd), const(final_g)],
        out_specs=pl.BlockSpec((tm, d), lambda i: (i, 0)),
        out_shape=jax.ShapeDtypeStruct((m, d), F32),
        compiler_params=_cparams(("arbitrary",)),
        name="mlp",
    )(x2, g, wu, wd, final_g)


def _rot_cols(w):
    hr = MLA_ROPE // 2
    return jnp.concatenate([-w[..., hr:], w[..., :hr]], axis=-1)


def _pack_w_in(w_in):
    dep, d, _ = w_in.shape
    sizes = (512, 512, 512, 512, MLA_Q_LORA, MLA_KV_LORA, MLA_ROPE, SSM_WIDTH, SSM_CONV_DIM, SSM_HEADS)
    assert sum(sizes) == MIXER_IN_COLS
    pts = np.cumsum(sizes)[:-1].tolist()
    mixers = w_in[..., :MIXER_IN_COLS].astype(BF16)
    hq, hf, hi, hg, cq, ckv, kr, sz, sxbc, sdt = jnp.split(mixers, pts, axis=-1)
    z = lambda n: jnp.zeros((dep, d, n), BF16)
    misc = jnp.concatenate([_rot_cols(kr), sdt, z(32 - SSM_HEADS), kr, z(32)], axis=-1)
    packed = jnp.concatenate([hq, hf, hi, hg, sz, misc, cq, sxbc, ckv], axis=-1)
    assert packed.shape[-1] == PROJ_COLS
    return packed, w_in[..., MIXER_IN_COLS:].astype(BF16)


def _pack_w_uq(w):
    dep, r, _ = w.shape
    wh = w.reshape(dep, r, MLA_HEADS, MLA_NOPE + MLA_ROPE)
    nope, rope = wh[..., :MLA_NOPE], wh[..., MLA_NOPE:]
    z = lambda n: jnp.zeros((dep, r, MLA_HEADS, n), w.dtype)
    a = jnp.concatenate([nope, rope, z(32)], axis=-1).reshape(dep, r, MLA_HEADS * HEAD_PAD)
    b = jnp.concatenate([z(64), _rot_cols(rope), z(32)], axis=-1).reshape(dep, r, MLA_HEADS * HEAD_PAD)
    return jnp.concatenate([a, b], axis=-1).astype(BF16)


def _pack_w_ukv(w):
    dep, r, _ = w.shape
    wh = w.reshape(dep, r, MLA_HEADS // 2, 2, MLA_NOPE + MLA_V)
    kn, v = wh[..., :MLA_NOPE], wh[..., MLA_NOPE:]
    z = jnp.zeros_like(kn)
    ka = jnp.concatenate([kn, z], axis=-1).reshape(dep, r, MLA_HEADS * HEAD_PAD)
    va = jnp.stack([jnp.concatenate([v[..., 0, :], z[..., 0, :]], axis=-1),
                    jnp.concatenate([z[..., 1, :], v[..., 1, :]], axis=-1)], axis=-2)
    va = va.reshape(dep, r, MLA_HEADS * HEAD_PAD)
    return jnp.concatenate([ka, va], axis=-1).astype(BF16)


def _v_ones():
    v = np.zeros((1, MLA_HEADS * HEAD_PAD), np.float32)
    for h in range(MLA_HEADS):
        v[0, h * HEAD_PAD + (MLA_V if h % 2 == 0 else 0)] = 1.0
    return jnp.asarray(v)


def _rope_tables(positions):
    inv = ROPE_THETA ** (-jnp.arange(0, MLA_ROPE, 2, dtype=F32) / MLA_ROPE)
    ang = positions.astype(F32).reshape(-1, 1) * inv
    z = lambda n: jnp.zeros((ang.shape[0], n), F32)
    cos, sin = jnp.cos(ang), jnp.sin(ang)
    return (jnp.concatenate([z(64), cos, cos, z(32)], axis=-1),
            jnp.concatenate([z(64), sin, sin, z(32)], axis=-1))


def _row_tile(n, want):
    t = min(n, want)
    assert n % t == 0, (n, t)
    return t


def kernel(x, positions, mix_norm, w_in, hg_lb_logits, hg_norm, mla_q_norm, mla_kv_norm, mla_w_uq, mla_w_ukv,
           ssm_conv_w, ssm_conv_b, ssm_dt_bias, ssm_a_log, ssm_d, ssm_norm, w_br_hg, w_br_mla, w_br_ssm,
           w_out, mlp_norm, w_up, w_down, final_norm):
    bsz, t, d = x.shape
    depth = w_in.shape[0]
    assert d == D_MODEL and t % SSD_CHUNK == 0
    m = bsz * t
    tm = _row_tile(m, ROW_TILE)
    tm_in = _row_tile(m, 2 * ROW_TILE)
    tq = _row_tile(t, ROW_TILE)
    tq_attn = _row_tile(t, 2 * ROW_TILE)

    lbs = jnp.cumsum(jax.nn.softmax(hg_lb_logits.astype(F32), axis=0), axis=0)
    lbs = lbs - lbs[0:1]
    w_in_p, w_gate = _pack_w_in(w_in)
    wq_p = _pack_w_uq(mla_w_uq)
    wkv_p = _pack_w_ukv(mla_w_ukv)
    vone = _v_ones()
    cos_t, sin_t = _rope_tables(positions)
    pad_h = lambda a: jnp.pad(a.astype(F32), ((0, 0), (DT_LANE, LANES - DT_LANE - SSM_HEADS)))
    dtb_p, alog_p = pad_h(ssm_dt_bias), pad_h(ssm_a_log)
    dsk_e = jnp.repeat(ssm_d.astype(F32), SSM_HEAD_DIM, axis=-1)
    bf = lambda a: a.astype(BF16)
    wa, wb, wc, wo, wu, wd = bf(w_br_hg), bf(w_br_mla), bf(w_br_ssm), bf(w_out), bf(w_up), bf(w_down)
    r1 = lambda a: a.reshape(1, -1).astype(F32)

    x2 = x.reshape(m, d)
    for l in range(depth):
        proj = _inproj(x2, r1(mix_norm[l]), w_in_p[l], tm_in, PROJ_COLS // 2)
        ya = _hgrn(proj, r1(lbs[l]), r1(hg_norm[l]), bsz, t, tq)
        q, k, v = _mla_prep(proj, cos_t, sin_t, r1(mla_q_norm[l]), r1(mla_kv_norm[l]), wq_p[l], wkv_p[l],
                            vone, tm)
        yb = _attn(q, k, v, bsz, t, tq_attn)
        yc = _ssd(proj, ssm_conv_w[l].astype(F32), r1(ssm_conv_b[l]), r1(dtb_p[l]), r1(alog_p[l]),
                  r1(dsk_e[l]), r1(ssm_norm[l]), bsz, t, tq)
        x2 = _merge(ya, yb, yc, x2, r1(mix_norm[l]), w_gate[l], wa[l], wb[l], wc[l], wo[l], tm)
        x2 = _mlp(x2, r1(mlp_norm[l]), wu[l], wd[l], r1(final_norm), tm, last_layer=(l == depth - 1))
    return x2.reshape(bsz, t, d)
```

```python
import functools
import math

import numpy as np
import jax
import jax.numpy as jnp
from jax import lax
from jax.experimental import pallas as pl
from jax.experimental.pallas import tpu as pltpu

F32 = jnp.float32
BF16 = jnp.bfloat16

D_MODEL = 1024
EPS = 1e-6
HG_HEADS = 4
HG_DK = 128
HG_DV = 128
HG_WIDTH = HG_HEADS * HG_DV
MLA_HEADS = 8
MLA_NOPE = 64
MLA_ROPE = 32
MLA_V = 64
MLA_Q_LORA = 384
MLA_KV_LORA = 256
MLA_WIDTH = MLA_HEADS * MLA_V
ROPE_THETA = 10000.0
SSM_HEADS = 8
SSM_HEAD_DIM = 64
SSM_WIDTH = SSM_HEADS * SSM_HEAD_DIM
SSM_GROUPS = 2
SSM_STATE = 64
SSM_CONV = 4
SSM_CONV_DIM = SSM_WIDTH + 2 * SSM_GROUPS * SSM_STATE
D_FF = 4 * D_MODEL
N_BRANCH = 3

LANES = 128
HEAD_PAD = 128

OFF_HQ, OFF_HF, OFF_HI, OFF_HG = 0, 512, 1024, 1536
OFF_SZ = 2048
OFF_MISC = 2560
DT_LANE = 32
OFF_CQ = 2688
OFF_SXBC = 3072
OFF_CKV = 3840
PROJ_COLS = 4096
HALF_COLS = PROJ_COLS // 2
MIXER_IN_COLS = 4008

HG_CHUNK = 64
HG_LEVELS = 6
SSD_CHUNK = 128
ATTN_HEADS = 4

ROW_TILE = 512
VMEM_LIMIT = 48 * 1024 * 1024


def _cparams(sem, flags=None):
    return pltpu.CompilerParams(dimension_semantics=sem, vmem_limit_bytes=VMEM_LIMIT, flags=flags)


def _sigmoid(x):
    return jax.nn.sigmoid(x)


def _split3(x):
    hi = x.astype(BF16)
    r1 = x - hi.astype(F32)
    mid = r1.astype(BF16)
    r2 = r1 - mid.astype(F32)
    return hi, mid, r2.astype(BF16)


def _dot(a, b):
    return jnp.dot(a, b, preferred_element_type=F32)


def _dot_nt(a, b):
    return lax.dot_general(a, b, (((1,), (1,)), ((), ())), preferred_element_type=F32)


N_SPLIT = 3


def _sel_rows(sel3, x):
    return _dot(sel3, jnp.concatenate(_split3(x), axis=0))


def _sel_cols(x, sel3):
    return _dot(jnp.concatenate(_split3(x), axis=1), sel3)


def _rms(x, g):
    ms = jnp.mean(x * x, axis=-1, keepdims=True)
    return x * lax.rsqrt(ms + EPS) * g


def _inproj_body(x_ref, g_ref, w_ref, o_ref):
    h = _rms(x_ref[...], g_ref[...]).astype(BF16)
    o_ref[...] = _dot(h, w_ref[...]).astype(o_ref.dtype)


def _inproj(x2, g, w, tm, tn):
    m, d = x2.shape
    n = w.shape[1]
    return pl.pallas_call(
        _inproj_body,
        grid=(n // tn, m // tm),
        in_specs=[pl.BlockSpec((tm, d), lambda j, i: (i, 0)),
                  pl.BlockSpec((1, d), lambda j, i: (0, 0)),
                  pl.BlockSpec((d, tn), lambda j, i: (0, j))],
        out_specs=pl.BlockSpec((tm, tn), lambda j, i: (i, j)),
        out_shape=jax.ShapeDtypeStruct((m, n), BF16),
        compiler_params=_cparams(("arbitrary", "arbitrary")),
        name="inproj",
    )(x2, g, w)


def _hgrn_level_matrix():
    c = HG_CHUNK
    mat = np.zeros((8, c, c), np.float32)
    for i in range(c):
        mat[0, i, : i + 1] = 1.0
        mat[7, i, i + 1:] = 1.0
    for lvl in range(HG_LEVELS):
        m = 1 << lvl
        for r in range(c):
            start = (r // (2 * m)) * 2 * m
            mid = start + m
            if r >= mid:
                mat[lvl + 1, r, mid: r + 1] = 1.0
            else:
                mat[lvl + 1, r, r + 1: mid] = 1.0
    return mat.reshape(8 * c, c)


def _hgrn_pair_masks():
    n = 2 * HG_CHUNK
    row, col = np.meshgrid(np.arange(n), np.arange(n), indexing="ij")
    out = [row == col]
    for lvl in range(HG_LEVELS):
        same_block = (row >> (lvl + 1)) == (col >> (lvl + 1))
        out.append(same_block & (((row >> lvl) & 1) == 1) & (((col >> lvl) & 1) == 0))
    return np.stack(out).astype(np.float32)


def _hgrn_body(x_ref, lb_ref, ng_ref, m_ref, mask_ref, o_ref, s_ref, *, n_chunks):
    c = HG_CHUNK

    @pl.when(pl.program_id(1) == 0)
    def _():
        s_ref[...] = jnp.zeros_like(s_ref)

    lb = lb_ref[...]
    log_lb = jnp.log(lb)
    log1m_lb = jnp.log1p(-lb)
    one_m_lb = 1.0 - lb
    ng = ng_ref[...]
    sel = m_ref[...]

    rowv = lax.broadcasted_iota(jnp.int32, (2 * c, HG_DK), 0)
    second_half = [((rowv >> lvl) & 1) == 1 for lvl in range(HG_LEVELS)]

    def chunk(ci, carry):
        r0 = pl.multiple_of(ci * c, c)
        rows = pl.ds(r0, c)
        slab = lambda off: x_ref[rows, off:off + HG_WIDTH].astype(F32)
        hq, ff, hv, hg = slab(OFF_HQ), slab(OFF_HF), slab(OFF_HI), slab(OFF_HG)
        q = hq * _sigmoid(hq)
        e_f = jnp.exp(-jnp.abs(ff))
        one_p = 1.0 + e_f
        log_sig = jnp.minimum(ff, 0.0) - jnp.log(one_p)
        b = log1m_lb + log_sig
        lf = jnp.maximum(log_lb, b) + jnp.log(1.0 + jnp.exp(-jnp.abs(log_lb - b)))
        k = one_m_lb * jnp.where(ff >= 0.0, e_f, 1.0) / one_p
        gate = hg * _sigmoid(hg)
        ex = _sel_rows(sel, lf)

        for pair in range(HG_HEADS // 2):
            sls = [slice(h * HG_DK, (h + 1) * HG_DK) for h in (2 * pair, 2 * pair + 1)]
            stack = lambda a, lo=0, hi=c: jnp.concatenate([a[lo:hi, sl] for sl in sls], axis=0)
            level = lambda n: stack(ex, n * c, (n + 1) * c)
            qp, kp, vp = stack(q).astype(BF16), stack(k).astype(BF16), stack(hv)
            decay_of = lambda n: jnp.exp(level(n).astype(BF16))
            att = mask_ref[0] * _dot(qp, stack(k).T.astype(BF16))
            for lvl in range(HG_LEVELS):
                xl = jnp.where(second_half[lvl], qp, kp) * decay_of(lvl + 1)
                att = att + mask_ref[lvl + 1] * _dot(xl, xl.astype(F32).T.astype(BF16))
            o_intra = _dot(att.astype(BF16), vp.astype(BF16))
            e0 = level(0)
            q_in = qp * jnp.exp(e0.astype(BF16))
            k_out = kp * decay_of(7)
            for idx, sl in enumerate(sls):
                h = 2 * pair + idx
                r = slice(idx * c, (idx + 1) * c)
                st = s_ref[h]
                o = o_intra[r] + _dot(q_in[r], st.T.astype(BF16))
                decay = jnp.exp(e0[(idx + 1) * c - 1:(idx + 1) * c, :])
                s_ref[h] = st * decay + _dot(vp[r].T.astype(BF16), k_out[r])
                y = _rms(o, ng[:, sl]) * gate[:, sl]
                o_ref[rows, sl] = y.astype(o_ref.dtype)
        return carry

    lax.fori_loop(0, n_chunks, chunk, 0, unroll=True)


def _hgrn(proj, lb, ng, bsz, t, tq):
    m = proj.shape[0]
    nq = t // tq
    sel = jnp.asarray(np.tile(_hgrn_level_matrix(), (1, N_SPLIT)), BF16)
    masks = jnp.asarray(_hgrn_pair_masks())
    w = HG_WIDTH

    vec = pl.BlockSpec((1, w), lambda b, i: (0, 0))
    return pl.pallas_call(
        functools.partial(_hgrn_body, n_chunks=tq // HG_CHUNK),
        grid=(bsz, nq),
        in_specs=[pl.BlockSpec((tq, HALF_COLS), lambda b, i: (b * nq + i, 0)), vec, vec,
                  pl.BlockSpec(sel.shape, lambda b, i: (0, 0)),
                  pl.BlockSpec(masks.shape, lambda b, i: (0, 0, 0))],
        out_specs=pl.BlockSpec((tq, w), lambda b, i: (b * nq + i, 0)),
        out_shape=jax.ShapeDtypeStruct((m, w), BF16),
        scratch_shapes=[pltpu.VMEM((HG_HEADS, HG_DV, HG_DK), F32)],
        compiler_params=_cparams(("arbitrary", "arbitrary")),
        name="hgrn2",
    )(proj, lb, ng, sel, masks)


def _mla_prep_body(cq_ref, ckv_ref, kr_ref, cos_ref, sin_ref, qn_ref, kvn_ref, wq_ref, wkv_ref, vone_ref,
                   q_out, k_out, v_out):
    scale = (MLA_NOPE + MLA_ROPE) ** -0.5 * math.log2(math.e)
    cqn = _rms(cq_ref[...].astype(F32), qn_ref[...]).astype(BF16)
    ckvn = _rms(ckv_ref[...].astype(F32), kvn_ref[...]).astype(BF16)
    qq = _dot(cqn, wq_ref[...])
    kv = _dot(ckvn, wkv_ref[...])
    cos = cos_ref[...]
    sin = sin_ref[...]
    lane = lax.broadcasted_iota(jnp.int32, (1, HEAD_PAD), 1)
    cq_t = scale * (cos + (lane < MLA_NOPE).astype(F32))
    sq_t = scale * sin
    krb = kr_ref[...].astype(F32)
    k_rope = krb * cos + pltpu.roll(krb, 64, 1) * sin
    hw = MLA_HEADS * HEAD_PAD
    vone = vone_ref[...]
    for h in range(MLA_HEADS):
        sl = slice(h * HEAD_PAD, (h + 1) * HEAD_PAD)
        sl2 = slice(hw + h * HEAD_PAD, hw + (h + 1) * HEAD_PAD)
        q_out[:, sl] = (qq[:, sl] * cq_t + qq[:, sl2] * sq_t).astype(BF16)
        k_out[:, sl] = (kv[:, sl] + k_rope).astype(BF16)
        v_out[:, sl] = (kv[:, sl2] + vone[:, sl]).astype(BF16)


def _mla_prep(proj, cos_t, sin_t, qn, kvn, wq, wkv, vone, tm):
    m = proj.shape[0]
    hw = MLA_HEADS * HEAD_PAD
    full = lambda a: pl.BlockSpec(a.shape, lambda i: (0, 0))
    out = jax.ShapeDtypeStruct((m, hw), BF16)
    ospec = pl.BlockSpec((tm, hw), lambda i: (i, 0))
    return pl.pallas_call(
        _mla_prep_body,
        grid=(m // tm,),
        in_specs=[pl.BlockSpec((tm, MLA_Q_LORA), lambda i: (i, OFF_CQ // MLA_Q_LORA)),
                  pl.BlockSpec((tm, MLA_KV_LORA), lambda i: (i, OFF_CKV // MLA_KV_LORA)),
                  pl.BlockSpec((tm, LANES), lambda i: (i, OFF_MISC // LANES)),
                  pl.BlockSpec((tm, LANES), lambda i: (i, 0)),
                  pl.BlockSpec((tm, LANES), lambda i: (i, 0)),
                  full(qn), full(kvn), full(wq), full(wkv), full(vone)],
        out_specs=[ospec, ospec, ospec],
        out_shape=[out, out, out],
        compiler_params=_cparams(("arbitrary",)),
        name="mla_prep",
    )(proj, proj, proj, cos_t, sin_t, qn, kvn, wq, wkv, vone)


def _attn_body(q_ref, k_ref, v_ref, o_ref, *, tq, tk):
    i = pl.program_id(2)
    lane = lax.broadcasted_iota(jnp.int32, (1, HEAD_PAD), 1)
    qpos = lax.broadcasted_iota(jnp.int32, (tq, tk), 0)
    kpos = lax.broadcasted_iota(jnp.int32, (tq, tk), 1)
    causal = kpos <= qpos
    heads = range(ATTN_HEADS)
    slices = [slice(h * HEAD_PAD, (h + 1) * HEAD_PAD) for h in heads]
    block = lambda j: pl.ds(pl.multiple_of(j * tk, tk), tk)

    def update(h, j, qrows, mask, m_prev, acc):
        s = _dot_nt(q_ref[qrows, slices[h]], k_ref[block(j), slices[h]])
        if mask is not None:
            s = jnp.where(mask, s, -jnp.inf)
        m_new = jnp.maximum(m_prev, jnp.max(s, axis=-1, keepdims=True))
        p = jnp.exp2((s - m_new).astype(BF16))
        return m_new, jnp.exp2(m_prev - m_new) * acc + _dot(p, v_ref[block(j), slices[h]])

    def step(j, state, mask=None):
        return tuple(update(h, j, slice(None), mask, *state[h]) for h in heads)

    def lower_half_step(j, state):
        new = []
        for h in heads:
            m, acc = state[h]
            m_lo, acc_lo = update(h, j, slice(tk, tq), causal[:tk], m[tk:], acc[tk:])
            new.append((jnp.concatenate([m[:tk], m_lo], axis=0), jnp.concatenate([acc[:tk], acc_lo], axis=0)))
        return tuple(new)

    def finish(state):
        for pair in range(ATTN_HEADS // 2):
            acc0, acc1 = state[2 * pair][1], state[2 * pair + 1][1]
            den0 = jnp.sum(jnp.where(lane == MLA_V, acc0, 0.0), axis=-1, keepdims=True)
            den1 = jnp.sum(jnp.where(lane == 0, acc1, 0.0), axis=-1, keepdims=True)
            o_ref[:, pair * HEAD_PAD:(pair + 1) * HEAD_PAD] = jnp.where(
                lane < MLA_V, acc0 / den0, acc1 / den1).astype(o_ref.dtype)

    def two_steps(j2, state):
        return step(2 * j2 + 1, step(2 * j2, state))

    init = (jnp.full((tq, 1), -jnp.inf, F32), jnp.zeros((tq, HEAD_PAD), F32))
    state = lax.fori_loop(0, i, two_steps, (init,) * ATTN_HEADS)
    finish(lower_half_step(2 * i + 1, step(2 * i, state, causal)))


def _attn(q, k, v, bsz, t, tq):
    m = q.shape[0]
    nq = t // tq
    pw = ATTN_HEADS * HEAD_PAD
    return pl.pallas_call(
        functools.partial(_attn_body, tq=tq, tk=tq // 2),
        grid=(bsz, MLA_HEADS // ATTN_HEADS, nq),
        in_specs=[pl.BlockSpec((tq, pw), lambda b, h, i: (b * nq + i, h)),
                  pl.BlockSpec((t, pw), lambda b, h, i: (b, h)),
                  pl.BlockSpec((t, pw), lambda b, h, i: (b, h))],
        out_specs=pl.BlockSpec((tq, ATTN_HEADS * MLA_V), lambda b, h, i: (b * nq + i, h)),
        out_shape=jax.ShapeDtypeStruct((m, MLA_WIDTH), BF16),
        compiler_params=_cparams(("arbitrary", "arbitrary", "arbitrary")),
        name="mla_attn",
    )(q, k, v)


def _softplus(x):
    return jnp.maximum(x, 0.0) + jnp.log1p(jnp.exp(-jnp.abs(x)))


def _ssd_body(x_ref, cw_ref, cb_ref, dtb_ref, alog_ref, dsk_ref, ng_ref, tril_ref, e8_ref,
              o_ref, xp_ref, xc_ref, s_ref, *, tq):
    L = SSD_CHUNK
    gn = SSM_GROUPS * SSM_STATE
    half = SSM_WIDTH // SSM_GROUPS
    t_idx = pl.program_id(1)

    @pl.when(t_idx == 0)
    def _():
        s_ref[...] = jnp.zeros_like(s_ref)
        xp_ref[0:8, :] = jnp.zeros((8, SSM_CONV_DIM), F32)

    @pl.when(t_idx > 0)
    def _():
        xp_ref[0:8, :] = xp_ref[tq:tq + 8, :]

    slab = lambda rows, off, width: x_ref[rows, off - HALF_COLS:off - HALF_COLS + width].astype(F32)
    xp_ref[8:8 + tq, :] = slab(slice(None), OFF_SXBC, SSM_CONV_DIM)
    cw = cw_ref[...]
    acc = cb_ref[...] + cw[3:4, :] * xp_ref[8:8 + tq, :]
    for w in range(SSM_CONV - 1):
        acc = acc + cw[w:w + 1, :] * xp_ref[5 + w:5 + w + tq, :]
    xc_ref[...] = acc * _sigmoid(acc)

    a_neg = -jnp.exp(alog_ref[...])
    dtb = dtb_ref[...]
    dsk = dsk_ref[...]
    ng = ng_ref[...]
    tril = tril_ref[...]
    e8 = e8_ref[...]
    ri = lax.broadcasted_iota(jnp.int32, (L, L), 0)
    ci = lax.broadcasted_iota(jnp.int32, (L, L), 1)
    tri = ci <= ri
    lane_gn = lax.broadcasted_iota(jnp.int32, (1, gn), 1)
    lane_w = lax.broadcasted_iota(jnp.int32, (1, SSM_WIDTH), 1)
    row_gn = lax.broadcasted_iota(jnp.int32, (gn, SSM_WIDTH), 0)
    col_w = lax.broadcasted_iota(jnp.int32, (gn, SSM_WIDTH), 1)
    blockdiag = (row_gn // SSM_STATE) == (col_w // half)
    head_mask = [(lane_w // SSM_HEAD_DIM) == h for h in range(SSM_HEADS)]

    def chunk(c, carry):
        rows = pl.ds(pl.multiple_of(c * L, L), L)
        xc = xc_ref[rows, :]
        xs = xc[:, :SSM_WIDTH]
        bm = xc[:, SSM_WIDTH:SSM_WIDTH + gn]
        cm = xc[:, SSM_WIDTH + gn:]
        dt = _softplus(slab(rows, OFF_MISC, LANES) + dtb)
        a = dt * a_neg
        acum = _sel_rows(tril, a)
        dt_e = _sel_cols(dt, e8)
        ac_e = _sel_cols(acum, e8)
        x_dt = xs * dt_e
        last = ac_e[L - 1:L, :]
        x_dec = (x_dt * jnp.exp(last - ac_e)).astype(BF16)
        x_b = x_dt.astype(BF16)
        bm_t = bm.T.astype(BF16)
        cb0 = _dot(jnp.where(lane_gn < SSM_STATE, cm, 0.0).astype(BF16), bm_t)
        cb1 = _dot(jnp.where(lane_gn >= SSM_STATE, cm, 0.0).astype(BF16), bm_t)
        ac_t = acum.T
        ws, xblk = [], []
        for h in range(SSM_HEADS):
            hl = DT_LANE + h
            seg = acum[:, hl:hl + 1] - ac_t[hl:hl + 1, :]
            lmat = jnp.exp(jnp.where(tri, seg, -jnp.inf))
            ws.append((lmat * (cb0 if h < SSM_HEADS // SSM_GROUPS else cb1)).astype(BF16))
            xblk.append(jnp.where(head_mask[h], x_b, jnp.zeros_like(x_b)))
        y = _dot(jnp.concatenate(ws, axis=1), jnp.concatenate(xblk, axis=0))
        st = s_ref[...]
        y = y + jnp.exp(ac_e) * _dot(cm.astype(BF16), st.astype(BF16))
        s_ref[...] = jnp.exp(last) * st + jnp.where(blockdiag, _dot(bm_t, x_dec), 0.0)
        y = y + xs * dsk
        z = slab(rows, OFF_SZ, SSM_WIDTH)
        y = y * (z * _sigmoid(z))
        y0 = _rms(y[:, :half], ng[:, :half])
        y1 = _rms(y[:, half:], ng[:, half:])
        o_ref[rows, :half] = y0.astype(o_ref.dtype)
        o_ref[rows, half:] = y1.astype(o_ref.dtype)
        return carry

    lax.fori_loop(0, tq // L, chunk, 0, unroll=True)


def _ssd(proj, cw, cb, dtb, alog, dsk, ng, bsz, t, tq):
    m = proj.shape[0]
    nq = t // tq
    L = SSD_CHUNK
    tril = jnp.asarray(np.tile(np.tril(np.ones((L, L), np.float32)), (1, N_SPLIT)), BF16)
    e8 = np.zeros((LANES, SSM_WIDTH), np.float32)
    for h in range(SSM_HEADS):
        e8[DT_LANE + h, h * SSM_HEAD_DIM:(h + 1) * SSM_HEAD_DIM] = 1.0
    e8 = jnp.asarray(np.tile(e8, (N_SPLIT, 1)), BF16)
    full = lambda a: pl.BlockSpec(a.shape, lambda b, i: (0, 0))
    return pl.pallas_call(
        functools.partial(_ssd_body, tq=tq),
        grid=(bsz, nq),
        in_specs=[pl.BlockSpec((tq, HALF_COLS), lambda b, i: (b * nq + i, 1)),
                  full(cw), full(cb), full(dtb), full(alog), full(dsk), full(ng), full(tril), full(e8)],
        out_specs=pl.BlockSpec((tq, SSM_WIDTH), lambda b, i: (b * nq + i, 0)),
        out_shape=jax.ShapeDtypeStruct((m, SSM_WIDTH), BF16),
        scratch_shapes=[pltpu.VMEM((tq + 8, SSM_CONV_DIM), F32),
                        pltpu.VMEM((tq, SSM_CONV_DIM), F32),
                        pltpu.VMEM((SSM_GROUPS * SSM_STATE, SSM_WIDTH), F32)],
        compiler_params=_cparams(("arbitrary", "arbitrary")),
        name="ssd",
    )(proj, cw, cb, dtb, alog, dsk, ng, tril, e8)


def _merge_body(ya_ref, yb_ref, yc_ref, x_ref, g_ref, wg_ref, wa_ref, wb_ref, wc_ref, wo_ref, o_ref):
    d = D_MODEL
    x = x_ref[...]
    h = _rms(x, g_ref[...]).astype(BF16)
    gate = lambda b: _sigmoid(_dot(h, wg_ref[:, b * d:(b + 1) * d]))
    merged = (gate(0) * _dot(ya_ref[...], wa_ref[...])
              + gate(1) * _dot(yb_ref[...], wb_ref[...])
              + gate(2) * _dot(yc_ref[...], wc_ref[...]))
    o_ref[...] = x + _dot(merged.astype(BF16), wo_ref[...])


def _merge(ya, yb, yc, x2, g, wg, wa, wb, wc, wo, tm):
    m, d = x2.shape
    full = lambda a: pl.BlockSpec(a.shape, lambda i: (0, 0))
    row = lambda w: pl.BlockSpec((tm, w), lambda i: (i, 0))
    return pl.pallas_call(
        _merge_body,
        grid=(m // tm,),
        in_specs=[row(HG_WIDTH), row(MLA_WIDTH), row(SSM_WIDTH), row(d),
                  full(g), full(wg), full(wa), full(wb), full(wc), full(wo)],
        out_specs=row(d),
        out_shape=jax.ShapeDtypeStruct((m, d), F32),
        compiler_params=_cparams(("arbitrary",)),
        name="merge_out",
    )(ya, yb, yc, x2, g, wg, wa, wb, wc, wo)


def _mlp_body(x_ref, g_ref, wu_ref, wd_ref, fg_ref, o_ref, *, last_layer):
    x = x_ref[...]
    h = _rms(x, g_ref[...]).astype(BF16)
    u = jnp.maximum(_dot(h, wu_ref[...]), 0.0)
    y = x + _dot((u * u).astype(BF16), wd_ref[...])
    o_ref[...] = _rms(y, fg_ref[...]) if last_layer else y


def _mlp(x2, g, wu, wd, final_g, tm, last_layer):
    m, d = x2.shape
    const = lambda a: pl.BlockSpec(a.shape, lambda i: (0, 0), pipeline_mode=pl.Buffered(1))
    return pl.pallas_call(
        functools.partial(_mlp_body, last_layer=last_layer),
        grid=(m // tm,),
        in_specs=[pl.BlockSpec((tm, d), lambda i: (i, 0)), const(g), const(wu), const(wd), const(final_g)],
        out_specs=pl.BlockSpec((tm, d), lambda i: (i, 0)),
        out_shape=jax.ShapeDtypeStruct((m, d), F32),
        compiler_params=_cparams(("arbitrary",)),
        name="mlp",
    )(x2, g, wu, wd, final_g)


def _rot_cols(w):
    hr = MLA_ROPE // 2
    return jnp.concatenate([-w[..., hr:], w[..., :hr]], axis=-1)


def _pack_w_in(w_in):
    dep, d, _ = w_in.shape
    sizes = (512, 512, 512, 512, MLA_Q_LORA, MLA_KV_LORA, MLA_ROPE, SSM_WIDTH, SSM_CONV_DIM, SSM_HEADS)
    assert sum(sizes) == MIXER_IN_COLS
    pts = np.cumsum(sizes)[:-1].tolist()
    mixers = w_in[..., :MIXER_IN_COLS].astype(BF16)
    hq, hf, hi, hg, cq, ckv, kr, sz, sxbc, sdt = jnp.split(mixers, pts, axis=-1)
    z = lambda n: jnp.zeros((dep, d, n), BF16)
    misc = jnp.concatenate([_rot_cols(kr), sdt, z(32 - SSM_HEADS), kr, z(32)], axis=-1)
    packed = jnp.concatenate([hq, hf, hi, hg, sz, misc, cq, sxbc, ckv], axis=-1)
    assert packed.shape[-1] == PROJ_COLS
    return packed, w_in[..., MIXER_IN_COLS:].astype(BF16)


def _pack_w_uq(w):
    dep, r, _ = w.shape
    wh = w.reshape(dep, r, MLA_HEADS, MLA_NOPE + MLA_ROPE)
    nope, rope = wh[..., :MLA_NOPE], wh[..., MLA_NOPE:]
    z = lambda n: jnp.zeros((dep, r, MLA_HEADS, n), w.dtype)
    a = jnp.concatenate([nope, rope, z(32)], axis=-1).reshape(dep, r, MLA_HEADS * HEAD_PAD)
    b = jnp.concatenate([z(64), _rot_cols(rope), z(32)], axis=-1).reshape(dep, r, MLA_HEADS * HEAD_PAD)
    return jnp.concatenate([a, b], axis=-1).astype(BF16)


def _pack_w_ukv(w):
    dep, r, _ = w.shape
    wh = w.reshape(dep, r, MLA_HEADS // 2, 2, MLA_NOPE + MLA_V)
    kn, v = wh[..., :MLA_NOPE], wh[..., MLA_NOPE:]
    z = jnp.zeros_like(kn)
    ka = jnp.concatenate([kn, z], axis=-1).reshape(dep, r, MLA_HEADS * HEAD_PAD)
    va = jnp.stack([jnp.concatenate([v[..., 0, :], z[..., 0, :]], axis=-1),
                    jnp.concatenate([z[..., 1, :], v[..., 1, :]], axis=-1)], axis=-2)
    va = va.reshape(dep, r, MLA_HEADS * HEAD_PAD)
    return jnp.concatenate([ka, va], axis=-1).astype(BF16)


def _v_ones():
    v = np.zeros((1, MLA_HEADS * HEAD_PAD), np.float32)
    for h in range(MLA_HEADS):
        v[0, h * HEAD_PAD + (MLA_V if h % 2 == 0 else 0)] = 1.0
    return jnp.asarray(v)


def _rope_tables(positions):
    inv = ROPE_THETA ** (-jnp.arange(0, MLA_ROPE, 2, dtype=F32) / MLA_ROPE)
    ang = positions.astype(F32).reshape(-1, 1) * inv
    z = lambda n: jnp.zeros((ang.shape[0], n), F32)
    cos, sin = jnp.cos(ang), jnp.sin(ang)
    return (jnp.concatenate([z(64), cos, cos, z(32)], axis=-1),
            jnp.concatenate([z(64), sin, sin, z(32)], axis=-1))


def _row_tile(n, want):
    t = min(n, want)
    assert n % t == 0, (n, t)
    return t


def kernel(x, positions, mix_norm, w_in, hg_lb_logits, hg_norm, mla_q_norm, mla_kv_norm, mla_w_uq, mla_w_ukv,
           ssm_conv_w, ssm_conv_b, ssm_dt_bias, ssm_a_log, ssm_d, ssm_norm, w_br_hg, w_br_mla, w_br_ssm,
           w_out, mlp_norm, w_up, w_down, final_norm):
    bsz, t, d = x.shape
    depth = w_in.shape[0]
    assert d == D_MODEL and t % SSD_CHUNK == 0
    m = bsz * t
    tm = _row_tile(m, ROW_TILE)
    tm_in = _row_tile(m, 2 * ROW_TILE)
    tq = _row_tile(t, ROW_TILE)
    tq_attn = _row_tile(t, 2 * ROW_TILE)

    lbs = jnp.cumsum(jax.nn.softmax(hg_lb_logits.astype(F32), axis=0), axis=0)
    lbs = lbs - lbs[0:1]
    w_in_p, w_gate = _pack_w_in(w_in)
    wq_p = _pack_w_uq(mla_w_uq)
    wkv_p = _pack_w_ukv(mla_w_ukv)
    vone = _v_ones()
    cos_t, sin_t = _rope_tables(positions)
    pad_h = lambda a: jnp.pad(a.astype(F32), ((0, 0), (DT_LANE, LANES - DT_LANE - SSM_HEADS)))
    dtb_p, alog_p = pad_h(ssm_dt_bias), pad_h(ssm_a_log)
    dsk_e = jnp.repeat(ssm_d.astype(F32), SSM_HEAD_DIM, axis=-1)
    bf = lambda a: a.astype(BF16)
    wa, wb, wc, wo, wu, wd = bf(w_br_hg), bf(w_br_mla), bf(w_br_ssm), bf(w_out), bf(w_up), bf(w_down)
    r1 = lambda a: a.reshape(1, -1).astype(F32)

    x2 = x.reshape(m, d)
    for l in range(depth):
        proj = _inproj(x2, r1(mix_norm[l]), w_in_p[l], tm_in, PROJ_COLS // 2)
        ya = _hgrn(proj, r1(lbs[l]), r1(hg_norm[l]), bsz, t, tq)
        q, k, v = _mla_prep(proj, cos_t, sin_t, r1(mla_q_norm[l]), r1(mla_kv_norm[l]), wq_p[l], wkv_p[l],
                            vone, tm_in)
        yb = _attn(q, k, v, bsz, t, tq_attn)
        yc = _ssd(proj, ssm_conv_w[l].astype(F32), r1(ssm_conv_b[l]), r1(dtb_p[l]), r1(alog_p[l]),
                  r1(dsk_e[l]), r1(ssm_norm[l]), bsz, t, tq)
        x2 = _merge(ya, yb, yc, x2, r1(mix_norm[l]), w_gate[l], wa[l], wb[l], wc[l], wo[l], tm)
        x2 = _mlp(x2, r1(mlp_norm[l]), wu[l], wd[l], r1(final_norm), tm, last_layer=(l == depth - 1))
    return x2.reshape(bsz, t, d)
```

```python
import functools
import math

import numpy as np
import jax
import jax.numpy as jnp
from jax import lax
from jax.experimental import pallas as pl
from jax.experimental.pallas import tpu as pltpu

F32 = jnp.float32
BF16 = jnp.bfloat16

D_MODEL = 1024
EPS = 1e-6
HG_HEADS = 4
HG_DK = 128
HG_DV = 128
HG_WIDTH = HG_HEADS * HG_DV
MLA_HEADS = 8
MLA_NOPE = 64
MLA_ROPE = 32
MLA_V = 64
MLA_Q_LORA = 384
MLA_KV_LORA = 256
MLA_WIDTH = MLA_HEADS * MLA_V
ROPE_THETA = 10000.0
SSM_HEADS = 8
SSM_HEAD_DIM = 64
SSM_WIDTH = SSM_HEADS * SSM_HEAD_DIM
SSM_GROUPS = 2
SSM_STATE = 64
SSM_CONV = 4
SSM_CONV_DIM = SSM_WIDTH + 2 * SSM_GROUPS * SSM_STATE

LANES = 128
HEAD_PAD = 128

OFF_HQ, OFF_HF, OFF_HI, OFF_HG = 0, 512, 1024, 1536
OFF_SZ = 2048
OFF_MISC = 2560
DT_LANE = 32
OFF_CQ = 2688
OFF_SXBC = 3072
OFF_CKV = 3840
PROJ_COLS = 4096
HALF_COLS = PROJ_COLS // 2
MIXER_IN_COLS = 4008

HG_CHUNK = 64
HG_LEVELS = 6
SSD_CHUNK = 128
ATTN_HEADS = 4

ROW_TILE = 512
VMEM_LIMIT = 48 * 1024 * 1024


def _cparams(sem):
    return pltpu.CompilerParams(dimension_semantics=sem, vmem_limit_bytes=VMEM_LIMIT)


def _sigmoid(x):
    return jax.nn.sigmoid(x)


def _split3(x):
    hi = x.astype(BF16)
    r1 = x - hi.astype(F32)
    mid = r1.astype(BF16)
    r2 = r1 - mid.astype(F32)
    return hi, mid, r2.astype(BF16)


def _dot(a, b):
    return jnp.dot(a, b, preferred_element_type=F32)


def _dot_nt(a, b):
    return lax.dot_general(a, b, (((1,), (1,)), ((), ())), preferred_element_type=F32)


N_SPLIT = 3


def _sel_rows(sel3, x):
    return _dot(sel3, jnp.concatenate(_split3(x), axis=0))


def _sel_cols(x, sel3):
    return _dot(jnp.concatenate(_split3(x), axis=1), sel3)


def _rms(x, g):
    ms = jnp.mean(x * x, axis=-1, keepdims=True)
    return x * lax.rsqrt(ms + EPS) * g


def _inproj_body(x_ref, g_ref, w_ref, o_ref):
    h = _rms(x_ref[...], g_ref[...]).astype(BF16)
    o_ref[...] = _dot(h, w_ref[...]).astype(o_ref.dtype)


def _inproj(x2, g, w, tm, tn):
    m, d = x2.shape
    n = w.shape[1]
    return pl.pallas_call(
        _inproj_body,
        grid=(n // tn, m // tm),
        in_specs=[pl.BlockSpec((tm, d), lambda j, i: (i, 0)),
                  pl.BlockSpec((1, d), lambda j, i: (0, 0)),
                  pl.BlockSpec((d, tn), lambda j, i: (0, j))],
        out_specs=pl.BlockSpec((tm, tn), lambda j, i: (i, j)),
        out_shape=jax.ShapeDtypeStruct((m, n), BF16),
        compiler_params=_cparams(("arbitrary", "arbitrary")),
        name="inproj",
    )(x2, g, w)


def _hgrn_level_matrix():
    c = HG_CHUNK
    mat = np.zeros((8, c, c), np.float32)
    for i in range(c):
        mat[0, i, : i + 1] = 1.0
        mat[7, i, i + 1:] = 1.0
    for lvl in range(HG_LEVELS):
        m = 1 << lvl
        for r in range(c):
            start = (r // (2 * m)) * 2 * m
            mid = start + m
            if r >= mid:
                mat[lvl + 1, r, mid: r + 1] = 1.0
            else:
                mat[lvl + 1, r, r + 1: mid] = 1.0
    return mat.reshape(8 * c, c)


def _hgrn_pair_masks():
    n = 2 * HG_CHUNK
    row, col = np.meshgrid(np.arange(n), np.arange(n), indexing="ij")
    out = [row == col]
    for lvl in range(HG_LEVELS):
        same_block = (row >> (lvl + 1)) == (col >> (lvl + 1))
        out.append(same_block & (((row >> lvl) & 1) == 1) & (((col >> lvl) & 1) == 0))
    return np.stack(out).astype(np.float32)


def _hgrn_body(x_ref, lb_ref, ng_ref, m_ref, mask_ref, o_ref, s_ref, *, n_chunks):
    c = HG_CHUNK

    @pl.when(pl.program_id(1) == 0)
    def _():
        s_ref[...] = jnp.zeros_like(s_ref)

    lb = lb_ref[...]
    log_lb = jnp.log(lb)
    log1m_lb = jnp.log1p(-lb)
    one_m_lb = 1.0 - lb
    ng = ng_ref[...]
    sel = m_ref[...]

    rowv = lax.broadcasted_iota(jnp.int32, (2 * c, HG_DK), 0)
    second_half = [((rowv >> lvl) & 1) == 1 for lvl in range(HG_LEVELS)]

    def chunk(ci, carry):
        r0 = pl.multiple_of(ci * c, c)
        rows = pl.ds(r0, c)
        slab = lambda off: x_ref[rows, off:off + HG_WIDTH].astype(F32)
        hq, ff, hv, hg = slab(OFF_HQ), slab(OFF_HF), slab(OFF_HI), slab(OFF_HG)
        q = hq * _sigmoid(hq)
        e_f = jnp.exp(-jnp.abs(ff))
        one_p = 1.0 + e_f
        log_sig = jnp.minimum(ff, 0.0) - jnp.log(one_p)
        b = log1m_lb + log_sig
        lf = jnp.maximum(log_lb, b) + jnp.log(1.0 + jnp.exp(-jnp.abs(log_lb - b)))
        k = one_m_lb * jnp.where(ff >= 0.0, e_f, 1.0) / one_p
        gate = hg * _sigmoid(hg)
        ex = _sel_rows(sel, lf)

        for pair in range(HG_HEADS // 2):
            sls = [slice(h * HG_DK, (h + 1) * HG_DK) for h in (2 * pair, 2 * pair + 1)]
            stack = lambda a, lo=0, hi=c: jnp.concatenate([a[lo:hi, sl] for sl in sls], axis=0)
            level = lambda n: stack(ex, n * c, (n + 1) * c)
            qp, kp, vp = stack(q).astype(BF16), stack(k).astype(BF16), stack(hv)
            decay_of = lambda n: jnp.exp(level(n).astype(BF16))
            att = mask_ref[0] * _dot(qp, stack(k).T.astype(BF16))
            for lvl in range(HG_LEVELS):
                xl = jnp.where(second_half[lvl], qp, kp) * decay_of(lvl + 1)
                att = att + mask_ref[lvl + 1] * _dot(xl, xl.astype(F32).T.astype(BF16))
            o_intra = _dot(att.astype(BF16), vp.astype(BF16))
            e0 = level(0)
            q_in = qp * jnp.exp(e0.astype(BF16))
            k_out = kp * decay_of(7)
            for idx, sl in enumerate(sls):
                h = 2 * pair + idx
                r = slice(idx * c, (idx + 1) * c)
                st = s_ref[h]
                o = o_intra[r] + _dot(q_in[r], st.T.astype(BF16))
                decay = jnp.exp(e0[(idx + 1) * c - 1:(idx + 1) * c, :])
                s_ref[h] = st * decay + _dot(vp[r].T.astype(BF16), k_out[r])
                y = _rms(o, ng[:, sl]) * gate[:, sl]
                o_ref[rows, sl] = y.astype(o_ref.dtype)
        return carry

    lax.fori_loop(0, n_chunks, chunk, 0, unroll=True)


def _hgrn(proj, lb, ng, bsz, t, tq):
    m = proj.shape[0]
    nq = t // tq
    sel = jnp.asarray(np.tile(_hgrn_level_matrix(), (1, N_SPLIT)), BF16)
    masks = jnp.asarray(_hgrn_pair_masks())
    w = HG_WIDTH

    vec = pl.BlockSpec((1, w), lambda b, i: (0, 0))
    return pl.pallas_call(
        functools.partial(_hgrn_body, n_chunks=tq // HG_CHUNK),
        grid=(bsz, nq),
        in_specs=[pl.BlockSpec((tq, HALF_COLS), lambda b, i: (b * nq + i, 0)), vec, vec,
                  pl.BlockSpec(sel.shape, lambda b, i: (0, 0)),
                  pl.BlockSpec(masks.shape, lambda b, i: (0, 0, 0))],
        out_specs=pl.BlockSpec((tq, w), lambda b, i: (b * nq + i, 0)),
        out_shape=jax.ShapeDtypeStruct((m, w), BF16),
        scratch_shapes=[pltpu.VMEM((HG_HEADS, HG_DV, HG_DK), F32)],
        compiler_params=_cparams(("arbitrary", "arbitrary")),
        name="hgrn2",
    )(proj, lb, ng, sel, masks)


def _mla_prep_body(cq_ref, ckv_ref, kr_ref, cos_ref, sin_ref, qn_ref, kvn_ref, wq_ref, wkv_ref, vone_ref,
                   q_out, k_out, v_out):
    scale = (MLA_NOPE + MLA_ROPE) ** -0.5 * math.log2(math.e)
    cqn = _rms(cq_ref[...].astype(F32), qn_ref[...]).astype(BF16)
    ckvn = _rms(ckv_ref[...].astype(F32), kvn_ref[...]).astype(BF16)
    qq = _dot(cqn, wq_ref[...])
    kv = _dot(ckvn, wkv_ref[...])
    cos = cos_ref[...]
    sin = sin_ref[...]
    lane = lax.broadcasted_iota(jnp.int32, (1, HEAD_PAD), 1)
    cq_t = scale * (cos + (lane < MLA_NOPE).astype(F32))
    sq_t = scale * sin
    krb = kr_ref[...].astype(F32)
    k_rope = krb * cos + pltpu.roll(krb, 64, 1) * sin
    hw = MLA_HEADS * HEAD_PAD
    vone = vone_ref[...]
    for h in range(MLA_HEADS):
        sl = slice(h * HEAD_PAD, (h + 1) * HEAD_PAD)
        sl2 = slice(hw + h * HEAD_PAD, hw + (h + 1) * HEAD_PAD)
        q_out[:, sl] = (qq[:, sl] * cq_t + qq[:, sl2] * sq_t).astype(BF16)
        k_out[:, sl] = (kv[:, sl] + k_rope).astype(BF16)
        v_out[:, sl] = (kv[:, sl2] + vone[:, sl]).astype(BF16)


def _mla_prep(proj, cos_t, sin_t, qn, kvn, wq, wkv, vone, tm):
    m = proj.shape[0]
    hw = MLA_HEADS * HEAD_PAD
    full = lambda a: pl.BlockSpec(a.shape, lambda i: (0, 0))
    out = jax.ShapeDtypeStruct((m, hw), BF16)
    ospec = pl.BlockSpec((tm, hw), lambda i: (i, 0))
    return pl.pallas_call(
        _mla_prep_body,
        grid=(m // tm,),
        in_specs=[pl.BlockSpec((tm, MLA_Q_LORA), lambda i: (i, OFF_CQ // MLA_Q_LORA)),
                  pl.BlockSpec((tm, MLA_KV_LORA), lambda i: (i, OFF_CKV // MLA_KV_LORA)),
                  pl.BlockSpec((tm, LANES), lambda i: (i, OFF_MISC // LANES)),
                  pl.BlockSpec((tm, LANES), lambda i: (i, 0)),
                  pl.BlockSpec((tm, LANES), lambda i: (i, 0)),
                  full(qn), full(kvn), full(wq), full(wkv), full(vone)],
        out_specs=[ospec, ospec, ospec],
        out_shape=[out, out, out],
        compiler_params=_cparams(("arbitrary",)),
        name="mla_prep",
    )(proj, proj, proj, cos_t, sin_t, qn, kvn, wq, wkv, vone)


def _attn_body(q_ref, k_ref, v_ref, o_ref, *, tq, tk):
    i = pl.program_id(2)
    lane = lax.broadcasted_iota(jnp.int32, (1, HEAD_PAD), 1)
    qpos = lax.broadcasted_iota(jnp.int32, (tq, tk), 0)
    kpos = lax.broadcasted_iota(jnp.int32, (tq, tk), 1)
    causal = kpos <= qpos
    heads = range(ATTN_HEADS)
    slices = [slice(h * HEAD_PAD, (h + 1) * HEAD_PAD) for h in heads]
    block = lambda j: pl.ds(pl.multiple_of(j * tk, tk), tk)

    def update(h, j, qrows, mask, m_prev, acc):
        s = _dot_nt(q_ref[qrows, slices[h]], k_ref[block(j), slices[h]])
        if mask is not None:
            s = jnp.where(mask, s, -jnp.inf)
        m_new = jnp.maximum(m_prev, jnp.max(s, axis=-1, keepdims=True))
        p = jnp.exp2((s - m_new).astype(BF16))
        return m_new, jnp.exp2(m_prev - m_new) * acc + _dot(p, v_ref[block(j), slices[h]])

    def step(j, state, mask=None):
        return tuple(update(h, j, slice(None), mask, *state[h]) for h in heads)

    def lower_half_step(j, state):
        new = []
        for h in heads:
            m, acc = state[h]
            m_lo, acc_lo = update(h, j, slice(tk, tq), causal[:tk], m[tk:], acc[tk:])
            new.append((jnp.concatenate([m[:tk], m_lo], axis=0), jnp.concatenate([acc[:tk], acc_lo], axis=0)))
        return tuple(new)

    def finish(state):
        for pair in range(ATTN_HEADS // 2):
            acc0, acc1 = state[2 * pair][1], state[2 * pair + 1][1]
            den0 = jnp.sum(jnp.where(lane == MLA_V, acc0, 0.0), axis=-1, keepdims=True)
            den1 = jnp.sum(jnp.where(lane == 0, acc1, 0.0), axis=-1, keepdims=True)
            o_ref[:, pair * HEAD_PAD:(pair + 1) * HEAD_PAD] = jnp.where(
                lane < MLA_V, acc0 / den0, acc1 / den1).astype(o_ref.dtype)

    def two_steps(j2, state):
        return step(2 * j2 + 1, step(2 * j2, state))

    init = (jnp.full((tq, 1), -jnp.inf, F32), jnp.zeros((tq, HEAD_PAD), F32))
    state = lax.fori_loop(0, i, two_steps, (init,) * ATTN_HEADS)
    finish(lower_half_step(2 * i + 1, step(2 * i, state, causal)))


def _attn(q, k, v, bsz, t, tq):
    m = q.shape[0]
    nq = t // tq
    pw = ATTN_HEADS * HEAD_PAD
    return pl.pallas_call(
        functools.partial(_attn_body, tq=tq, tk=tq // 2),
        grid=(bsz, MLA_HEADS // ATTN_HEADS, nq),
        in_specs=[pl.BlockSpec((tq, pw), lambda b, h, i: (b * nq + i, h)),
                  pl.BlockSpec((t, pw), lambda b, h, i: (b, h)),
                  pl.BlockSpec((t, pw), lambda b, h, i: (b, h))],
        out_specs=pl.BlockSpec((tq, ATTN_HEADS * MLA_V), lambda b, h, i: (b * nq + i, h)),
        out_shape=jax.ShapeDtypeStruct((m, MLA_WIDTH), BF16),
        compiler_params=_cparams(("arbitrary", "arbitrary", "arbitrary")),
        name="mla_attn",
    )(q, k, v)


def _softplus(x):
    return jnp.maximum(x, 0.0) + jnp.log1p(jnp.exp(-jnp.abs(x)))


def _ssd_body(x_ref, cw_ref, cb_ref, dtb_ref, alog_ref, dsk_ref, ng_ref, tril_ref, e8_ref,
              o_ref, xp_ref, xc_ref, s_ref, *, tq):
    L = SSD_CHUNK
    gn = SSM_GROUPS * SSM_STATE
    half = SSM_WIDTH // SSM_GROUPS
    t_idx = pl.program_id(1)

    @pl.when(t_idx == 0)
    def _():
        s_ref[...] = jnp.zeros_like(s_ref)
        xp_ref[0:8, :] = jnp.zeros((8, SSM_CONV_DIM), F32)

    @pl.when(t_idx > 0)
    def _():
        xp_ref[0:8, :] = xp_ref[tq:tq + 8, :]

    slab = lambda rows, off, width: x_ref[rows, off - HALF_COLS:off - HALF_COLS + width].astype(F32)
    xp_ref[8:8 + tq, :] = slab(slice(None), OFF_SXBC, SSM_CONV_DIM)
    cw = cw_ref[...]
    acc = cb_ref[...] + cw[3:4, :] * xp_ref[8:8 + tq, :]
    for w in range(SSM_CONV - 1):
        acc = acc + cw[w:w + 1, :] * xp_ref[5 + w:5 + w + tq, :]
    xc_ref[...] = acc * _sigmoid(acc)

    a_neg = -jnp.exp(alog_ref[...])
    dtb = dtb_ref[...]
    dsk = dsk_ref[...]
    ng = ng_ref[...]
    tril = tril_ref[...]
    e8 = e8_ref[...]
    ri = lax.broadcasted_iota(jnp.int32, (L, L), 0)
    ci = lax.broadcasted_iota(jnp.int32, (L, L), 1)
    tri = ci <= ri
    lane_gn = lax.broadcasted_iota(jnp.int32, (1, gn), 1)
    lane_w = lax.broadcasted_iota(jnp.int32, (1, SSM_WIDTH), 1)
    row_gn = lax.broadcasted_iota(jnp.int32, (gn, SSM_WIDTH), 0)
    col_w = lax.broadcasted_iota(jnp.int32, (gn, SSM_WIDTH), 1)
    blockdiag = (row_gn // SSM_STATE) == (col_w // half)
    head_mask = [(lane_w // SSM_HEAD_DIM) == h for h in range(SSM_HEADS)]

    def chunk(c, carry):
        rows = pl.ds(pl.multiple_of(c * L, L), L)
        xc = xc_ref[rows, :]
        xs = xc[:, :SSM_WIDTH]
        bm = xc[:, SSM_WIDTH:SSM_WIDTH + gn]
        cm = xc[:, SSM_WIDTH + gn:]
        dt = _softplus(slab(rows, OFF_MISC, LANES) + dtb)
        a = dt * a_neg
        acum = _sel_rows(tril, a)
        dt_e = _sel_cols(dt, e8)
        ac_e = _sel_cols(acum, e8)
        x_dt = xs * dt_e
        last = ac_e[L - 1:L, :]
        x_dec = (x_dt * jnp.exp(last - ac_e)).astype(BF16)
        x_b = x_dt.astype(BF16)
        bm_t = bm.T.astype(BF16)
        cb0 = _dot(jnp.where(lane_gn < SSM_STATE, cm, 0.0).astype(BF16), bm_t)
        cb1 = _dot(jnp.where(lane_gn >= SSM_STATE, cm, 0.0).astype(BF16), bm_t)
        ac_t = acum.T
        ws, xblk = [], []
        for h in range(SSM_HEADS):
            hl = DT_LANE + h
            seg = acum[:, hl:hl + 1] - ac_t[hl:hl + 1, :]
            lmat = jnp.exp(jnp.where(tri, seg, -jnp.inf))
            ws.append((lmat * (cb0 if h < SSM_HEADS // SSM_GROUPS else cb1)).astype(BF16))
            xblk.append(jnp.where(head_mask[h], x_b, jnp.zeros_like(x_b)))
        y = _dot(jnp.concatenate(ws, axis=1), jnp.concatenate(xblk, axis=0))
        st = s_ref[...]
        y = y + jnp.exp(ac_e) * _dot(cm.astype(BF16), st.astype(BF16))
        s_ref[...] = jnp.exp(last) * st + jnp.where(blockdiag, _dot(bm_t, x_dec), 0.0)
        y = y + xs * dsk
        z = slab(rows, OFF_SZ, SSM_WIDTH)
        y = y * (z * _sigmoid(z))
        y0 = _rms(y[:, :half], ng[:, :half])
        y1 = _rms(y[:, half:], ng[:, half:])
        o_ref[rows, :half] = y0.astype(o_ref.dtype)
        o_ref[rows, half:] = y1.astype(o_ref.dtype)
        return carry

    lax.fori_loop(0, tq // L, chunk, 0, unroll=True)


def _ssd(proj, cw, cb, dtb, alog, dsk, ng, bsz, t, tq):
    m = proj.shape[0]
    nq = t // tq
    L = SSD_CHUNK
    tril = jnp.asarray(np.tile(np.tril(np.ones((L, L), np.float32)), (1, N_SPLIT)), BF16)
    e8 = np.zeros((LANES, SSM_WIDTH), np.float32)
    for h in range(SSM_HEADS):
        e8[DT_LANE + h, h * SSM_HEAD_DIM:(h + 1) * SSM_HEAD_DIM] = 1.0
    e8 = jnp.asarray(np.tile(e8, (N_SPLIT, 1)), BF16)
    full = lambda a: pl.BlockSpec(a.shape, lambda b, i: (0, 0))
    return pl.pallas_call(
        functools.partial(_ssd_body, tq=tq),
        grid=(bsz, nq),
        in_specs=[pl.BlockSpec((tq, HALF_COLS), lambda b, i: (b * nq + i, 1)),
                  full(cw), full(cb), full(dtb), full(alog), full(dsk), full(ng), full(tril), full(e8)],
        out_specs=pl.BlockSpec((tq, SSM_WIDTH), lambda b, i: (b * nq + i, 0)),
        out_shape=jax.ShapeDtypeStruct((m, SSM_WIDTH), BF16),
        scratch_shapes=[pltpu.VMEM((tq + 8, SSM_CONV_DIM), F32),
                        pltpu.VMEM((tq, SSM_CONV_DIM), F32),
                        pltpu.VMEM((SSM_GROUPS * SSM_STATE, SSM_WIDTH), F32)],
        compiler_params=_cparams(("arbitrary", "arbitrary")),
        name="ssd",
    )(proj, cw, cb, dtb, alog, dsk, ng, tril, e8)


def _merge_body(ya_ref, yb_ref, yc_ref, x_ref, g_ref, wg_ref, wa_ref, wb_ref, wc_ref, wo_ref, o_ref):
    d = D_MODEL
    x = x_ref[...]
    h = _rms(x, g_ref[...]).astype(BF16)
    gate = lambda b: _sigmoid(_dot(h, wg_ref[:, b * d:(b + 1) * d]))
    merged = (gate(0) * _dot(ya_ref[...], wa_ref[...])
              + gate(1) * _dot(yb_ref[...], wb_ref[...])
              + gate(2) * _dot(yc_ref[...], wc_ref[...]))
    o_ref[...] = x + _dot(merged.astype(BF16), wo_ref[...])


def _merge(ya, yb, yc, x2, g, wg, wa, wb, wc, wo, tm):
    m, d = x2.shape
    full = lambda a: pl.BlockSpec(a.shape, lambda i: (0, 0))
    row = lambda w: pl.BlockSpec((tm, w), lambda i: (i, 0))
    return pl.pallas_call(
        _merge_body,
        grid=(m // tm,),
        in_specs=[row(HG_WIDTH), row(MLA_WIDTH), row(SSM_WIDTH), row(d),
                  full(g), full(wg), full(wa), full(wb), full(wc), full(wo)],
        out_specs=row(d),
        out_shape=jax.ShapeDtypeStruct((m, d), F32),
        compiler_params=_cparams(("arbitrary",)),
        name="merge_out",
    )(ya, yb, yc, x2, g, wg, wa, wb, wc, wo)


def _mlp_body(x_ref, g_ref, wu_ref, wd_ref, fg_ref, o_ref, *, last_layer):
    x = x_ref[...]
    h = _rms(x, g_ref[...]).astype(BF16)
    u = jnp.maximum(_dot(h, wu_ref[...]), 0.0)
    y = x + _dot((u * u).astype(BF16), wd_ref[...])
    o_ref[...] = _rms(y, fg_ref[...]) if last_layer else y


def _mlp(x2, g, wu, wd, final_g, tm, last_layer):
    m, d = x2.shape
    const = lambda a: pl.BlockSpec(a.shape, lambda i: (0, 0), pipeline_mode=pl.Buffered(1))
    return pl.pallas_call(
        functools.partial(_mlp_body, last_layer=last_layer),
        grid=(m // tm,),
        in_specs=[pl.BlockSpec((tm, d), lambda i: (i, 0)), const(g), const(wu), const(wd), const(final_g)],
        out_specs=pl.BlockSpec((tm, d), lambda i: (i, 0)),
        out_shape=jax.ShapeDtypeStruct((m, d), F32),
        compiler_params=_cparams(("arbitrary",)),
        name="mlp",
    )(x2, g, wu, wd, final_g)


def _rot_cols(w):
    hr = MLA_ROPE // 2
    return jnp.concatenate([-w[..., hr:], w[..., :hr]], axis=-1)


def _pack_w_in(w_in):
    dep, d, _ = w_in.shape
    sizes = (512, 512, 512, 512, MLA_Q_LORA, MLA_KV_LORA, MLA_ROPE, SSM_WIDTH, SSM_CONV_DIM, SSM_HEADS)
    assert sum(sizes) == MIXER_IN_COLS
    pts = np.cumsum(sizes)[:-1].tolist()
    mixers = w_in[..., :MIXER_IN_COLS].astype(BF16)
    hq, hf, hi, hg, cq, ckv, kr, sz, sxbc, sdt = jnp.split(mixers, pts, axis=-1)
    z = lambda n: jnp.zeros((dep, d, n), BF16)
    misc = jnp.concatenate([_rot_cols(kr), sdt, z(32 - SSM_HEADS), kr, z(32)], axis=-1)
    packed = jnp.concatenate([hq, hf, hi, hg, sz, misc, cq, sxbc, ckv], axis=-1)
    assert packed.shape[-1] == PROJ_COLS
    return packed, w_in[..., MIXER_IN_COLS:].astype(BF16)


def _pack_w_uq(w):
    dep, r, _ = w.shape
    wh = w.reshape(dep, r, MLA_HEADS, MLA_NOPE + MLA_ROPE)
    nope, rope = wh[..., :MLA_NOPE], wh[..., MLA_NOPE:]
    z = lambda n: jnp.zeros((dep, r, MLA_HEADS, n), w.dtype)
    a = jnp.concatenate([nope, rope, z(32)], axis=-1).reshape(dep, r, MLA_HEADS * HEAD_PAD)
    b = jnp.concatenate([z(64), _rot_cols(rope), z(32)], axis=-1).reshape(dep, r, MLA_HEADS * HEAD_PAD)
    return jnp.concatenate([a, b], axis=-1).astype(BF16)


def _pack_w_ukv(w):
    dep, r, _ = w.shape
    wh = w.reshape(dep, r, MLA_HEADS // 2, 2, MLA_NOPE + MLA_V)
    kn, v = wh[..., :MLA_NOPE], wh[..., MLA_NOPE:]
    z = jnp.zeros_like(kn)
    ka = jnp.concatenate([kn, z], axis=-1).reshape(dep, r, MLA_HEADS * HEAD_PAD)
    va = jnp.stack([jnp.concatenate([v[..., 0, :], z[..., 0, :]], axis=-1),
                    jnp.concatenate([z[..., 1, :], v[..., 1, :]], axis=-1)], axis=-2)
    va = va.reshape(dep, r, MLA_HEADS * HEAD_PAD)
    return jnp.concatenate([ka, va], axis=-1).astype(BF16)


def _v_ones():
    v = np.zeros((1, MLA_HEADS * HEAD_PAD), np.float32)
    for h in range(MLA_HEADS):
        v[0, h * HEAD_PAD + (MLA_V if h % 2 == 0 else 0)] = 1.0
    return jnp.asarray(v)


def _rope_tables(positions):
    inv = ROPE_THETA ** (-jnp.arange(0, MLA_ROPE, 2, dtype=F32) / MLA_ROPE)
    ang = positions.astype(F32).reshape(-1, 1) * inv
    z = lambda n: jnp.zeros((ang.shape[0], n), F32)
    cos, sin = jnp.cos(ang), jnp.sin(ang)
    return (jnp.concatenate([z(64), cos, cos, z(32)], axis=-1),
            jnp.concatenate([z(64), sin, sin, z(32)], axis=-1))


def _row_tile(n, want):
    t = min(n, want)
    assert n % t == 0, (n, t)
    return t


def kernel(x, positions, mix_norm, w_in, hg_lb_logits, hg_norm, mla_q_norm, mla_kv_norm, mla_w_uq, mla_w_ukv,
           ssm_conv_w, ssm_conv_b, ssm_dt_bias, ssm_a_log, ssm_d, ssm_norm, w_br_hg, w_br_mla, w_br_ssm,
           w_out, mlp_norm, w_up, w_down, final_norm):
    bsz, t, d = x.shape
    depth = w_in.shape[0]
    assert d == D_MODEL and t % SSD_CHUNK == 0
    m = bsz * t
    tm = _row_tile(m, ROW_TILE)
    tm_in = _row_tile(m, 2 * ROW_TILE)
    tq = _row_tile(t, ROW_TILE)
    tq_attn = _row_tile(t, 2 * ROW_TILE)

    lbs = jnp.cumsum(jax.nn.softmax(hg_lb_logits.astype(F32), axis=0), axis=0)
    lbs = lbs - lbs[0:1]
    w_in_p, w_gate = _pack_w_in(w_in)
    wq_p = _pack_w_uq(mla_w_uq)
    wkv_p = _pack_w_ukv(mla_w_ukv)
    vone = _v_ones()
    cos_t, sin_t = _rope_tables(positions)
    pad_h = lambda a: jnp.pad(a.astype(F32), ((0, 0), (DT_LANE, LANES - DT_LANE - SSM_HEADS)))
    dtb_p, alog_p = pad_h(ssm_dt_bias), pad_h(ssm_a_log)
    dsk_e = jnp.repeat(ssm_d.astype(F32), SSM_HEAD_DIM, axis=-1)
    bf = lambda a: a.astype(BF16)
    wa, wb, wc, wo, wu, wd = bf(w_br_hg), bf(w_br_mla), bf(w_br_ssm), bf(w_out), bf(w_up), bf(w_down)
    r1 = lambda a: a.reshape(1, -1).astype(F32)

    x2 = x.reshape(m, d)
    for l in range(depth):
        proj = _inproj(x2, r1(mix_norm[l]), w_in_p[l], tm_in, PROJ_COLS // 2)
        ya = _hgrn(proj, r1(lbs[l]), r1(hg_norm[l]), bsz, t, tq)
        q, k, v = _mla_prep(proj, cos_t, sin_t, r1(mla_q_norm[l]), r1(mla_kv_norm[l]), wq_p[l], wkv_p[l],
                            vone, tm_in)
        yb = _attn(q, k, v, bsz, t, tq_attn)
        yc = _ssd(proj, ssm_conv_w[l].astype(F32), r1(ssm_conv_b[l]), r1(dtb_p[l]), r1(alog_p[l]),
                  r1(dsk_e[l]), r1(ssm_norm[l]), bsz, t, tq)
        x2 = _merge(ya, yb, yc, x2, r1(mix_norm[l]), w_gate[l], wa[l], wb[l], wc[l], wo[l], tm)
        x2 = _mlp(x2, r1(mlp_norm[l]), wu[l], wd[l], r1(final_norm), tm, last_layer=(l == depth - 1))
    return x2.reshape(bsz, t, d)
```

```python
import functools
import math

import numpy as np
import jax
import jax.numpy as jnp
from jax import lax
from jax.experimental import pallas as pl
from jax.experimental.pallas import tpu as pltpu

F32 = jnp.float32
BF16 = jnp.bfloat16

D_MODEL = 1024
EPS = 1e-6
HG_HEADS = 4
HG_DK = 128
HG_DV = 128
HG_WIDTH = HG_HEADS * HG_DV
MLA_HEADS = 8
MLA_NOPE = 64
MLA_ROPE = 32
MLA_V = 64
MLA_Q_LORA = 384
MLA_KV_LORA = 256
MLA_WIDTH = MLA_HEADS * MLA_V
ROPE_THETA = 10000.0
SSM_HEADS = 8
SSM_HEAD_DIM = 64
SSM_WIDTH = SSM_HEADS * SSM_HEAD_DIM
SSM_GROUPS = 2
SSM_STATE = 64
SSM_CONV = 4
SSM_CONV_DIM = SSM_WIDTH + 2 * SSM_GROUPS * SSM_STATE

LANES = 128
HEAD_PAD = 128

OFF_HQ, OFF_HF, OFF_HI, OFF_HG = 0, 512, 1024, 1536
OFF_SZ = 2048
OFF_MISC = 2560
DT_LANE = 32
OFF_CQ = 2688
OFF_SXBC = 3072
OFF_CKV = 3840
PROJ_COLS = 4096
HALF_COLS = PROJ_COLS // 2
MIXER_IN_COLS = 4008

HG_CHUNK = 64
HG_LEVELS = 6
SSD_CHUNK = 128
ATTN_HEADS = 4

ROW_TILE = 512
VMEM_LIMIT = 48 * 1024 * 1024


def _cparams(sem):
    return pltpu.CompilerParams(dimension_semantics=sem, vmem_limit_bytes=VMEM_LIMIT)


def _sigmoid(x):
    return jax.nn.sigmoid(x)


def _split3(x):
    hi = x.astype(BF16)
    r1 = x - hi.astype(F32)
    mid = r1.astype(BF16)
    r2 = r1 - mid.astype(F32)
    return hi, mid, r2.astype(BF16)


def _dot(a, b):
    return jnp.dot(a, b, preferred_element_type=F32)


def _dot_nt(a, b):
    return lax.dot_general(a, b, (((1,), (1,)), ((), ())), preferred_element_type=F32)


N_SPLIT = 3


def _sel_rows(sel3, x):
    return _dot(sel3, jnp.concatenate(_split3(x), axis=0))


def _sel_cols(x, sel3):
    return _dot(jnp.concatenate(_split3(x), axis=1), sel3)


def _rms(x, g):
    ms = jnp.mean(x * x, axis=-1, keepdims=True)
    return x * lax.rsqrt(ms + EPS) * g


def _inproj_body(x_ref, g_ref, w_ref, o_ref):
    h = _rms(x_ref[...], g_ref[...]).astype(BF16)
    o_ref[...] = _dot(h, w_ref[...]).astype(o_ref.dtype)


def _inproj(x2, g, w, tm, tn):
    m, d = x2.shape
    n = w.shape[1]
    return pl.pallas_call(
        _inproj_body,
        grid=(n // tn, m // tm),
        in_specs=[pl.BlockSpec((tm, d), lambda j, i: (i, 0)),
                  pl.BlockSpec((1, d), lambda j, i: (0, 0)),
                  pl.BlockSpec((d, tn), lambda j, i: (0, j))],
        out_specs=pl.BlockSpec((tm, tn), lambda j, i: (i, j)),
        out_shape=jax.ShapeDtypeStruct((m, n), BF16),
        compiler_params=_cparams(("arbitrary", "arbitrary")),
        name="inproj",
    )(x2, g, w)


def _hgrn_level_matrix():
    c = HG_CHUNK
    mat = np.zeros((8, c, c), np.float32)
    for i in range(c):
        mat[0, i, : i + 1] = 1.0
        mat[7, i, i + 1:] = 1.0
    for lvl in range(HG_LEVELS):
        m = 1 << lvl
        for r in range(c):
            start = (r // (2 * m)) * 2 * m
            mid = start + m
            if r >= mid:
                mat[lvl + 1, r, mid: r + 1] = 1.0
            else:
                mat[lvl + 1, r, r + 1: mid] = 1.0
    return mat.reshape(8 * c, c)


def _hgrn_pair_masks():
    n = 2 * HG_CHUNK
    row, col = np.meshgrid(np.arange(n), np.arange(n), indexing="ij")
    out = [row == col]
    for lvl in range(HG_LEVELS):
        same_block = (row >> (lvl + 1)) == (col >> (lvl + 1))
        out.append(same_block & (((row >> lvl) & 1) == 1) & (((col >> lvl) & 1) == 0))
    return np.stack(out).astype(np.float32)


def _hgrn_body(x_ref, lb_ref, ng_ref, m_ref, mask_ref, o_ref, s_ref, *, n_chunks):
    c = HG_CHUNK

    @pl.when(pl.program_id(1) == 0)
    def _():
        s_ref[...] = jnp.zeros_like(s_ref)

    lb = lb_ref[...]
    log_lb = jnp.log(lb)
    log1m_lb = jnp.log1p(-lb)
    one_m_lb = 1.0 - lb
    ng = ng_ref[...]
    sel = m_ref[...]

    rowv = lax.broadcasted_iota(jnp.int32, (2 * c, HG_DK), 0)
    second_half = [((rowv >> lvl) & 1) == 1 for lvl in range(HG_LEVELS)]

    def chunk(ci, carry):
        r0 = pl.multiple_of(ci * c, c)
        rows = pl.ds(r0, c)
        slab = lambda off: x_ref[rows, off:off + HG_WIDTH].astype(F32)
        hq, ff, hv, hg = slab(OFF_HQ), slab(OFF_HF), slab(OFF_HI), slab(OFF_HG)
        q = hq * _sigmoid(hq)
        e_f = jnp.exp(-jnp.abs(ff))
        one_p = 1.0 + e_f
        log_sig = jnp.minimum(ff, 0.0) - jnp.log(one_p)
        b = log1m_lb + log_sig
        lf = jnp.maximum(log_lb, b) + jnp.log(1.0 + jnp.exp(-jnp.abs(log_lb - b)))
        k = one_m_lb * jnp.where(ff >= 0.0, e_f, 1.0) / one_p
        gate = hg * _sigmoid(hg)
        ex = _sel_rows(sel, lf)

        for pair in range(HG_HEADS // 2):
            sls = [slice(h * HG_DK, (h + 1) * HG_DK) for h in (2 * pair, 2 * pair + 1)]
            stack = lambda a, lo=0, hi=c: jnp.concatenate([a[lo:hi, sl] for sl in sls], axis=0)
            level = lambda n: stack(ex, n * c, (n + 1) * c)
            qp, kp, vp = stack(q).astype(BF16), stack(k).astype(BF16), stack(hv)
            decay_of = lambda n: jnp.exp(level(n).astype(BF16))
            att = mask_ref[0] * _dot(qp, stack(k).T.astype(BF16))
            for lvl in range(HG_LEVELS):
                xl = jnp.where(second_half[lvl], qp, kp) * decay_of(lvl + 1)
                att = att + mask_ref[lvl + 1] * _dot(xl, xl.astype(F32).T.astype(BF16))
            o_intra = _dot(att.astype(BF16), vp.astype(BF16))
            e0 = level(0)
            q_in = qp * jnp.exp(e0.astype(BF16))
            k_out = kp * decay_of(7)
            for idx, sl in enumerate(sls):
                h = 2 * pair + idx
                r = slice(idx * c, (idx + 1) * c)
                st = s_ref[h]
                o = o_intra[r] + _dot(q_in[r], st.T.astype(BF16))
                decay = jnp.exp(e0[(idx + 1) * c - 1:(idx + 1) * c, :])
                s_ref[h] = st * decay + _dot(vp[r].T.astype(BF16), k_out[r])
                y = _rms(o, ng[:, sl]) * gate[:, sl]
                o_ref[rows, sl] = y.astype(o_ref.dtype)
        return carry

    lax.fori_loop(0, n_chunks, chunk, 0, unroll=True)


def _hgrn(proj, lb, ng, bsz, t, tq):
    m = proj.shape[0]
    nq = t // tq
    sel = jnp.asarray(np.tile(_hgrn_level_matrix(), (1, N_SPLIT)), BF16)
    masks = jnp.asarray(_hgrn_pair_masks())
    w = HG_WIDTH

    vec = pl.BlockSpec((1, w), lambda b, i: (0, 0))
    return pl.pallas_call(
        functools.partial(_hgrn_body, n_chunks=tq // HG_CHUNK),
        grid=(bsz, nq),
        in_specs=[pl.BlockSpec((tq, HALF_COLS), lambda b, i: (b * nq + i, 0)), vec, vec,
                  pl.BlockSpec(sel.shape, lambda b, i: (0, 0)),
                  pl.BlockSpec(masks.shape, lambda b, i: (0, 0, 0))],
        out_specs=pl.BlockSpec((tq, w), lambda b, i: (b * nq + i, 0)),
        out_shape=jax.ShapeDtypeStruct((m, w), BF16),
        scratch_shapes=[pltpu.VMEM((HG_HEADS, HG_DV, HG_DK), F32)],
        compiler_params=_cparams(("arbitrary", "arbitrary")),
        name="hgrn2",
    )(proj, lb, ng, sel, masks)


def _mla_prep_body(cq_ref, ckv_ref, kr_ref, cos_ref, sin_ref, qn_ref, kvn_ref, wq_ref, wkv_ref, vone_ref,
                   q_out, k_out, v_out):
    scale = (MLA_NOPE + MLA_ROPE) ** -0.5 * math.log2(math.e)
    cqn = _rms(cq_ref[...].astype(F32), qn_ref[...]).astype(BF16)
    ckvn = _rms(ckv_ref[...].astype(F32), kvn_ref[...]).astype(BF16)
    qq = _dot(cqn, wq_ref[...])
    kv = _dot(ckvn, wkv_ref[...])
    cos = cos_ref[...]
    sin = sin_ref[...]
    lane = lax.broadcasted_iota(jnp.int32, (1, HEAD_PAD), 1)
    cq_t = scale * (cos + (lane < MLA_NOPE).astype(F32))
    sq_t = scale * sin
    krb = kr_ref[...].astype(F32)
    k_rope = krb * cos + pltpu.roll(krb, 64, 1) * sin
    hw = MLA_HEADS * HEAD_PAD
    vone = vone_ref[...]
    for h in range(MLA_HEADS):
        sl = slice(h * HEAD_PAD, (h + 1) * HEAD_PAD)
        sl2 = slice(hw + h * HEAD_PAD, hw + (h + 1) * HEAD_PAD)
        q_out[:, sl] = (qq[:, sl] * cq_t + qq[:, sl2] * sq_t).astype(BF16)
        k_out[:, sl] = (kv[:, sl] + k_rope).astype(BF16)
        v_out[:, sl] = (kv[:, sl2] + vone[:, sl]).astype(BF16)


def _mla_prep(proj, cos_t, sin_t, qn, kvn, wq, wkv, vone, tm):
    m = proj.shape[0]
    hw = MLA_HEADS * HEAD_PAD
    full = lambda a: pl.BlockSpec(a.shape, lambda i: (0, 0))
    out = jax.ShapeDtypeStruct((m, hw), BF16)
    ospec = pl.BlockSpec((tm, hw), lambda i: (i, 0))
    return pl.pallas_call(
        _mla_prep_body,
        grid=(m // tm,),
        in_specs=[pl.BlockSpec((tm, MLA_Q_LORA), lambda i: (i, OFF_CQ // MLA_Q_LORA)),
                  pl.BlockSpec((tm, MLA_KV_LORA), lambda i: (i, OFF_CKV // MLA_KV_LORA)),
                  pl.BlockSpec((tm, LANES), lambda i: (i, OFF_MISC // LANES)),
                  pl.BlockSpec((tm, LANES), lambda i: (i, 0)),
                  pl.BlockSpec((tm, LANES), lambda i: (i, 0)),
                  full(qn), full(kvn), full(wq), full(wkv), full(vone)],
        out_specs=[ospec, ospec, ospec],
        out_shape=[out, out, out],
        compiler_params=_cparams(("arbitrary",)),
        name="mla_prep",
    )(proj, proj, proj, cos_t, sin_t, qn, kvn, wq, wkv, vone)


def _attn_body(q_ref, k_ref, v_ref, o_ref, *, tq, tk):
    i = pl.program_id(2)
    lane = lax.broadcasted_iota(jnp.int32, (1, HEAD_PAD), 1)
    qpos = lax.broadcasted_iota(jnp.int32, (tq, tk), 0)
    kpos = lax.broadcasted_iota(jnp.int32, (tq, tk), 1)
    causal = kpos <= qpos
    heads = range(ATTN_HEADS)
    slices = [slice(h * HEAD_PAD, (h + 1) * HEAD_PAD) for h in heads]
    block = lambda j: pl.ds(pl.multiple_of(j * tk, tk), tk)

    def update(h, j, qrows, mask, m_prev, acc):
        s = _dot_nt(q_ref[qrows, slices[h]], k_ref[block(j), slices[h]])
        if mask is not None:
            s = jnp.where(mask, s, -jnp.inf)
        m_new = jnp.maximum(m_prev, jnp.max(s, axis=-1, keepdims=True))
        p = jnp.exp2((s - m_new).astype(BF16))
        return m_new, jnp.exp2(m_prev - m_new) * acc + _dot(p, v_ref[block(j), slices[h]])

    def step(j, state, mask=None):
        return tuple(update(h, j, slice(None), mask, *state[h]) for h in heads)

    def lower_half_step(j, state):
        new = []
        for h in heads:
            m, acc = state[h]
            m_lo, acc_lo = update(h, j, slice(tk, tq), causal[:tk], m[tk:], acc[tk:])
            new.append((jnp.concatenate([m[:tk], m_lo], axis=0), jnp.concatenate([acc[:tk], acc_lo], axis=0)))
        return tuple(new)

    def finish(state):
        for pair in range(ATTN_HEADS // 2):
            acc0, acc1 = state[2 * pair][1], state[2 * pair + 1][1]
            den0 = jnp.sum(jnp.where(lane == MLA_V, acc0, 0.0), axis=-1, keepdims=True)
            den1 = jnp.sum(jnp.where(lane == 0, acc1, 0.0), axis=-1, keepdims=True)
            o_ref[:, pair * HEAD_PAD:(pair + 1) * HEAD_PAD] = jnp.where(
                lane < MLA_V, acc0 / den0, acc1 / den1).astype(o_ref.dtype)

    def two_steps(j2, state):
        return step(2 * j2 + 1, step(2 * j2, state))

    init = (jnp.full((tq, 1), -jnp.inf, F32), jnp.zeros((tq, HEAD_PAD), F32))
    state = lax.fori_loop(0, i, two_steps, (init,) * ATTN_HEADS)
    finish(lower_half_step(2 * i + 1, step(2 * i, state, causal)))


def _attn(q, k, v, bsz, t, tq):
    m = q.shape[0]
    nq = t // tq
    pw = ATTN_HEADS * HEAD_PAD
    return pl.pallas_call(
        functools.partial(_attn_body, tq=tq, tk=tq // 2),
        grid=(bsz, MLA_HEADS // ATTN_HEADS, nq),
        in_specs=[pl.BlockSpec((tq, pw), lambda b, h, i: (b * nq + i, h)),
                  pl.BlockSpec((t, pw), lambda b, h, i: (b, h)),
                  pl.BlockSpec((t, pw), lambda b, h, i: (b, h))],
        out_specs=pl.BlockSpec((tq, ATTN_HEADS * MLA_V), lambda b, h, i: (b * nq + i, h)),
        out_shape=jax.ShapeDtypeStruct((m, MLA_WIDTH), BF16),
        compiler_params=_cparams(("arbitrary", "arbitrary", "arbitrary")),
        name="mla_attn",
    )(q, k, v)


def _softplus(x):
    return jnp.maximum(x, 0.0) + jnp.log1p(jnp.exp(-jnp.abs(x)))


def _ssd_body(x_ref, cw_ref, cb_ref, dtb_ref, alog_ref, dsk_ref, ng_ref, tril_ref, e8_ref,
              o_ref, xp_ref, xc_ref, s_ref, *, tq):
    L = SSD_CHUNK
    gn = SSM_GROUPS * SSM_STATE
    half = SSM_WIDTH // SSM_GROUPS
    t_idx = pl.program_id(1)

    @pl.when(t_idx == 0)
    def _():
        s_ref[...] = jnp.zeros_like(s_ref)
        xp_ref[0:8, :] = jnp.zeros((8, SSM_CONV_DIM), F32)

    @pl.when(t_idx > 0)
    def _():
        xp_ref[0:8, :] = xp_ref[tq:tq + 8, :]

    slab = lambda rows, off, width: x_ref[rows, off - HALF_COLS:off - HALF_COLS + width].astype(F32)
    xp_ref[8:8 + tq, :] = slab(slice(None), OFF_SXBC, SSM_CONV_DIM)
    cw = cw_ref[...]
    acc = cb_ref[...] + cw[3:4, :] * xp_ref[8:8 + tq, :]
    for w in range(SSM_CONV - 1):
        acc = acc + cw[w:w + 1, :] * xp_ref[5 + w:5 + w + tq, :]
    xc_ref[...] = acc * _sigmoid(acc)

    a_neg = -jnp.exp(alog_ref[...])
    dtb = dtb_ref[...]
    dsk = dsk_ref[...]
    ng = ng_ref[...]
    tril = tril_ref[...]
    e8 = e8_ref[...]
    ri = lax.broadcasted_iota(jnp.int32, (L, L), 0)
    ci = lax.broadcasted_iota(jnp.int32, (L, L), 1)
    tri = ci <= ri
    lane_gn = lax.broadcasted_iota(jnp.int32, (1, gn), 1)
    lane_w = lax.broadcasted_iota(jnp.int32, (1, SSM_WIDTH), 1)
    row_gn = lax.broadcasted_iota(jnp.int32, (gn, SSM_WIDTH), 0)
    col_w = lax.broadcasted_iota(jnp.int32, (gn, SSM_WIDTH), 1)
    blockdiag = (row_gn // SSM_STATE) == (col_w // half)
    head_mask = [(lane_w // SSM_HEAD_DIM) == h for h in range(SSM_HEADS)]

    def chunk(c, carry):
        rows = pl.ds(pl.multiple_of(c * L, L), L)
        xc = xc_ref[rows, :]
        xs = xc[:, :SSM_WIDTH]
        bm = xc[:, SSM_WIDTH:SSM_WIDTH + gn]
        cm = xc[:, SSM_WIDTH + gn:]
        dt = _softplus(slab(rows, OFF_MISC, LANES) + dtb)
        a = dt * a_neg
        acum = _sel_rows(tril, a)
        dt_e = _sel_cols(dt, e8)
        ac_e = _sel_cols(acum, e8)
        x_dt = xs * dt_e
        last = ac_e[L - 1:L, :]
        x_dec = (x_dt * jnp.exp(last - ac_e)).astype(BF16)
        x_b = x_dt.astype(BF16)
        bm_t = bm.T.astype(BF16)
        cb0 = _dot(jnp.where(lane_gn < SSM_STATE, cm, 0.0).astype(BF16), bm_t)
        cb1 = _dot(jnp.where(lane_gn >= SSM_STATE, cm, 0.0).astype(BF16), bm_t)
        ac_t = acum.T
        ws, xblk = [], []
        for h in range(SSM_HEADS):
            hl = DT_LANE + h
            seg = acum[:, hl:hl + 1] - ac_t[hl:hl + 1, :]
            lmat = jnp.exp(jnp.where(tri, seg, -jnp.inf))
            ws.append((lmat * (cb0 if h < SSM_HEADS // SSM_GROUPS else cb1)).astype(BF16))
            xblk.append(jnp.where(head_mask[h], x_b, jnp.zeros_like(x_b)))
        y = _dot(jnp.concatenate(ws, axis=1), jnp.concatenate(xblk, axis=0))
        st = s_ref[...]
        y = y + jnp.exp(ac_e) * _dot(cm.astype(BF16), st.astype(BF16))
        s_ref[...] = jnp.exp(last) * st + jnp.where(blockdiag, _dot(bm_t, x_dec), 0.0)
        y = y + xs * dsk
        z = slab(rows, OFF_SZ, SSM_WIDTH)
        y = y * (z * _sigmoid(z))
        y0 = _rms(y[:, :half], ng[:, :half])
        y1 = _rms(y[:, half:], ng[:, half:])
        o_ref[rows, :half] = y0.astype(o_ref.dtype)
        o_ref[rows, half:] = y1.astype(o_ref.dtype)
        return carry

    lax.fori_loop(0, tq // L, chunk, 0, unroll=True)


def _ssd(proj, cw, cb, dtb, alog, dsk, ng, bsz, t, tq):
    m = proj.shape[0]
    nq = t // tq
    L = SSD_CHUNK
    tril = jnp.asarray(np.tile(np.tril(np.ones((L, L), np.float32)), (1, N_SPLIT)), BF16)
    e8 = np.zeros((LANES, SSM_WIDTH), np.float32)
    for h in range(SSM_HEADS):
        e8[DT_LANE + h, h * SSM_HEAD_DIM:(h + 1) * SSM_HEAD_DIM] = 1.0
    e8 = jnp.asarray(np.tile(e8, (N_SPLIT, 1)), BF16)
    full = lambda a: pl.BlockSpec(a.shape, lambda b, i: (0, 0))
    return pl.pallas_call(
        functools.partial(_ssd_body, tq=tq),
        grid=(bsz, nq),
        in_specs=[pl.BlockSpec((tq, HALF_COLS), lambda b, i: (b * nq + i, 1)),
                  full(cw), full(cb), full(dtb), full(alog), full(dsk), full(ng), full(tril), full(e8)],
        out_specs=pl.BlockSpec((tq, SSM_WIDTH), lambda b, i: (b * nq + i, 0)),
        out_shape=jax.ShapeDtypeStruct((m, SSM_WIDTH), BF16),
        scratch_shapes=[pltpu.VMEM((tq + 8, SSM_CONV_DIM), F32),
                        pltpu.VMEM((tq, SSM_CONV_DIM), F32),
                        pltpu.VMEM((SSM_GROUPS * SSM_STATE, SSM_WIDTH), F32)],
        compiler_params=_cparams(("arbitrary", "arbitrary")),
        name="ssd",
    )(proj, cw, cb, dtb, alog, dsk, ng, tril, e8)


def _merge_body(ya_ref, yb_ref, yc_ref, x_ref, g_ref, wg_ref, wa_ref, wb_ref, wc_ref, wo_ref, o_ref):
    d = D_MODEL
    x = x_ref[...]
    h = _rms(x, g_ref[...]).astype(BF16)
    gate = lambda b: _sigmoid(_dot(h, wg_ref[:, b * d:(b + 1) * d]))
    merged = (gate(0) * _dot(ya_ref[...], wa_ref[...])
              + gate(1) * _dot(yb_ref[...], wb_ref[...])
              + gate(2) * _dot(yc_ref[...], wc_ref[...]))
    o_ref[...] = x + _dot(merged.astype(BF16), wo_ref[...])


def _merge(ya, yb, yc, x2, g, wg, wa, wb, wc, wo, tm):
    m, d = x2.shape
    full = lambda a: pl.BlockSpec(a.shape, lambda i: (0, 0))
    row = lambda w: pl.BlockSpec((tm, w), lambda i: (i, 0))
    return pl.pallas_call(
        _merge_body,
        grid=(m // tm,),
        in_specs=[row(HG_WIDTH), row(MLA_WIDTH), row(SSM_WIDTH), row(d),
                  full(g), full(wg), full(wa), full(wb), full(wc), full(wo)],
        out_specs=row(d),
        out_shape=jax.ShapeDtypeStruct((m, d), F32),
        compiler_params=_cparams(("arbitrary",)),
        name="merge_out",
    )(ya, yb, yc, x2, g, wg, wa, wb, wc, wo)


def _mlp_body(x_ref, g_ref, wu_ref, wd_ref, fg_ref, o_ref, *, last_layer):
    x = x_ref[...]
    h = _rms(x, g_ref[...]).astype(BF16)
    u = jnp.maximum(_dot(h, wu_ref[...]), 0.0)
    y = x + _dot((u * u).astype(BF16), wd_ref[...])
    o_ref[...] = _rms(y, fg_ref[...]) if last_layer else y


def _mlp(x2, g, wu, wd, final_g, tm, last_layer):
    m, d = x2.shape
    const = lambda a: pl.BlockSpec(a.shape, lambda i: (0, 0), pipeline_mode=pl.Buffered(1))
    return pl.pallas_call(
        functools.partial(_mlp_body, last_layer=last_layer),
        grid=(m // tm,),
        in_specs=[pl.BlockSpec((tm, d), lambda i: (i, 0)), const(g), const(wu), const(wd), const(final_g)],
        out_specs=pl.BlockSpec((tm, d), lambda i: (i, 0)),
        out_shape=jax.ShapeDtypeStruct((m, d), F32),
        compiler_params=_cparams(("arbitrary",)),
        name="mlp",
    )(x2, g, wu, wd, final_g)


def _rot_cols(w):
    hr = MLA_ROPE // 2
    return jnp.concatenate([-w[..., hr:], w[..., :hr]], axis=-1)


def _pack_w_in(w_in):
    dep, d, _ = w_in.shape
    sizes = (512, 512, 512, 512, MLA_Q_LORA, MLA_KV_LORA, MLA_ROPE, SSM_WIDTH, SSM_CONV_DIM, SSM_HEADS)
    assert sum(sizes) == MIXER_IN_COLS
    pts = np.cumsum(sizes)[:-1].tolist()
    mixers = w_in[..., :MIXER_IN_COLS].astype(BF16)
    hq, hf, hi, hg, cq, ckv, kr, sz, sxbc, sdt = jnp.split(mixers, pts, axis=-1)
    z = lambda n: jnp.zeros((dep, d, n), BF16)
    misc = jnp.concatenate([_rot_cols(kr), sdt, z(32 - SSM_HEADS), kr, z(32)], axis=-1)
    packed = jnp.concatenate([hq, hf, hi, hg, sz, misc, cq, sxbc, ckv], axis=-1)
    assert packed.shape[-1] == PROJ_COLS
    return packed, w_in[..., MIXER_IN_COLS:].astype(BF16)


def _pack_w_uq(w):
    dep, r, _ = w.shape
    wh = w.reshape(dep, r, MLA_HEADS, MLA_NOPE + MLA_ROPE)
    nope, rope = wh[..., :MLA_NOPE], wh[..., MLA_NOPE:]
    z = lambda n: jnp.zeros((dep, r, MLA_HEADS, n), w.dtype)
    a = jnp.concatenate([nope, rope, z(32)], axis=-1).reshape(dep, r, MLA_HEADS * HEAD_PAD)
    b = jnp.concatenate([z(64), _rot_cols(rope), z(32)], axis=-1).reshape(dep, r, MLA_HEADS * HEAD_PAD)
    return jnp.concatenate([a, b], axis=-1).astype(BF16)


def _pack_w_ukv(w):
    dep, r, _ = w.shape
    wh = w.reshape(dep, r, MLA_HEADS // 2, 2, MLA_NOPE + MLA_V)
    kn, v = wh[..., :MLA_NOPE], wh[..., MLA_NOPE:]
    z = jnp.zeros_like(kn)
    ka = jnp.concatenate([kn, z], axis=-1).reshape(dep, r, MLA_HEADS * HEAD_PAD)
    va = jnp.stack([jnp.concatenate([v[..., 0, :], z[..., 0, :]], axis=-1),
                    jnp.concatenate([z[..., 1, :], v[..., 1, :]], axis=-1)], axis=-2)
    va = va.reshape(dep, r, MLA_HEADS * HEAD_PAD)
    return jnp.concatenate([ka, va], axis=-1).astype(BF16)


def _v_ones():
    v = np.zeros((1, MLA_HEADS * HEAD_PAD), np.float32)
    for h in range(MLA_HEADS):
        v[0, h * HEAD_PAD + (MLA_V if h % 2 == 0 else 0)] = 1.0
    return jnp.asarray(v)


def _rope_tables(positions):
    inv = ROPE_THETA ** (-jnp.arange(0, MLA_ROPE, 2, dtype=F32) / MLA_ROPE)
    ang = positions.astype(F32).reshape(-1, 1) * inv
    z = lambda n: jnp.zeros((ang.shape[0], n), F32)
    cos, sin = jnp.cos(ang), jnp.sin(ang)
    return (jnp.concatenate([z(64), cos, cos, z(32)], axis=-1),
            jnp.concatenate([z(64), sin, sin, z(32)], axis=-1))


def _row_tile(n, want):
    t = min(n, want)
    assert n % t == 0, (n, t)
    return t


def kernel(x, positions, mix_norm, w_in, hg_lb_logits, hg_norm, mla_q_norm, mla_kv_norm, mla_w_uq, mla_w_ukv,
           ssm_conv_w, ssm_conv_b, ssm_dt_bias, ssm_a_log, ssm_d, ssm_norm, w_br_hg, w_br_mla, w_br_ssm,
           w_out, mlp_norm, w_up, w_down, final_norm):
    bsz, t, d = x.shape
    depth = w_in.shape[0]
    assert d == D_MODEL and t % SSD_CHUNK == 0
    m = bsz * t
    tm = _row_tile(m, ROW_TILE)
    tm_in = _row_tile(m, 2 * ROW_TILE)
    tq = _row_tile(t, 2 * ROW_TILE)

    lbs = jnp.cumsum(jax.nn.softmax(hg_lb_logits.astype(F32), axis=0), axis=0)
    lbs = lbs - lbs[0:1]
    w_in_p, w_gate = _pack_w_in(w_in)
    wq_p = _pack_w_uq(mla_w_uq)
    wkv_p = _pack_w_ukv(mla_w_ukv)
    vone = _v_ones()
    cos_t, sin_t = _rope_tables(positions)
    pad_h = lambda a: jnp.pad(a.astype(F32), ((0, 0), (DT_LANE, LANES - DT_LANE - SSM_HEADS)))
    dtb_p, alog_p = pad_h(ssm_dt_bias), pad_h(ssm_a_log)
    dsk_e = jnp.repeat(ssm_d.astype(F32), SSM_HEAD_DIM, axis=-1)
    bf = lambda a: a.astype(BF16)
    wa, wb, wc, wo, wu, wd = bf(w_br_hg), bf(w_br_mla), bf(w_br_ssm), bf(w_out), bf(w_up), bf(w_down)
    r1 = lambda a: a.reshape(1, -1).astype(F32)

    x2 = x.reshape(m, d)
    for l in range(depth):
        proj = _inproj(x2, r1(mix_norm[l]), w_in_p[l], tm_in, PROJ_COLS // 2)
        ya = _hgrn(proj, r1(lbs[l]), r1(hg_norm[l]), bsz, t, tq)
        q, k, v = _mla_prep(proj, cos_t, sin_t, r1(mla_q_norm[l]), r1(mla_kv_norm[l]), wq_p[l], wkv_p[l],
                            vone, tm_in)
        yb = _attn(q, k, v, bsz, t, tq)
        yc = _ssd(proj, ssm_conv_w[l].astype(F32), r1(ssm_conv_b[l]), r1(dtb_p[l]), r1(alog_p[l]),
                  r1(dsk_e[l]), r1(ssm_norm[l]), bsz, t, tq)
        x2 = _merge(ya, yb, yc, x2, r1(mix_norm[l]), w_gate[l], wa[l], wb[l], wc[l], wo[l], tm)
        x2 = _mlp(x2, r1(mlp_norm[l]), wu[l], wd[l], r1(final_norm), tm, last_layer=(l == depth - 1))
    return x2.reshape(bsz, t, d)
```
